```python
import math
import jax
import jax.numpy as jnp
from jax import lax
import numpy as np

D_MODEL = 4096
BATCH = 2
SEQ = 4096
DEPTH = 4

GRID_W = 64
CTX_LEN = 256

MIX_WIDTH = D_MODEL
W_HYENA = D_MODEL // 4
W_NA = D_MODEL // 2
W_CONF = D_MODEL // 4
NA_HEAD_DIM = 128
NA_HEADS = W_NA // NA_HEAD_DIM
NA_KH_MAX = 8
NA_KW = 16
ROPE_THETA = 10000.0

HYENA_ORDER = 2
HYENA_SHORT_K = 3
HYENA_EMB_DIM = 33
HYENA_FILTER_HIDDEN = 64
HYENA_TARGET = 1e-2
HYENA_FAST_PCT = 0.3
HYENA_SLOW_PCT = 1.5
HYENA_MIN_DECAY = math.log(HYENA_TARGET) / HYENA_SLOW_PCT
HYENA_MAX_DECAY = math.log(HYENA_TARGET) / HYENA_FAST_PCT

CONF_K = 31
FFN_HIDDEN = -(-8 * D_MODEL // (3 * 256)) * 256
MOD_RANK = 512
N_MOD = 6
DEEPNORM_ALPHA = (2 * DEPTH) ** 0.25
DEEPNORM_BETA = (8 * DEPTH) ** -0.25
LN_EPS = 1e-5

HY_COLS = 3 * W_HYENA
NA_COLS = 3 * W_NA
CF_COLS = 2 * W_CONF
OFF_NA = HY_COLS
OFF_KV = OFF_NA + W_NA
OFF_CONF = OFF_NA + NA_COLS
IN_COLS = OFF_CONF + CF_COLS

kernel_name = 'hymba_style_hyena_natten_conformer_dit'


def _layer_norm(z, g, b):
    zf = z.astype(jnp.float32)
    mu = jnp.mean(zf, axis=-1, keepdims=True)
    var = jnp.mean(jnp.square(zf - mu), axis=-1, keepdims=True)
    return ((zf - mu) * lax.rsqrt(var + LN_EPS) * g + b).astype(z.dtype)


def _modulation(cond, w_down, w_up, b):
    m = (jax.nn.silu(cond) @ w_down) @ w_up + b
    return m.reshape(cond.shape[0], N_MOD, 1, D_MODEL)


def _modulate(z, shift, scale):
    return z * (1.0 + scale) + shift


def _dwconv(u, w, b):
    k = w.shape[0]
    y = lax.conv_general_dilated(u, w[:, None, :].astype(u.dtype), window_strides=(1,),
                                 padding=[(k // 2, k // 2)], dimension_numbers=('NWC', 'WIO', 'NWC'),
                                 feature_group_count=u.shape[-1])
    return y + b


def _hyena_filters(L, w1, b1, w2, b2, w3, b3, w4, freq):
    t = jnp.linspace(0.0, 1.0, L, dtype=jnp.float32)[:, None]
    bands = (HYENA_EMB_DIM - 1) // 2
    omega = (2.0 * math.pi / L) * jnp.arange(L, dtype=jnp.float32)[:, None]
    f = jnp.linspace(1e-4, bands - 1, bands, dtype=jnp.float32)[None, :]
    emb = jnp.concatenate([t, jnp.cos(f * omega), -jnp.sin(f * omega)], axis=-1)
    hid = jnp.sin(freq * (emb @ w1 + b1))
    hid = jnp.sin(freq * (hid @ w2 + b2))
    hid = jnp.sin(freq * (hid @ w3 + b3))
    h = (hid @ w4).astype(jnp.float32).reshape(L, 2 * HYENA_ORDER, W_HYENA)
    deltas = jnp.abs(jnp.linspace(HYENA_MIN_DECAY, HYENA_MAX_DECAY, W_HYENA, dtype=jnp.float32))
    h = h * jnp.exp(-t[:, :, None] * deltas)
    return h / (jnp.sum(jnp.abs(h), axis=0, keepdims=True) + 1e-6)


def _bidir_long_conv(u, h_fwd, h_bwd, skip):
    L = u.shape[1]
    n = 2 * L
    uf = u.astype(jnp.float32)
    y_fwd = jnp.fft.irfft(jnp.fft.rfft(uf, n=n, axis=1) * jnp.fft.rfft(h_fwd, n=n, axis=0), n=n, axis=1)[:, :L]
    y_bwd = jnp.fft.irfft(jnp.fft.rfft(uf[:, ::-1], n=n, axis=1) * jnp.fft.rfft(h_bwd, n=n, axis=0),
                          n=n, axis=1)[:, :L][:, ::-1]
    return (y_fwd + y_bwd + uf * skip.astype(jnp.float32)).astype(u.dtype)


def _hyena_group(p_hy, filt, short_w, short_b, skip):
    z = _dwconv(p_hy, short_w, short_b)
    v, x1, x2 = jnp.split(z, 3, axis=-1)
    z = x1 * _bidir_long_conv(v, filt[:, 0], filt[:, 1], skip[0])
    return x2 * _bidir_long_conv(z, filt[:, 2], filt[:, 3], skip[1])


def _conformer_group(p_cf, dw_w, dw_b, g, b):
    a, gate = jnp.split(p_cf, 2, axis=-1)
    z = _dwconv(a * jax.nn.sigmoid(gate), dw_w, dw_b)
    return jax.nn.silu(_layer_norm(z, g, b))


def _split_heads(z):
    B, L, _ = z.shape
    return z.reshape(B, L, NA_HEADS, NA_HEAD_DIM).transpose(0, 2, 1, 3)


def _merge_heads(z):
    B, H, L, dh = z.shape
    return z.transpose(0, 2, 1, 3).reshape(B, L, H * dh)


def _axial_rope(L):
    t = jnp.arange(L)
    row = (t // GRID_W).astype(jnp.float32)[:, None]
    col = (t % GRID_W).astype(jnp.float32)[:, None]
    axis_dim = NA_HEAD_DIM // 2
    inv_freq = ROPE_THETA ** (-jnp.arange(0, axis_dim, 2, dtype=jnp.float32) / axis_dim)
    ang = jnp.concatenate([row * inv_freq, row * inv_freq, col * inv_freq, col * inv_freq], axis=-1)
    return jnp.cos(ang), jnp.sin(ang)


def _rotate_half(z):
    z1, z2 = jnp.split(z, 2, axis=-1)
    return jnp.concatenate([-z2, z1], axis=-1)


def _apply_rope(z, cos, sin):
    z_row, z_col = jnp.split(z, 2, axis=-1)
    rot = jnp.concatenate([_rotate_half(z_row), _rotate_half(z_col)], axis=-1)
    return (z * cos + rot * sin).astype(z.dtype)


def _neighbourhood_attention(q, k, v, k_ctx, v_ctx, rpb, cos, sin):
    B, H, S, dh = q.shape
    rows = S // GRID_W
    kh = min(NA_KH_MAX, rows)
    scale = dh ** -0.5
    q_rot = _apply_rope(q, cos, sin).reshape(B, H, rows, GRID_W, dh)
    k_rot = _apply_rope(k, cos, sin).reshape(B, H, rows, GRID_W, dh)
    v_grid = v.reshape(B, H, rows, GRID_W, dh)
    r = jnp.arange(rows)
    row_idx = jnp.clip(r - kh // 2, 0, rows - kh)[:, None] + jnp.arange(kh)[None, :]
    k_blk = k_rot[:, :, row_idx]
    v_blk = v_grid[:, :, row_idx]
    c = jnp.arange(GRID_W)
    col_start = jnp.clip(c - NA_KW // 2, 0, GRID_W - NA_KW)
    col_ok = (c[None, :] >= col_start[:, None]) & (c[None, :] < col_start[:, None] + NA_KW)
    d_row = row_idx - r[:, None] + (NA_KH_MAX - 1)
    d_col = jnp.clip(c[None, :] - c[:, None], -(NA_KW - 1), NA_KW - 1) + (NA_KW - 1)
    bias = rpb[:, d_row[:, None, :, None], d_col[None, :, None, :]].astype(jnp.float32)
    s_win = jnp.einsum('bhrqd,bhrjkd->bhrqjk', q_rot, k_blk).astype(jnp.float32) * scale + bias
    s_win = jnp.where(col_ok[:, None, :], s_win, -jnp.inf).reshape(B, H, rows, GRID_W, kh * GRID_W)
    q_grid = q.reshape(B, H, rows, GRID_W, dh)
    s_ctx = jnp.einsum('bhrqd,bhcd->bhrqc', q_grid, k_ctx).astype(jnp.float32) * scale
    p = jax.nn.softmax(jnp.concatenate([s_win, s_ctx], axis=-1), axis=-1)
    p_win = p[..., :kh * GRID_W].reshape(B, H, rows, GRID_W, kh, GRID_W).astype(v.dtype)
    p_ctx = p[..., kh * GRID_W:].astype(v.dtype)
    o = jnp.einsum('bhrqjk,bhrjkd->bhrqd', p_win, v_blk) + jnp.einsum('bhrqc,bhcd->bhrqd', p_ctx, v_ctx)
    return o.reshape(B, H, S, dh)


def _context_attention(q, k, v):
    s = jnp.einsum('bhqd,bhkd->bhqk', q, k).astype(jnp.float32) * (q.shape[-1] ** -0.5)
    p = jax.nn.softmax(s, axis=-1).astype(v.dtype)
    return jnp.einsum('bhqk,bhkd->bhqd', p, v)


def _swiglu(h, w_gate, w_up, w_down):
    return (jax.nn.silu(h @ w_gate) * (h @ w_up)) @ w_down


def setup_inputs(seed: int = 0) -> dict:
    key = jax.random.key(seed)
    ks = jax.random.split(key, 32)

    def nrm(k, shape, scale):
        return jax.random.normal(k, shape, jnp.float32) * scale

    D = D_MODEL
    Hf = HYENA_FILTER_HIDDEN
    return {
        'x': nrm(ks[0], (BATCH, SEQ, D), 1.0),
        'c': nrm(ks[1], (BATCH, D), 1.0),
        'ctx': nrm(ks[2], (BATCH, CTX_LEN, D), 1.0),
        'c_ctx': nrm(ks[3], (D,), 1.0),
        'w_mod_down': nrm(ks[4], (DEPTH, D, MOD_RANK), D ** -0.5),
        'w_mod_up': nrm(ks[5], (DEPTH, MOD_RANK, N_MOD * D), 0.5 * MOD_RANK ** -0.5),
        'b_mod': nrm(ks[6], (DEPTH, N_MOD * D), 0.02),
        'w_in': nrm(ks[7], (DEPTH, D, IN_COLS), D ** -0.5),
        'w_out': nrm(ks[8], (DEPTH, MIX_WIDTH, D), DEEPNORM_BETA * MIX_WIDTH ** -0.5),
        'hy_short_w': nrm(ks[9], (DEPTH, HYENA_SHORT_K, HY_COLS), HYENA_SHORT_K ** -0.5),
        'hy_short_b': nrm(ks[10], (DEPTH, HY_COLS), 0.02),
        'hy_filt_w1': nrm(ks[11], (DEPTH, HYENA_EMB_DIM, Hf), HYENA_EMB_DIM ** -0.5),
        'hy_filt_b1': nrm(ks[12], (DEPTH, Hf), 0.02),
        'hy_filt_w2': nrm(ks[13], (DEPTH, Hf, Hf), Hf ** -0.5),
        'hy_filt_b2': nrm(ks[14], (DEPTH, Hf), 0.02),
        'hy_filt_w3': nrm(ks[15], (DEPTH, Hf, Hf), Hf ** -0.5),
        'hy_filt_b3': nrm(ks[16], (DEPTH, Hf), 0.02),
        'hy_filt_w4': nrm(ks[17], (DEPTH, Hf, 2 * HYENA_ORDER * W_HYENA), Hf ** -0.5),
        'hy_filt_freq': 1.0 + nrm(ks[18], (DEPTH, Hf), 0.02),
        'hy_skip': nrm(ks[19], (DEPTH, HYENA_ORDER, W_HYENA), 1.0),
        'na_rpb': nrm(ks[20], (DEPTH, NA_HEADS, 2 * NA_KH_MAX - 1, 2 * NA_KW - 1), 0.1),
        'cf_dw_w': nrm(ks[21], (DEPTH, CONF_K, W_CONF), CONF_K ** -0.5),
        'cf_dw_b': nrm(ks[22], (DEPTH, W_CONF), 0.02),
        'cf_norm_g': 1.0 + nrm(ks[23], (DEPTH, W_CONF), 0.02),
        'cf_norm_b': nrm(ks[24], (DEPTH, W_CONF), 0.02),
        'w_ffn_gate': nrm(ks[25], (DEPTH, D, FFN_HIDDEN), D ** -0.5),
        'w_ffn_up': nrm(ks[26], (DEPTH, D, FFN_HIDDEN), D ** -0.5),
        'w_ffn_down': nrm(ks[27], (DEPTH, FFN_HIDDEN, D), DEEPNORM_BETA * FFN_HIDDEN ** -0.5),
        'ln_mix_g': 1.0 + nrm(ks[28], (DEPTH, D), 0.02),
        'ln_mix_b': nrm(ks[29], (DEPTH, D), 0.02),
        'ln_ffn_g': 1.0 + nrm(ks[30], (DEPTH, D), 0.02),
        'ln_ffn_b': nrm(ks[31], (DEPTH, D), 0.02),
    }


def reference(x, c, ctx, c_ctx, w_mod_down, w_mod_up, b_mod, w_in, w_out, hy_short_w, hy_short_b,
              hy_filt_w1, hy_filt_b1, hy_filt_w2, hy_filt_b2, hy_filt_w3, hy_filt_b3, hy_filt_w4,
              hy_filt_freq, hy_skip, na_rpb, cf_dw_w, cf_dw_b, cf_norm_g, cf_norm_b,
              w_ffn_gate, w_ffn_up, w_ffn_down, ln_mix_g, ln_mix_b, ln_ffn_g, ln_ffn_b):
    S = x.shape[1]
    L_ctx = ctx.shape[1]
    cos, sin = _axial_rope(S)
    for l in range(DEPTH):
        last = l == DEPTH - 1
        mod = _modulation(c, w_mod_down[l], w_mod_up[l], b_mod[l])
        mod_c = _modulation(c_ctx[None, :], w_mod_down[l], w_mod_up[l], b_mod[l])
        filt_args = (hy_filt_w1[l], hy_filt_b1[l], hy_filt_w2[l], hy_filt_b2[l],
                     hy_filt_w3[l], hy_filt_b3[l], hy_filt_w4[l], hy_filt_freq[l])

        h = _modulate(x, mod[:, 0], mod[:, 1])
        hc = _modulate(ctx, mod_c[:, 0], mod_c[:, 1])
        if last:
            kv_c = hc @ w_in[l][:, OFF_KV:OFF_CONF]
        else:
            pc = hc @ w_in[l]
            kv_c = pc[..., OFF_KV:OFF_CONF]
        k_c, v_c = [_split_heads(t) for t in jnp.split(kv_c, 2, axis=-1)]

        p = h @ w_in[l]
        y_hy = _hyena_group(p[..., :OFF_NA], _hyena_filters(S, *filt_args),
                            hy_short_w[l], hy_short_b[l], hy_skip[l])
        q, k, v = [_split_heads(t) for t in jnp.split(p[..., OFF_NA:OFF_CONF], 3, axis=-1)]
        y_na = _merge_heads(_neighbourhood_attention(q, k, v, k_c, v_c, na_rpb[l], cos, sin))
        y_cf = _conformer_group(p[..., OFF_CONF:], cf_dw_w[l], cf_dw_b[l], cf_norm_g[l], cf_norm_b[l])
        y = jnp.concatenate([y_hy, y_na, y_cf], axis=-1) @ w_out[l]
        x = _layer_norm(DEEPNORM_ALPHA * x + mod[:, 2] * y, ln_mix_g[l], ln_mix_b[l])

        if not last:
            yc_hy = _hyena_group(pc[..., :OFF_NA], _hyena_filters(L_ctx, *filt_args),
                                 hy_short_w[l], hy_short_b[l], hy_skip[l])
            q_c = _split_heads(pc[..., OFF_NA:OFF_KV])
            yc_na = _merge_heads(_context_attention(q_c, k_c, v_c))
            yc_cf = _conformer_group(pc[..., OFF_CONF:], cf_dw_w[l], cf_dw_b[l], cf_norm_g[l], cf_norm_b[l])
            yc = jnp.concatenate([yc_hy, yc_na, yc_cf], axis=-1) @ w_out[l]
            ctx = _layer_norm(DEEPNORM_ALPHA * ctx + mod_c[:, 2] * yc, ln_mix_g[l], ln_mix_b[l])
            hc = _modulate(ctx, mod_c[:, 3], mod_c[:, 4])
            fc = _swiglu(hc, w_ffn_gate[l], w_ffn_up[l], w_ffn_down[l])
            ctx = _layer_norm(DEEPNORM_ALPHA * ctx + mod_c[:, 5] * fc, ln_ffn_g[l], ln_ffn_b[l])

        h = _modulate(x, mod[:, 3], mod[:, 4])
        f = _swiglu(h, w_ffn_gate[l], w_ffn_up[l], w_ffn_down[l])
        x = _layer_norm(DEEPNORM_ALPHA * x + mod[:, 5] * f, ln_ffn_g[l], ln_ffn_b[l])
    return x
```

```python
import functools
import math

import numpy as np
import jax
import jax.numpy as jnp
from jax import lax
from jax.experimental import pallas as pl
from jax.experimental.pallas import tpu as pltpu

F32 = jnp.float32
BF16 = jnp.bfloat16

GRID_W = 64
NA_HEAD_DIM = 128
NA_KH = 8
NA_KW = 16
ROPE_THETA = 10000.0
HYENA_EMB_DIM = 33
HYENA_TARGET = 1e-2
HYENA_MIN_DECAY = math.log(HYENA_TARGET) / 1.5
HYENA_MAX_DECAY = math.log(HYENA_TARGET) / 0.3
N_MOD = 6
LN_EPS = 1e-5
MASK_VALUE = -1e30

V7X_VMEM_LIMIT_BYTES = 56 * 1024 * 1024
CONV_HALO_ROWS = 16
FFN_PAD = 1024


def _cparams(*sem):
    return pltpu.CompilerParams(dimension_semantics=sem, vmem_limit_bytes=V7X_VMEM_LIMIT_BYTES)


def _tile(dim, pref, align=128):
    t = (min(dim, pref) // align) * align
    while t >= align:
        if dim % t == 0:
            return t
        t -= align
    return dim


def _sigmoid(z):
    return jax.nn.sigmoid(z)


def _mod_body(cond_ref, wd_ref, wu_ref, b_ref, o_ref, t_ref):
    @pl.when(pl.program_id(1) == 0)
    def _():
        cnd = cond_ref[...]
        t_ref[...] = jnp.dot(cnd * _sigmoid(cnd), wd_ref[...], preferred_element_type=F32)

    o_ref[...] = jnp.dot(t_ref[...], wu_ref[...], preferred_element_type=F32) + b_ref[...]


def _modulation_all(cond8, w_down, w_up, b_mod):
    depth, d, rank = w_down.shape
    n_out = w_up.shape[2]
    tn = _tile(n_out, 2048)
    return pl.pallas_call(
        _mod_body,
        grid=(depth, n_out // tn),
        in_specs=[
            pl.BlockSpec((8, d), lambda l, j: (0, 0)),
            pl.BlockSpec((None, d, rank), lambda l, j: (l, 0, 0)),
            pl.BlockSpec((None, rank, tn), lambda l, j: (l, 0, j)),
            pl.BlockSpec((None, 1, tn), lambda l, j: (l, 0, j)),
        ],
        out_specs=pl.BlockSpec((None, 8, tn), lambda l, j: (l, 0, j)),
        out_shape=jax.ShapeDtypeStruct((depth, 8, n_out), F32),
        scratch_shapes=[pltpu.VMEM((8, rank), F32)],
        compiler_params=_cparams("arbitrary", "arbitrary"),
        name="modulation",
    )(cond8, w_down, w_up, b_mod.reshape(depth, 1, n_out))


def _modulate_body(x_ref, shift_ref, scale_ref, o_ref):
    o_ref[...] = (x_ref[...] * (1.0 + scale_ref[...]) + shift_ref[...]).astype(o_ref.dtype)


def _modulate(x3, shift, scale):
    b, s, d = x3.shape
    ts = _tile(s, 256)
    tok = pl.BlockSpec((None, ts, d), lambda bi, i: (bi, i, 0))
    vec = pl.BlockSpec((None, 1, d), lambda bi, i: (bi, 0, 0))
    return pl.pallas_call(
        _modulate_body,
        grid=(b, s // ts),
        in_specs=[tok, vec, vec],
        out_specs=tok,
        out_shape=jax.ShapeDtypeStruct((b, s, d), BF16),
        compiler_params=_cparams("parallel", "parallel"),
        name="modulate",
    )(x3, shift, scale)


def _ln_body(*refs, alpha, with_h):
    if with_h:
        x_ref, y_ref, gate_ref, g_ref, b_ref, shift_ref, scale_ref, xo_ref, ho_ref = refs
    else:
        x_ref, y_ref, gate_ref, g_ref, b_ref, xo_ref = refs
    z = alpha * x_ref[...] + gate_ref[...] * y_ref[...]
    mu = jnp.mean(z, axis=-1, keepdims=True)
    zc = z - mu
    var = jnp.mean(zc * zc, axis=-1, keepdims=True)
    xn = zc * lax.rsqrt(var + LN_EPS) * g_ref[...] + b_ref[...]
    xo_ref[...] = xn
    if with_h:
        ho_ref[...] = (xn * (1.0 + scale_ref[...]) + shift_ref[...]).astype(ho_ref.dtype)


def _ln_residual(x3, y3, gate, g, b, alpha, shift=None, scale=None):
    bsz, s, d = x3.shape
    ts = _tile(s, 256)
    with_h = shift is not None
    tok = pl.BlockSpec((None, ts, d), lambda bi, i: (bi, i, 0))
    vec = pl.BlockSpec((None, 1, d), lambda bi, i: (bi, 0, 0))
    par = pl.BlockSpec((1, d), lambda bi, i: (0, 0))
    in_specs = [tok, tok, vec, par, par]
    args = [x3, y3, gate, g.reshape(1, d), b.reshape(1, d)]
    out_specs = [tok]
    out_shape = [jax.ShapeDtypeStruct((bsz, s, d), F32)]
    if with_h:
        in_specs += [vec, vec]
        args += [shift, scale]
        out_specs.append(tok)
        out_shape.append(jax.ShapeDtypeStruct((bsz, s, d), BF16))
    res = pl.pallas_call(
        functools.partial(_ln_body, alpha=alpha, with_h=with_h),
        grid=(bsz, s // ts),
        in_specs=in_specs,
        out_specs=out_specs,
        out_shape=out_shape,
        compiler_params=_cparams("parallel", "parallel"),
        name="ln_residual",
    )(*args)
    return (res[0], res[1]) if with_h else (res[0], None)


def _mm_body(a_ref, b_ref, o_ref):
    o_ref[...] = jnp.dot(a_ref[...], b_ref[...], preferred_element_type=F32).astype(o_ref.dtype)


def _matmul(a, w, layer, *, col0=0, ncols=None, tm=1024, tn=1024, out_dtype=F32):
    m, k = a.shape
    ncols = w.shape[2] - col0 if ncols is None else ncols
    tm = _tile(m, tm)
    tn = _tile(math.gcd(ncols, col0), tn)
    jb = col0 // tn
    return pl.pallas_call(
        _mm_body,
        grid=(m // tm, ncols // tn),
        in_specs=[
            pl.BlockSpec((tm, k), lambda i, j: (i, 0)),
            pl.BlockSpec((None, k, tn), lambda i, j: (layer, 0, j + jb)),
        ],
        out_specs=pl.BlockSpec((tm, tn), lambda i, j: (i, j)),
        out_shape=jax.ShapeDtypeStruct((m, ncols), out_dtype),
        compiler_params=_cparams("parallel", "parallel"),
        name="matmul",
    )(a, w)


def _mm3_body(a1_ref, a2_ref, a3_ref, w_ref, o_ref):
    c1 = a1_ref.shape[1]
    c2 = a2_ref.shape[1]
    acc = jnp.dot(a1_ref[...], w_ref[0:c1, :], preferred_element_type=F32)
    acc += jnp.dot(a2_ref[...], w_ref[c1:c1 + c2, :], preferred_element_type=F32)
    acc += jnp.dot(a3_ref[...], w_ref[c1 + c2:, :], preferred_element_type=F32)
    o_ref[...] = acc


def _matmul_concat3(a1, a2, a3, w, layer, *, tm=1024, tn=1024):
    m = a1.shape[0]
    k, n = w.shape[1], w.shape[2]
    assert a1.shape[1] + a2.shape[1] + a3.shape[1] == k
    tm = _tile(m, tm)
    tn = _tile(n, tn)
    return pl.pallas_call(
        _mm3_body,
        grid=(m // tm, n // tn),
        in_specs=[
            pl.BlockSpec((tm, a1.shape[1]), lambda i, j: (i, 0)),
            pl.BlockSpec((tm, a2.shape[1]), lambda i, j: (i, 0)),
            pl.BlockSpec((tm, a3.shape[1]), lambda i, j: (i, 0)),
            pl.BlockSpec((None, k, tn), lambda i, j: (layer, 0, j)),
        ],
        out_specs=pl.BlockSpec((tm, tn), lambda i, j: (i, j)),
        out_shape=jax.ShapeDtypeStruct((m, n), F32),
        compiler_params=_cparams("parallel", "parallel"),
        name="matmul_out",
    )(a1, a2, a3, w)


def _gu_body(a_ref, wg_ref, wu_ref, o_ref):
    a = a_ref[...]
    g = jnp.dot(a, wg_ref[...], preferred_element_type=F32)
    u = jnp.dot(a, wu_ref[...], preferred_element_type=F32)
    o_ref[...] = (g * _sigmoid(g) * u).astype(o_ref.dtype)


def _matmul_swiglu_in(a, wg, wu, layer, *, tm=1024, tn=512):
    m, k = a.shape
    n = wg.shape[2]
    tm = _tile(m, tm)
    tn = _tile(n, tn)
    wspec = pl.BlockSpec((None, k, tn), lambda i, j: (layer, 0, j))
    return pl.pallas_call(
        _gu_body,
        grid=(m // tm, n // tn),
        in_specs=[pl.BlockSpec((tm, k), lambda i, j: (i, 0)), wspec, wspec],
        out_specs=pl.BlockSpec((tm, tn), lambda i, j: (i, j)),
        out_shape=jax.ShapeDtypeStruct((m, n), BF16),
        compiler_params=_cparams("parallel", "parallel"),
        name="matmul_swiglu_in",
    )(a, wg, wu)


def _mmk_body(a_ref, b_ref, o_ref, acc_ref):
    kk = pl.program_id(2)
    prod = jnp.dot(a_ref[...], b_ref[...], preferred_element_type=F32)

    @pl.when(kk == 0)
    def _():
        acc_ref[...] = prod

    @pl.when(kk > 0)
    def _():
        acc_ref[...] += prod

    @pl.when(kk == pl.num_programs(2) - 1)
    def _():
        o_ref[...] = acc_ref[...].astype(o_ref.dtype)


def _matmul_kgrid(a, w, layer, *, tm=2048, tn=1024, tk=1024):
    m, k = a.shape
    n = w.shape[2]
    tm = _tile(m, tm)
    tn = _tile(n, tn)
    tk = _tile(k, tk)
    return pl.pallas_call(
        _mmk_body,
        grid=(m // tm, n // tn, k // tk),
        in_specs=[
            pl.BlockSpec((tm, tk), lambda i, j, kk: (i, kk)),
            pl.BlockSpec((None, tk, tn), lambda i, j, kk: (layer, kk, j)),
        ],
        out_specs=pl.BlockSpec((tm, tn), lambda i, j, kk: (i, j)),
        out_shape=jax.ShapeDtypeStruct((m, n), F32),
        scratch_shapes=[pltpu.VMEM((tm, tn), F32)],
        compiler_params=_cparams("parallel", "parallel", "arbitrary"),
        name="matmul_kgrid",
    )(a, w)


def _rope_tables(s):
    t = jnp.arange(s)
    row = (t // GRID_W).astype(F32)[:, None]
    col = (t % GRID_W).astype(F32)[:, None]
    axis_dim = NA_HEAD_DIM // 2
    inv_freq = ROPE_THETA ** (-jnp.arange(0, axis_dim, 2, dtype=F32) / axis_dim)
    ang = jnp.concatenate([row * inv_freq, row * inv_freq, col * inv_freq, col * inv_freq], axis=-1)
    quarter = NA_HEAD_DIM // 4
    negate = (jnp.arange(NA_HEAD_DIM) % (2 * quarter)) < quarter
    sin = jnp.sin(ang)
    return jnp.cos(ang), jnp.where(negate[None, :], -sin, sin)


def _na_bias_classes(rpb):
    c = np.arange(NA_KH)[:, None]
    j = np.arange(NA_KH)[None, :]
    d_row = j - c + (NA_KH - 1)
    cq = np.arange(GRID_W)[:, None]
    ck = np.arange(GRID_W)[None, :]
    d_col = np.clip(ck - cq, -(NA_KW - 1), NA_KW - 1) + (NA_KW - 1)
    col_start = np.clip(cq - NA_KW // 2, 0, GRID_W - NA_KW)
    ok = (ck >= col_start) & (ck < col_start + NA_KW)
    bias = rpb[:, d_row[:, None, :, None], d_col[None, :, None, :]].astype(F32)
    bias = jnp.where(ok[None, None, :, None, :], bias, MASK_VALUE)
    return bias.reshape(rpb.shape[0], NA_KH, GRID_W, NA_KH * GRID_W)


def _na_body(q_ref, k_ref, v_ref, kc_ref, vc_ref, bias_ref, cos_ref, sin_ref, o_ref,
             qr_s, kr_s, qb_s, vb_s, *, rows, scale):
    cos = cos_ref[...]
    sin = sin_ref[...]
    quarter = NA_HEAD_DIM // 4
    lane = lax.broadcasted_iota(jnp.int32, cos.shape, 1)
    first = (lane % (2 * quarter)) < quarter

    def rope(z):
        rot = jnp.where(first, pltpu.roll(z, NA_HEAD_DIM - quarter, 1), pltpu.roll(z, quarter, 1))
        return z * cos + rot * sin

    q = q_ref[...]
    qr_s[...] = rope(q).astype(BF16)
    qb_s[...] = q.astype(BF16)
    kr_s[...] = rope(k_ref[...]).astype(BF16)
    vb_s[...] = v_ref[...].astype(BF16)
    kc = kc_ref[...].astype(BF16)
    vc = vc_ref[...].astype(BF16)
    nt = (((1,), (1,)), ((), ()))
    win = NA_KH * GRID_W

    def body(r, carry):
        start = jnp.clip(r - NA_KH // 2, 0, rows - NA_KH)
        cls = r - start
        q0 = pl.multiple_of(r * GRID_W, GRID_W)
        k0 = pl.multiple_of(start * GRID_W, GRID_W)
        s_win = lax.dot_general(qr_s[pl.ds(q0, GRID_W), :], kr_s[pl.ds(k0, win), :], nt,
                                preferred_element_type=F32) * scale + bias_ref[cls]
        s_ctx = lax.dot_general(qb_s[pl.ds(q0, GRID_W), :], kc, nt, preferred_element_type=F32) * scale
        m = jnp.maximum(jnp.max(s_win, axis=-1, keepdims=True), jnp.max(s_ctx, axis=-1, keepdims=True))
        e_win = jnp.exp(s_win - m)
        e_ctx = jnp.exp(s_ctx - m)
        den = jnp.sum(e_win, axis=-1, keepdims=True) + jnp.sum(e_ctx, axis=-1, keepdims=True)
        o = jnp.dot(e_win.astype(BF16), vb_s[pl.ds(k0, win), :], preferred_element_type=F32)
        o += jnp.dot(e_ctx.astype(BF16), vc, preferred_element_type=F32)
        o_ref[pl.ds(q0, GRID_W), :] = (o / den).astype(o_ref.dtype)
        return carry

    lax.fori_loop(0, rows, body, 0)


def _neighbourhood_attention(p3, kvc3, bias_cls, cos, sin, *, off_q, off_k, off_v, off_kc, off_vc, heads):
    bsz, s, _ = p3.shape
    lc = kvc3.shape[1]
    rows = s // GRID_W
    assert s % GRID_W == 0 and rows >= NA_KH
    dh = NA_HEAD_DIM
    win = NA_KH * GRID_W

    def col(off):
        return pl.BlockSpec((None, s, dh), lambda b, h: (b, 0, off + h))

    def colc(off):
        return pl.BlockSpec((None, lc, dh), lambda b, h: (b, 0, off + h))

    tab = pl.BlockSpec((s, dh), lambda b, h: (0, 0))
    return pl.pallas_call(
        functools.partial(_na_body, rows=rows, scale=dh ** -0.5),
        grid=(bsz, heads),
        in_specs=[col(off_q), col(off_k), col(off_v), colc(off_kc), colc(off_vc),
                  pl.BlockSpec((None, NA_KH, GRID_W, win), lambda b, h: (h, 0, 0, 0)), tab, tab],
        out_specs=pl.BlockSpec((None, s, dh), lambda b, h: (b, 0, h)),
        out_shape=jax.ShapeDtypeStruct((bsz, s, heads * dh), BF16),
        scratch_shapes=[pltpu.VMEM((s, dh), BF16)] * 4,
        compiler_params=_cparams("parallel", "parallel"),
        name="neighbourhood_attention",
    )(p3, p3, p3, kvc3, kvc3, bias_cls, cos, sin)


def _ctx_attn_body(q_ref, k_ref, v_ref, o_ref, *, scale):
    nt = (((1,), (1,)), ((), ()))
    s = lax.dot_general(q_ref[...].astype(BF16), k_ref[...].astype(BF16), nt, preferred_element_type=F32) * scale
    m = jnp.max(s, axis=-1, keepdims=True)
    e = jnp.exp(s - m)
    den = jnp.sum(e, axis=-1, keepdims=True)
    o = jnp.dot(e.astype(BF16), v_ref[...].astype(BF16), preferred_element_type=F32)
    o_ref[...] = (o / den).astype(o_ref.dtype)


def _context_attention(pc3, *, off_q, off_k, off_v, heads):
    bsz, lc, _ = pc3.shape
    dh = NA_HEAD_DIM

    def col(off):
        return pl.BlockSpec((None, lc, dh), lambda b, h: (b, 0, off + h))

    return pl.pallas_call(
        functools.partial(_ctx_attn_body, scale=dh ** -0.5),
        grid=(bsz, heads),
        in_specs=[col(off_q), col(off_k), col(off_v)],
        out_specs=pl.BlockSpec((None, lc, dh), lambda b, h: (b, 0, h)),
        out_shape=jax.ShapeDtypeStruct((bsz, lc, heads * dh), BF16),
        compiler_params=_cparams("parallel", "parallel"),
        name="context_attention",
    )(pc3, pc3, pc3)


def _conformer_body(a_ref, g_ref, pa_ref, pg_ref, na_ref, ng_ref, w_ref, wb_ref, lg_ref, lb_ref, o_ref, u_s,
                    *, ts, taps):
    i = pl.program_id(1)
    halo = CONV_HALO_ROWS
    a = a_ref[...]
    g = g_ref[...]
    u_s[halo:halo + ts, :] = a * _sigmoid(g)
    prev = pa_ref[...] * _sigmoid(pg_ref[...])
    u_s[0:halo, :] = jnp.where(i > 0, prev, 0.0)
    nxt = na_ref[...] * _sigmoid(ng_ref[...])
    u_s[halo + ts:, :] = jnp.where(i < pl.num_programs(1) - 1, nxt, 0.0)
    base = halo - taps // 2
    z = wb_ref[...] + w_ref[0:1, :] * u_s[base:base + ts, :]
    for t in range(1, taps):
        z += w_ref[t:t + 1, :] * u_s[base + t:base + t + ts, :]
    mu = jnp.mean(z, axis=-1, keepdims=True)
    zc = z - mu
    var = jnp.mean(zc * zc, axis=-1, keepdims=True)
    zn = zc * lax.rsqrt(var + LN_EPS) * lg_ref[...] + lb_ref[...]
    o_ref[...] = (zn * _sigmoid(zn)).astype(o_ref.dtype)


def _conformer(p3, dw_w, dw_b, ln_g, ln_b, *, off):
    bsz, l, _ = p3.shape
    taps, c = dw_w.shape
    halo = CONV_HALO_ROWS
    assert taps // 2 < halo and l % halo == 0
    ts = _tile(l, 256)
    hb = ts // halo
    nh = l // halo

    def cur(o):
        return pl.BlockSpec((None, ts, c), lambda b, i: (b, i, o))

    def prev(o):
        return pl.BlockSpec((None, halo, c), lambda b, i: (b, jnp.maximum(i * hb - 1, 0), o))

    def nxt(o):
        return pl.BlockSpec((None, halo, c), lambda b, i: (b, jnp.minimum((i + 1) * hb, nh - 1), o))

    par = pl.BlockSpec((1, c), lambda b, i: (0, 0))
    return pl.pallas_call(
        functools.partial(_conformer_body, ts=ts, taps=taps),
        grid=(bsz, l // ts),
        in_specs=[cur(off), cur(off + 1), prev(off), prev(off + 1), nxt(off), nxt(off + 1),
                  pl.BlockSpec((taps, c), lambda b, i: (0, 0)), par, par, par],
        out_specs=pl.BlockSpec((None, ts, c), lambda b, i: (b, i, 0)),
        out_shape=jax.ShapeDtypeStruct((bsz, l, c), BF16),
        scratch_shapes=[pltpu.VMEM((ts + 2 * halo, c), F32)],
        compiler_params=_cparams("parallel", "arbitrary"),
        name="conformer",
    )(p3, p3, p3, p3, p3, p3, dw_w, dw_b.reshape(1, c), ln_g.reshape(1, c), ln_b.reshape(1, c))


def _hy_short_body(cur_ref, prev_ref, next_ref, w_ref, b_ref, v_ref, x1_ref, x2_ref, u_s, *, ts, c):
    i = pl.program_id(1)
    halo = CONV_HALO_ROWS
    u_s[halo:halo + ts, :] = cur_ref[...]
    u_s[0:halo, :] = jnp.where(i > 0, prev_ref[...], 0.0)
    u_s[halo + ts:, :] = jnp.where(i < pl.num_programs(1) - 1, next_ref[...], 0.0)
    z = (b_ref[...] + w_ref[0:1, :] * u_s[halo - 1:halo - 1 + ts, :] + w_ref[1:2, :] * u_s[halo:halo + ts, :]
         + w_ref[2:3, :] * u_s[halo + 1:halo + 1 + ts, :])
    v_ref[...] = z[:, 0:c].astype(v_ref.dtype)
    x1_ref[...] = z[:, c:2 * c]
    x2_ref[...] = z[:, 2 * c:3 * c]


def _hyena_short_conv(p3, short_w, short_b, c):
    bsz, l, _ = p3.shape
    halo = CONV_HALO_ROWS
    ts = _tile(l, 256)
    hb = ts // halo
    nh = l // halo
    w3 = 3 * c
    out = pl.BlockSpec((ts, c), lambda b, i: (i, b))
    return pl.pallas_call(
        functools.partial(_hy_short_body, ts=ts, c=c),
        grid=(bsz, l // ts),
        in_specs=[
            pl.BlockSpec((None, ts, w3), lambda b, i: (b, i, 0)),
            pl.BlockSpec((None, halo, w3), lambda b, i: (b, jnp.maximum(i * hb - 1, 0), 0)),
            pl.BlockSpec((None, halo, w3), lambda b, i: (b, jnp.minimum((i + 1) * hb, nh - 1), 0)),
            pl.BlockSpec((3, w3), lambda b, i: (0, 0)),
            pl.BlockSpec((1, w3), lambda b, i: (0, 0)),
        ],
        out_specs=[out, out, out],
        out_shape=[jax.ShapeDtypeStruct((l, bsz * c), BF16), jax.ShapeDtypeStruct((l, bsz * c), F32),
                   jax.ShapeDtypeStruct((l, bsz * c), F32)],
        scratch_shapes=[pltpu.VMEM((ts + 2 * halo, w3), F32)],
        compiler_params=_cparams("parallel", "arbitrary"),
        name="hyena_short_conv",
    )(p3, p3, p3, short_w, short_b.reshape(1, w3))


def _hyena_embedding(l):
    t = jnp.linspace(0.0, 1.0, l, dtype=F32)[:, None]
    bands = (HYENA_EMB_DIM - 1) // 2
    omega = (2.0 * math.pi / l) * jnp.arange(l, dtype=F32)[:, None]
    f = jnp.linspace(1e-4, bands - 1, bands, dtype=F32)[None, :]
    return jnp.concatenate([t, jnp.cos(f * omega), -jnp.sin(f * omega)], axis=-1)


def _hy_filter_body(emb_ref, w1_ref, b1_ref, w2_ref, b2_ref, w3_ref, b3_ref, freq_ref, w4_ref, delta_ref, o_ref):
    emb = emb_ref[...]
    freq = freq_ref[...]
    hid = jnp.sin(freq * (jnp.dot(emb, w1_ref[...], preferred_element_type=F32) + b1_ref[...]))
    hid = jnp.sin(freq * (jnp.dot(hid, w2_ref[...], preferred_element_type=F32) + b2_ref[...]))
    hid = jnp.sin(freq * (jnp.dot(hid, w3_ref[...], preferred_element_type=F32) + b3_ref[...]))
    h = jnp.dot(hid, w4_ref[...], preferred_element_type=F32)
    h = h * jnp.exp(-emb[:, 0:1] * delta_ref[...])
    h = h / (jnp.sum(jnp.abs(h), axis=0, keepdims=True) + 1e-6)
    o_ref[...] = h.astype(o_ref.dtype)


def _hyena_filters(emb, w1, b1, w2, b2, w3, b3, w4, freq, deltas4):
    l, e = emb.shape
    hf = w1.shape[1]
    n4 = w4.shape[1]
    tn = _tile(n4, 512)

    def full(shape):
        return pl.BlockSpec(shape, lambda j: (0, 0))

    return pl.pallas_call(
        _hy_filter_body,
        grid=(n4 // tn,),
        in_specs=[full((l, e)), full((e, hf)), full((1, hf)), full((hf, hf)), full((1, hf)), full((hf, hf)),
                  full((1, hf)), full((1, hf)),
                  pl.BlockSpec((hf, tn), lambda j: (0, j)), pl.BlockSpec((1, tn), lambda j: (0, j))],
        out_specs=pl.BlockSpec((l, tn), lambda j: (0, j)),
        out_shape=jax.ShapeDtypeStruct((l, n4), BF16),
        compiler_params=_cparams("parallel"),
        name="hyena_filters",
    )(emb, w1, b1.reshape(1, hf), w2, b2.reshape(1, hf), w3, b3.reshape(1, hf), freq.reshape(1, hf), w4, deltas4)


def _dft_table(l, tm):
    n = 2 * l
    half = tm // 2
    r = np.arange(n)
    f_np = (r // tm) * half + (r % half)
    is_im_np = (r % tm) >= half
    sq = 1 << (int(math.log2(l)) // 2)
    hi = l // sq
    f = jnp.asarray(f_np, jnp.int32)[:, None]
    w0 = 2.0 * math.pi / n
    a_hi = ((f * (jnp.arange(hi, dtype=jnp.int32)[None, :] * sq)) % n).astype(F32) * w0
    a_lo = ((f * jnp.arange(sq, dtype=jnp.int32)[None, :]) % n).astype(F32) * w0
    ch, sh = jnp.cos(a_hi)[:, :, None], jnp.sin(a_hi)[:, :, None]
    cl, sl = jnp.cos(a_lo)[:, None, :], jnp.sin(a_lo)[:, None, :]
    cosv = (ch * cl - sh * sl).reshape(n, l)
    sinv = (sh * cl + ch * sl).reshape(n, l)
    nyq = jnp.asarray(np.where(np.arange(l) % 2 == 0, 1.0, -1.0), F32)[None, :]
    is_im = jnp.asarray(is_im_np)[:, None]
    table = jnp.where(is_im, jnp.where(f == 0, nyq, -sinv), cosv)
    return table.astype(BF16)


def _spec_body(a_ref, hf_ref, hb_ref, skip_ref, gr_ref, gi_ref, *, n):
    i = pl.program_id(0)
    a = a_ref[...]
    half = a.shape[0] // 2
    ff = jnp.dot(a, hf_ref[...], preferred_element_type=F32)
    fb = jnp.dot(a, hb_ref[...], preferred_element_type=F32)
    skip = skip_ref[...]
    row = lax.broadcasted_iota(jnp.int32, (half, 1), 0)
    is_dc = jnp.logical_and(row == 0, i == 0)
    sc = jnp.where(is_dc, 1.0 / n, 2.0 / n)
    gr = ff[:half] + fb[:half] + skip
    gi = jnp.where(is_dc, ff[half:] + fb[half:] + skip, ff[half:] - fb[half:])
    gr_ref[...] = gr * sc
    gi_ref[...] = gi * sc


def _hyena_filter_spectrum(table, filt, skip, c, tm):
    n, l = table.shape
    order = skip.shape[0]
    tn = _tile(c, 512)
    cb = c // tn
    half = tm // 2
    out = pl.BlockSpec((None, half, tn), lambda i, o, j: (o, i, j))
    return pl.pallas_call(
        functools.partial(_spec_body, n=n),
        grid=(n // tm, order, cb),
        in_specs=[
            pl.BlockSpec((tm, l), lambda i, o, j: (i, 0)),
            pl.BlockSpec((l, tn), lambda i, o, j: (0, (2 * o) * cb + j)),
            pl.BlockSpec((l, tn), lambda i, o, j: (0, (2 * o + 1) * cb + j)),
            pl.BlockSpec((None, 1, tn), lambda i, o, j: (o, 0, j)),
        ],
        out_specs=[out, out],
        out_shape=[jax.ShapeDtypeStruct((order, l, c), F32)] * 2,
        compiler_params=_cparams("parallel", "arbitrary", "arbitrary"),
        name="hyena_filter_spectrum",
    )(table, filt, filt, skip.reshape(order, 1, c))


def _fwd_body(a_ref, u_ref, gr_ref, gi_ref, o_ref):
    i = pl.program_id(0)
    half = a_ref.shape[0] // 2
    z = jnp.dot(a_ref[...], u_ref[...], preferred_element_type=F32)
    zr, zi = z[:half], z[half:]
    gr, gi = gr_ref[...], gi_ref[...]
    row = lax.broadcasted_iota(jnp.int32, (half, 1), 0)
    is_dc = jnp.logical_and(row == 0, i == 0)
    yr = jnp.where(is_dc, zr * gr, zr * gr - zi * gi)
    yi = jnp.where(is_dc, zi * gi, zr * gi + zi * gr)
    o_ref[:half, :] = yr.astype(o_ref.dtype)
    o_ref[half:, :] = yi.astype(o_ref.dtype)


def _hyena_fwd_mul(table, u, gr, gi, order, c, tm):
    n, l = table.shape
    ncol = u.shape[1]
    tn = _tile(c, 1024)
    cb = c // tn
    half = tm // 2
    gspec = pl.BlockSpec((None, half, tn), lambda i, j: (order, i, j % cb))
    return pl.pallas_call(
        _fwd_body,
        grid=(n // tm, ncol // tn),
        in_specs=[pl.BlockSpec((tm, l), lambda i, j: (i, 0)), pl.BlockSpec((l, tn), lambda i, j: (0, j)),
                  gspec, gspec],
        out_specs=pl.BlockSpec((tm, tn), lambda i, j: (i, j)),
        out_shape=jax.ShapeDtypeStruct((n, ncol), BF16),
        compiler_params=_cparams("parallel", "arbitrary"),
        name="hyena_fwd_dft",
    )(table, u, gr, gi)


def _inv_body(a_ref, y_ref, x_ref, o_ref, acc_ref):
    kk = pl.program_id(2)
    tn_dims = (((0,), (0,)), ((), ()))
    prod = lax.dot_general(a_ref[...], y_ref[...], tn_dims, preferred_element_type=F32)

    @pl.when(kk == 0)
    def _():
        acc_ref[...] = prod

    @pl.when(kk > 0)
    def _():
        acc_ref[...] += prod

    @pl.when(kk == pl.num_programs(2) - 1)
    def _():
        o_ref[...] = (x_ref[...] * acc_ref[...]).astype(o_ref.dtype)


def _hyena_inv_gate(table, y, xg, c, *, token_major):
    n, l = table.shape
    ncol = y.shape[1]
    tmt = _tile(l, 1024)
    tk = _tile(n, 2048)
    tn = c if token_major else _tile(c, 1024)
    if token_major:
        out_spec = pl.BlockSpec((None, tmt, c), lambda i, j, kk: (j, i, 0))
        out_shape = jax.ShapeDtypeStruct((ncol // c, l, c), BF16)
    else:
        out_spec = pl.BlockSpec((tmt, tn), lambda i, j, kk: (i, j))
        out_shape = jax.ShapeDtypeStruct((l, ncol), BF16)
    return pl.pallas_call(
        _inv_body,
        grid=(l // tmt, ncol // tn, n // tk),
        in_specs=[pl.BlockSpec((tk, tmt), lambda i, j, kk: (kk, i)),
                  pl.BlockSpec((tk, tn), lambda i, j, kk: (kk, j)),
                  pl.BlockSpec((tmt, tn), lambda i, j, kk: (i, j))],
        out_specs=out_spec,
        out_shape=out_shape,
        scratch_shapes=[pltpu.VMEM((tmt, tn), F32)],
        compiler_params=_cparams("parallel", "parallel", "arbitrary"),
        name="hyena_inv_dft",
    )(table, y, xg)


def _hyena_group(p3, table, tm, emb, filt_w, short_w, short_b, skip, deltas4):
    c = skip.shape[1]
    filt = _hyena_filters(emb, *filt_w, deltas4)
    gr, gi = _hyena_filter_spectrum(table, filt, skip, c, tm)
    v, x1, x2 = _hyena_short_conv(p3, short_w, short_b, c)
    y1 = _hyena_fwd_mul(table, v, gr, gi, 0, c, tm)
    z1 = _hyena_inv_gate(table, y1, x1, c, token_major=False)
    y2 = _hyena_fwd_mul(table, z1, gr, gi, 1, c, tm)
    return _hyena_inv_gate(table, y2, x2, c, token_major=True)


def kernel(x, c, ctx, c_ctx, w_mod_down, w_mod_up, b_mod, w_in, w_out, hy_short_w, hy_short_b, hy_filt_w1, hy_filt_b1, hy_filt_w2, hy_filt_b2, hy_filt_w3, hy_filt_b3, hy_filt_w4, hy_filt_freq, hy_skip, na_rpb, cf_dw_w, cf_dw_b, cf_norm_g, cf_norm_b, w_ffn_gate, w_ffn_up, w_ffn_down, ln_mix_g, ln_mix_b, ln_ffn_g, ln_ffn_b):
    bsz, s, d = x.shape
    lc = ctx.shape[1]
    depth = w_in.shape[0]
    in_cols = w_in.shape[2]
    w_hy = hy_skip.shape[2]
    w_cf = cf_dw_w.shape[2]
    w_na = (in_cols - 3 * w_hy - 2 * w_cf) // 3
    heads = w_na // NA_HEAD_DIM
    off_na = 3 * w_hy
    off_kv = off_na + w_na
    off_conf = off_na + 3 * w_na
    alpha = (2 * depth) ** 0.25
    dh = NA_HEAD_DIM
    assert bsz + 1 <= 8 and off_conf % w_cf == 0 and off_na % dh == 0

    ffn_h = w_ffn_gate.shape[2]
    pad = (-ffn_h) % FFN_PAD
    w_in_b = w_in.astype(BF16)
    w_out_b = w_out.astype(BF16)
    wg_b = jnp.pad(w_ffn_gate.astype(BF16), ((0, 0), (0, 0), (0, pad)))
    wu_b = jnp.pad(w_ffn_up.astype(BF16), ((0, 0), (0, 0), (0, pad)))
    wd_b = jnp.pad(w_ffn_down.astype(BF16), ((0, 0), (0, pad), (0, 0)))

    cos, sin = _rope_tables(s)
    tm_lat = min(1024, 2 * s)
    tm_ctx = min(1024, 2 * lc)
    table_lat = _dft_table(s, tm_lat)
    table_ctx = _dft_table(lc, tm_ctx)
    emb_lat = _hyena_embedding(s)
    emb_ctx = _hyena_embedding(lc)
    deltas = jnp.abs(jnp.linspace(HYENA_MIN_DECAY, HYENA_MAX_DECAY, w_hy, dtype=F32))
    deltas4 = jnp.tile(deltas, 4)[None, :]

    cond8 = jnp.zeros((8, d), F32).at[:bsz].set(c).at[bsz].set(c_ctx)
    mods = _modulation_all(cond8, w_mod_down, w_mod_up, b_mod).reshape(depth, 8, N_MOD, d)

    def mod_lat(l, i):
        return mods[l, :bsz, i][:, None, :]

    def mod_ctx(l, i):
        return jnp.broadcast_to(mods[l, bsz, i][None, None, :], (bsz, 1, d))

    h = _modulate(x, mod_lat(0, 0), mod_lat(0, 1))
    hc = _modulate(ctx, mod_ctx(0, 0), mod_ctx(0, 1))

    for l in range(depth):
        last = l == depth - 1
        filt_w = (hy_filt_w1[l], hy_filt_b1[l], hy_filt_w2[l], hy_filt_b2[l], hy_filt_w3[l], hy_filt_b3[l],
                  hy_filt_w4[l], hy_filt_freq[l])
        bias_cls = _na_bias_classes(na_rpb[l])

        hc2 = hc.reshape(bsz * lc, d)
        if last:
            kvc = _matmul(hc2, w_in_b, l, col0=off_kv, ncols=2 * w_na).reshape(bsz, lc, 2 * w_na)
            off_kc, off_vc = 0, heads
        else:
            pc = _matmul(hc2, w_in_b, l).reshape(bsz, lc, in_cols)
            kvc = pc
            off_kc, off_vc = off_kv // dh, (off_kv + w_na) // dh

        p = _matmul(h.reshape(bsz * s, d), w_in_b, l).reshape(bsz, s, in_cols)
        y_hy = _hyena_group(p, table_lat, tm_lat, emb_lat, filt_w, hy_short_w[l], hy_short_b[l], hy_skip[l], deltas4)
        y_na = _neighbourhood_attention(p, kvc, bias_cls, cos, sin, off_q=off_na // dh, off_k=off_kv // dh,
                                        off_v=(off_kv + w_na) // dh, off_kc=off_kc, off_vc=off_vc, heads=heads)
        y_cf = _conformer(p, cf_dw_w[l], cf_dw_b[l], cf_norm_g[l], cf_norm_b[l], off=off_conf // w_cf)
        y = _matmul_concat3(y_hy.reshape(bsz * s, w_hy), y_na.reshape(bsz * s, w_na), y_cf.reshape(bsz * s, w_cf),
                            w_out_b, l).reshape(bsz, s, d)
        x, h = _ln_residual(x, y, mod_lat(l, 2), ln_mix_g[l], ln_mix_b[l], alpha, mod_lat(l, 3), mod_lat(l, 4))

        if not last:
            yc_hy = _hyena_group(pc, table_ctx, tm_ctx, emb_ctx, filt_w, hy_short_w[l], hy_short_b[l], hy_skip[l],
                                 deltas4)
            yc_na = _context_attention(pc, off_q=off_na // dh, off_k=off_kv // dh, off_v=(off_kv + w_na) // dh,
                                       heads=heads)
            yc_cf = _conformer(pc, cf_dw_w[l], cf_dw_b[l], cf_norm_g[l], cf_norm_b[l], off=off_conf // w_cf)
            yc = _matmul_concat3(yc_hy.reshape(bsz * lc, w_hy), yc_na.reshape(bsz * lc, w_na),
                                 yc_cf.reshape(bsz * lc, w_cf), w_out_b, l).reshape(bsz, lc, d)
            ctx, hc = _ln_residual(ctx, yc, mod_ctx(l, 2), ln_mix_g[l], ln_mix_b[l], alpha, mod_ctx(l, 3),
                                   mod_ctx(l, 4))
            ac = _matmul_swiglu_in(hc.reshape(bsz * lc, d), wg_b, wu_b, l)
            fc = _matmul_kgrid(ac, wd_b, l).reshape(bsz, lc, d)
            ctx, hc = _ln_residual(ctx, fc, mod_ctx(l, 5), ln_ffn_g[l], ln_ffn_b[l], alpha, mod_ctx(l + 1, 0),
                                   mod_ctx(l + 1, 1))

        a = _matmul_swiglu_in(h.reshape(bsz * s, d), wg_b, wu_b, l)
        f = _matmul_kgrid(a, wd_b, l).reshape(bsz, s, d)
        if last:
            x, _ = _ln_residual(x, f, mod_lat(l, 5), ln_ffn_g[l], ln_ffn_b[l], alpha)
        else:
            x, h = _ln_residual(x, f, mod_lat(l, 5), ln_ffn_g[l], ln_ffn_b[l], alpha, mod_lat(l + 1, 0),
                                mod_lat(l + 1, 1))
    return x
```

```python
import functools
import math

import numpy as np
import jax
import jax.numpy as jnp
from jax import lax
from jax.experimental import pallas as pl
from jax.experimental.pallas import tpu as pltpu

F32 = jnp.float32
BF16 = jnp.bfloat16

GRID_W = 64
NA_HEAD_DIM = 128
NA_KH = 8
NA_KW = 16
NA_QROWS = 4
ROPE_THETA = 10000.0
HYENA_EMB_DIM = 33
HYENA_TARGET = 1e-2
HYENA_MIN_DECAY = math.log(HYENA_TARGET) / 1.5
HYENA_MAX_DECAY = math.log(HYENA_TARGET) / 0.3
N_MOD = 6
LN_EPS = 1e-5
MASK_VALUE = -1e30

V7X_VMEM_LIMIT_BYTES = 56 * 1024 * 1024
V7X_VMEM_LIMIT_DOWN_BYTES = 60 * 1024 * 1024
CONV_HALO_ROWS = 16


def _cparams(*sem, vmem=V7X_VMEM_LIMIT_BYTES):
    return pltpu.CompilerParams(dimension_semantics=sem, vmem_limit_bytes=vmem)


def _tile(dim, pref, align=128):
    t = (min(dim, pref) // align) * align
    while t >= align:
        if dim % t == 0:
            return t
        t -= align
    return dim


def _sigmoid(z):
    return jax.nn.sigmoid(z)


def _mod_body(cond_ref, wd_ref, wu_ref, b_ref, o_ref, t_ref):
    @pl.when(pl.program_id(1) == 0)
    def _():
        cnd = cond_ref[...]
        t_ref[...] = jnp.dot(cnd * _sigmoid(cnd), wd_ref[...], preferred_element_type=F32)

    o_ref[...] = jnp.dot(t_ref[...], wu_ref[...], preferred_element_type=F32) + b_ref[...]


def _modulation_all(cond8, w_down, w_up, b_mod):
    depth, d, rank = w_down.shape
    n_out = w_up.shape[2]
    tn = _tile(n_out, 2048)
    return pl.pallas_call(
        _mod_body,
        grid=(depth, n_out // tn),
        in_specs=[
            pl.BlockSpec((8, d), lambda l, j: (0, 0)),
            pl.BlockSpec((None, d, rank), lambda l, j: (l, 0, 0)),
            pl.BlockSpec((None, rank, tn), lambda l, j: (l, 0, j)),
            pl.BlockSpec((None, 1, tn), lambda l, j: (l, 0, j)),
        ],
        out_specs=pl.BlockSpec((None, 8, tn), lambda l, j: (l, 0, j)),
        out_shape=jax.ShapeDtypeStruct((depth, 8, n_out), F32),
        scratch_shapes=[pltpu.VMEM((8, rank), F32)],
        compiler_params=_cparams("arbitrary", "arbitrary"),
        name="modulation",
    )(cond8, w_down, w_up, b_mod.reshape(depth, 1, n_out))


def _modulate_body(x_ref, shift_ref, scale_ref, o_ref):
    o_ref[...] = (x_ref[...] * (1.0 + scale_ref[...]) + shift_ref[...]).astype(o_ref.dtype)


def _modulate(x3, shift, scale):
    b, s, d = x3.shape
    ts = _tile(s, 256)
    tok = pl.BlockSpec((None, ts, d), lambda bi, i: (bi, i, 0))
    vec = pl.BlockSpec((None, 1, d), lambda bi, i: (bi, 0, 0))
    return pl.pallas_call(
        _modulate_body,
        grid=(b, s // ts),
        in_specs=[tok, vec, vec],
        out_specs=tok,
        out_shape=jax.ShapeDtypeStruct((b, s, d), BF16),
        compiler_params=_cparams("parallel", "parallel"),
        name="modulate",
    )(x3, shift, scale)


def _ln_body(*refs, alpha, with_h):
    if with_h:
        x_ref, y_ref, gate_ref, g_ref, b_ref, shift_ref, scale_ref, xo_ref, ho_ref = refs
    else:
        x_ref, y_ref, gate_ref, g_ref, b_ref, xo_ref = refs
    z = alpha * x_ref[...] + gate_ref[...] * y_ref[...].astype(F32)
    mu = jnp.mean(z, axis=-1, keepdims=True)
    zc = z - mu
    var = jnp.mean(zc * zc, axis=-1, keepdims=True)
    xn = zc * lax.rsqrt(var + LN_EPS) * g_ref[...] + b_ref[...]
    xo_ref[...] = xn
    if with_h:
        ho_ref[...] = (xn * (1.0 + scale_ref[...]) + shift_ref[...]).astype(ho_ref.dtype)


def _ln_residual(x3, y3, gate, g, b, alpha, shift=None, scale=None):
    bsz, s, d = x3.shape
    ts = _tile(s, 256)
    with_h = shift is not None
    tok = pl.BlockSpec((None, ts, d), lambda bi, i: (bi, i, 0))
    vec = pl.BlockSpec((None, 1, d), lambda bi, i: (bi, 0, 0))
    par = pl.BlockSpec((1, d), lambda bi, i: (0, 0))
    in_specs = [tok, tok, vec, par, par]
    args = [x3, y3, gate, g.reshape(1, d), b.reshape(1, d)]
    out_specs = [tok]
    out_shape = [jax.ShapeDtypeStruct((bsz, s, d), F32)]
    if with_h:
        in_specs += [vec, vec]
        args += [shift, scale]
        out_specs.append(tok)
        out_shape.append(jax.ShapeDtypeStruct((bsz, s, d), BF16))
    res = pl.pallas_call(
        functools.partial(_ln_body, alpha=alpha, with_h=with_h),
        grid=(bsz, s // ts),
        in_specs=in_specs,
        out_specs=out_specs,
        out_shape=out_shape,
        compiler_params=_cparams("parallel", "parallel"),
        name="ln_residual",
    )(*args)
    return (res[0], res[1]) if with_h else (res[0], None)


def _row_spec(tm, k, single_buffer):
    mode = dict(pipeline_mode=pl.Buffered(1)) if single_buffer else {}
    return pl.BlockSpec((tm, k), lambda i, j: (i, 0), **mode)


def _mm_tiles(m, ncols):
    if m >= 2048:
        return _tile(m, 2048), _tile(ncols, 256)
    return m, _tile(ncols, 1024)


def _mm_body(a_ref, b_ref, o_ref):
    o_ref[...] = jnp.dot(a_ref[...], b_ref[...].astype(BF16), preferred_element_type=F32).astype(o_ref.dtype)


def _matmul(a, w, layer, *, col0=0, ncols=None, out_dtype=F32):
    m, k = a.shape
    ncols = w.shape[2] - col0 if ncols is None else ncols
    tm, tn = _mm_tiles(m, math.gcd(ncols, col0))
    jb = col0 // tn
    return pl.pallas_call(
        _mm_body,
        grid=(m // tm, ncols // tn),
        in_specs=[
            _row_spec(tm, k, m > tm),
            pl.BlockSpec((None, k, tn), lambda i, j: (layer, 0, j + jb)),
        ],
        out_specs=pl.BlockSpec((tm, tn), lambda i, j: (i, j)),
        out_shape=jax.ShapeDtypeStruct((m, ncols), out_dtype),
        compiler_params=_cparams("parallel", "arbitrary"),
        name="matmul",
    )(a, w)


def _mm3_body(a1_ref, a2_ref, a3_ref, w_ref, o_ref):
    c1 = a1_ref.shape[1]
    c2 = a2_ref.shape[1]
    acc = jnp.dot(a1_ref[...], w_ref[0:c1, :].astype(BF16), preferred_element_type=F32)
    acc += jnp.dot(a2_ref[...], w_ref[c1:c1 + c2, :].astype(BF16), preferred_element_type=F32)
    acc += jnp.dot(a3_ref[...], w_ref[c1 + c2:, :].astype(BF16), preferred_element_type=F32)
    o_ref[...] = acc.astype(o_ref.dtype)


def _matmul_concat3(a1, a2, a3, w, layer):
    m = a1.shape[0]
    k, n = w.shape[1], w.shape[2]
    assert a1.shape[1] + a2.shape[1] + a3.shape[1] == k
    tm, tn = _mm_tiles(m, n)
    return pl.pallas_call(
        _mm3_body,
        grid=(m // tm, n // tn),
        in_specs=[
            _row_spec(tm, a1.shape[1], m > tm),
            _row_spec(tm, a2.shape[1], m > tm),
            _row_spec(tm, a3.shape[1], m > tm),
            pl.BlockSpec((None, k, tn), lambda i, j: (layer, 0, j)),
        ],
        out_specs=pl.BlockSpec((tm, tn), lambda i, j: (i, j)),
        out_shape=jax.ShapeDtypeStruct((m, n), BF16),
        compiler_params=_cparams("parallel", "arbitrary"),
        name="matmul_out",
    )(a1, a2, a3, w)


def _gu_body(a_ref, wg_ref, wu_ref, o_ref):
    a = a_ref[...]
    g = jnp.dot(a, wg_ref[...].astype(BF16), preferred_element_type=F32)
    u = jnp.dot(a, wu_ref[...].astype(BF16), preferred_element_type=F32)
    o_ref[...] = (g * _sigmoid(g) * u).astype(o_ref.dtype)


def _matmul_swiglu_in(a, wg, wu, layer):
    m, k = a.shape
    n = wg.shape[2]
    tm, tn = _mm_tiles(m, n)
    tn = min(tn, _tile(n, 512))
    wspec = pl.BlockSpec((None, k, tn), lambda i, j: (layer, 0, j))
    return pl.pallas_call(
        _gu_body,
        grid=(m // tm, n // tn),
        in_specs=[_row_spec(tm, k, m > tm), wspec, wspec],
        out_specs=pl.BlockSpec((tm, tn), lambda i, j: (i, j)),
        out_shape=jax.ShapeDtypeStruct((m, n), BF16),
        compiler_params=_cparams("parallel", "arbitrary"),
        name="matmul_swiglu_in",
    )(a, wg, wu)


def _matmul_down(a, w, layer):
    m, k = a.shape
    n = w.shape[2]
    tm = _tile(m, 1024)
    tn = _tile(n, 256)
    return pl.pallas_call(
        _mm_body,
        grid=(m // tm, n // tn),
        in_specs=[_row_spec(tm, k, m > tm), pl.BlockSpec((None, k, tn), lambda i, j: (layer, 0, j))],
        out_specs=pl.BlockSpec((tm, tn), lambda i, j: (i, j)),
        out_shape=jax.ShapeDtypeStruct((m, n), BF16),
        compiler_params=_cparams("parallel", "arbitrary", vmem=V7X_VMEM_LIMIT_DOWN_BYTES),
        name="matmul_down",
    )(a, w)


def _rope_tables(s):
    t = jnp.arange(s)
    row = (t // GRID_W).astype(F32)[:, None]
    col = (t % GRID_W).astype(F32)[:, None]
    axis_dim = NA_HEAD_DIM // 2
    inv_freq = ROPE_THETA ** (-jnp.arange(0, axis_dim, 2, dtype=F32) / axis_dim)
    ang = jnp.concatenate([row * inv_freq, row * inv_freq, col * inv_freq, col * inv_freq], axis=-1)
    quarter = NA_HEAD_DIM // 4
    negate = (jnp.arange(NA_HEAD_DIM) % (2 * quarter)) < quarter
    sin = jnp.sin(ang)
    return jnp.cos(ang), jnp.where(negate[None, :], -sin, sin)


def _bias_table_body(rpb_ref, onehot_ref, o_ref):
    o_ref[...] = jnp.dot(rpb_ref[...], onehot_ref[...], preferred_element_type=F32,
                         precision=lax.Precision.HIGHEST)


def _na_bias_table(rpb):
    h, nr, nc = rpb.shape
    cq = np.arange(GRID_W)[:, None]
    ck = np.arange(GRID_W)[None, :]
    d_col = (np.clip(ck - cq, -(NA_KW - 1), NA_KW - 1) + (NA_KW - 1)).reshape(-1)
    ncp = -(-nc // 8) * 8
    rows_p = -(-(h * nr) // 8) * 8
    onehot = np.zeros((ncp, GRID_W * GRID_W), np.float32)
    onehot[d_col, np.arange(GRID_W * GRID_W)] = 1.0
    rpb2 = jnp.pad(rpb.reshape(h * nr, nc), ((0, rows_p - h * nr), (0, ncp - nc)))
    out = pl.pallas_call(
        _bias_table_body,
        out_shape=jax.ShapeDtypeStruct((rows_p, GRID_W * GRID_W), F32),
        compiler_params=_cparams(),
        name="na_bias_table",
    )(rpb2, jnp.asarray(onehot))
    return out[:h * nr].reshape(h, nr, GRID_W, GRID_W)


def _na_bias_classes(table):
    h = table.shape[0]
    span = NA_QROWS + NA_KH - 1
    tp = jnp.pad(table, ((0, 0), (span, span), (0, 0), (0, 0)))
    bases = (0, -(NA_KH // 2), -(NA_KH - 1))
    valid = np.zeros((3, NA_QROWS, span), bool)
    tiles = []
    for cls, base in enumerate(bases):
        for a in range(NA_QROWS):
            lo = base - a + (NA_KH - 1) + span
            tiles.append(tp[:, lo:lo + span])
            for j in range(span):
                rel = base + j - a
                if cls == 0:
                    valid[cls, a, j] = j < NA_KH
                elif cls == 1:
                    valid[cls, a, j] = -(NA_KH // 2) <= rel < NA_KH - NA_KH // 2
                else:
                    valid[cls, a, j] = j >= span - NA_KH
    cq = np.arange(GRID_W)[:, None]
    ck = np.arange(GRID_W)[None, :]
    col_start = np.clip(cq - NA_KW // 2, 0, GRID_W - NA_KW)
    col_ok = (ck >= col_start) & (ck < col_start + NA_KW)
    t = jnp.stack(tiles, axis=1).reshape(h, 3, NA_QROWS, span, GRID_W, GRID_W)
    mask = valid[None, :, :, :, None, None] & col_ok[None, None, None, None, :, :]
    t = jnp.where(mask, t, MASK_VALUE)
    return t.transpose(0, 1, 2, 4, 3, 5).reshape(h, 3, NA_QROWS * GRID_W, span * GRID_W)


def _na_body(q_ref, k_ref, v_ref, kc_ref, vc_ref, bias_ref, cos_ref, sin_ref, o_ref,
             qr_s, kr_s, qb_s, vb_s, *, rows, scale):
    cos = cos_ref[...]
    sin = sin_ref[...]
    quarter = NA_HEAD_DIM // 4
    lane = lax.broadcasted_iota(jnp.int32, cos.shape, 1)
    first = (lane % (2 * quarter)) < quarter

    def rope(z):
        rot = jnp.where(first, pltpu.roll(z, NA_HEAD_DIM - quarter, 1), pltpu.roll(z, quarter, 1))
        return z * cos + rot * sin

    q = q_ref[...]
    qr_s[...] = (rope(q) * scale).astype(BF16)
    qb_s[...] = (q * scale).astype(BF16)
    kr_s[...] = rope(k_ref[...]).astype(BF16)
    vb_s[...] = v_ref[...].astype(BF16)
    kc = kc_ref[...].astype(BF16)
    vc = vc_ref[...].astype(BF16)
    nt = (((1,), (1,)), ((), ()))
    groups = rows // NA_QROWS
    span = NA_QROWS + NA_KH - 1
    qn = NA_QROWS * GRID_W
    kn = span * GRID_W

    def body(g, carry):
        ks = jnp.clip(g * NA_QROWS - NA_KH // 2, 0, rows - span)
        cls = jnp.where(g == 0, 0, jnp.where(g == groups - 1, 2, 1))
        q0 = pl.multiple_of(g * qn, qn)
        k0 = pl.multiple_of(ks * GRID_W, GRID_W)
        s_win = lax.dot_general(qr_s[pl.ds(q0, qn), :], kr_s[pl.ds(k0, kn), :], nt,
                                preferred_element_type=F32) + bias_ref[cls]
        s_ctx = lax.dot_general(qb_s[pl.ds(q0, qn), :], kc, nt, preferred_element_type=F32)
        m = jnp.maximum(jnp.max(s_win, axis=-1, keepdims=True), jnp.max(s_ctx, axis=-1, keepdims=True))
        e_win = jnp.exp(s_win - m)
        e_ctx = jnp.exp(s_ctx - m)
        den = jnp.sum(e_win, axis=-1, keepdims=True) + jnp.sum(e_ctx, axis=-1, keepdims=True)
        o = jnp.dot(e_win.astype(BF16), vb_s[pl.ds(k0, kn), :], preferred_element_type=F32)
        o += jnp.dot(e_ctx.astype(BF16), vc, preferred_element_type=F32)
        o_ref[pl.ds(q0, qn), :] = (o / den).astype(o_ref.dtype)
        return carry

    lax.fori_loop(0, groups, body, 0, unroll=4)


def _neighbourhood_attention(p3, kvc3, bias_cls, cos, sin, *, off_q, off_k, off_v, off_kc, off_vc, heads):
    bsz, s, _ = p3.shape
    lc = kvc3.shape[1]
    rows = s // GRID_W
    span = NA_QROWS + NA_KH - 1
    assert s % GRID_W == 0 and rows % NA_QROWS == 0 and rows // NA_QROWS >= 3
    dh = NA_HEAD_DIM

    def col(off):
        return pl.BlockSpec((None, s, dh), lambda h, b: (b, 0, off + h))

    def colc(off):
        return pl.BlockSpec((None, lc, dh), lambda h, b: (b, 0, off + h))

    tab = pl.BlockSpec((s, dh), lambda h, b: (0, 0))
    return pl.pallas_call(
        functools.partial(_na_body, rows=rows, scale=dh ** -0.5),
        grid=(heads, bsz),
        in_specs=[col(off_q), col(off_k), col(off_v), colc(off_kc), colc(off_vc),
                  pl.BlockSpec((None, 3, NA_QROWS * GRID_W, span * GRID_W), lambda h, b: (h, 0, 0, 0)), tab, tab],
        out_specs=pl.BlockSpec((None, s, dh), lambda h, b: (b, 0, h)),
        out_shape=jax.ShapeDtypeStruct((bsz, s, heads * dh), BF16),
        scratch_shapes=[pltpu.VMEM((s, dh), BF16)] * 4,
        compiler_params=_cparams("parallel", "arbitrary"),
        name="neighbourhood_attention",
    )(p3, p3, p3, kvc3, kvc3, bias_cls, cos, sin)


def _ctx_attn_body(q_ref, k_ref, v_ref, o_ref, *, scale):
    nt = (((1,), (1,)), ((), ()))
    s = lax.dot_general(q_ref[...].astype(BF16), k_ref[...].astype(BF16), nt, preferred_element_type=F32) * scale
    m = jnp.max(s, axis=-1, keepdims=True)
    e = jnp.exp(s - m)
    den = jnp.sum(e, axis=-1, keepdims=True)
    o = jnp.dot(e.astype(BF16), v_ref[...].astype(BF16), preferred_element_type=F32)
    o_ref[...] = (o / den).astype(o_ref.dtype)


def _context_attention(pc3, *, off_q, off_k, off_v, heads):
    bsz, lc, _ = pc3.shape
    dh = NA_HEAD_DIM

    def col(off):
        return pl.BlockSpec((None, lc, dh), lambda b, h: (b, 0, off + h))

    return pl.pallas_call(
        functools.partial(_ctx_attn_body, scale=dh ** -0.5),
        grid=(bsz, heads),
        in_specs=[col(off_q), col(off_k), col(off_v)],
        out_specs=pl.BlockSpec((None, lc, dh), lambda b, h: (b, 0, h)),
        out_shape=jax.ShapeDtypeStruct((bsz, lc, heads * dh), BF16),
        compiler_params=_cparams("parallel", "parallel"),
        name="context_attention",
    )(pc3, pc3, pc3)


def _conformer_body(a_ref, g_ref, pa_ref, pg_ref, na_ref, ng_ref, w_ref, wb_ref, lg_ref, lb_ref, o_ref, u_s,
                    *, ts, taps):
    i = pl.program_id(1)
    halo = CONV_HALO_ROWS
    a = a_ref[...]
    g = g_ref[...]
    u_s[halo:halo + ts, :] = a * _sigmoid(g)
    prev = pa_ref[...] * _sigmoid(pg_ref[...])
    u_s[0:halo, :] = jnp.where(i > 0, prev, 0.0)
    nxt = na_ref[...] * _sigmoid(ng_ref[...])
    u_s[halo + ts:, :] = jnp.where(i < pl.num_programs(1) - 1, nxt, 0.0)
    base = halo - taps // 2
    z = wb_ref[...] + w_ref[0:1, :] * u_s[base:base + ts, :]
    for t in range(1, taps):
        z += w_ref[t:t + 1, :] * u_s[base + t:base + t + ts, :]
    mu = jnp.mean(z, axis=-1, keepdims=True)
    zc = z - mu
    var = jnp.mean(zc * zc, axis=-1, keepdims=True)
    zn = zc * lax.rsqrt(var + LN_EPS) * lg_ref[...] + lb_ref[...]
    o_ref[...] = (zn * _sigmoid(zn)).astype(o_ref.dtype)


def _conformer(p3, dw_w, dw_b, ln_g, ln_b, *, off):
    bsz, l, _ = p3.shape
    taps, c = dw_w.shape
    halo = CONV_HALO_ROWS
    assert taps // 2 < halo and l % halo == 0
    ts = _tile(l, 256)
    hb = ts // halo
    nh = l // halo

    def cur(o):
        return pl.BlockSpec((None, ts, c), lambda b, i: (b, i, o))

    def prev(o):
        return pl.BlockSpec((None, halo, c), lambda b, i: (b, jnp.maximum(i * hb - 1, 0), o))

    def nxt(o):
        return pl.BlockSpec((None, halo, c), lambda b, i: (b, jnp.minimum((i + 1) * hb, nh - 1), o))

    par = pl.BlockSpec((1, c), lambda b, i: (0, 0))
    return pl.pallas_call(
        functools.partial(_conformer_body, ts=ts, taps=taps),
        grid=(bsz, l // ts),
        in_specs=[cur(off), cur(off + 1), prev(off), prev(off + 1), nxt(off), nxt(off + 1),
                  pl.BlockSpec((taps, c), lambda b, i: (0, 0)), par, par, par],
        out_specs=pl.BlockSpec((None, ts, c), lambda b, i: (b, i, 0)),
        out_shape=jax.ShapeDtypeStruct((bsz, l, c), BF16),
        scratch_shapes=[pltpu.VMEM((ts + 2 * halo, c), F32)],
        compiler_params=_cparams("parallel", "arbitrary"),
        name="conformer",
    )(p3, p3, p3, p3, p3, p3, dw_w, dw_b.reshape(1, c), ln_g.reshape(1, c), ln_b.reshape(1, c))


def _hy_short_body(cur_ref, prev_ref, next_ref, w_ref, b_ref, v_ref, x1_ref, x2_ref, u_s, *, ts, c):
    i = pl.program_id(1)
    halo = CONV_HALO_ROWS
    u_s[halo:halo + ts, :] = cur_ref[...]
    u_s[0:halo, :] = jnp.where(i > 0, prev_ref[...], 0.0)
    u_s[halo + ts:, :] = jnp.where(i < pl.num_programs(1) - 1, next_ref[...], 0.0)
    z = (b_ref[...] + w_ref[0:1, :] * u_s[halo - 1:halo - 1 + ts, :] + w_ref[1:2, :] * u_s[halo:halo + ts, :]
         + w_ref[2:3, :] * u_s[halo + 1:halo + 1 + ts, :])
    v_ref[...] = z[:, 0:c].astype(v_ref.dtype)
    x1_ref[...] = z[:, c:2 * c]
    x2_ref[...] = z[:, 2 * c:3 * c]


def _hyena_short_conv(p3, short_w, short_b, c):
    bsz, l, _ = p3.shape
    halo = CONV_HALO_ROWS
    ts = _tile(l, 256)
    hb = ts // halo
    nh = l // halo
    w3 = 3 * c
    out = pl.BlockSpec((ts, c), lambda b, i: (i, b))
    return pl.pallas_call(
        functools.partial(_hy_short_body, ts=ts, c=c),
        grid=(bsz, l // ts),
        in_specs=[
            pl.BlockSpec((None, ts, w3), lambda b, i: (b, i, 0)),
            pl.BlockSpec((None, halo, w3), lambda b, i: (b, jnp.maximum(i * hb - 1, 0), 0)),
            pl.BlockSpec((None, halo, w3), lambda b, i: (b, jnp.minimum((i + 1) * hb, nh - 1), 0)),
            pl.BlockSpec((3, w3), lambda b, i: (0, 0)),
            pl.BlockSpec((1, w3), lambda b, i: (0, 0)),
        ],
        out_specs=[out, out, out],
        out_shape=[jax.ShapeDtypeStruct((l, bsz * c), BF16), jax.ShapeDtypeStruct((l, bsz * c), F32),
                   jax.ShapeDtypeStruct((l, bsz * c), F32)],
        scratch_shapes=[pltpu.VMEM((ts + 2 * halo, w3), F32)],
        compiler_params=_cparams("parallel", "arbitrary"),
        name="hyena_short_conv",
    )(p3, p3, p3, short_w, short_b.reshape(1, w3))


def _hyena_embedding(l):
    t = jnp.linspace(0.0, 1.0, l, dtype=F32)[:, None]
    bands = (HYENA_EMB_DIM - 1) // 2
    omega = (2.0 * math.pi / l) * jnp.arange(l, dtype=F32)[:, None]
    f = jnp.linspace(1e-4, bands - 1, bands, dtype=F32)[None, :]
    return jnp.concatenate([t, jnp.cos(f * omega), -jnp.sin(f * omega)], axis=-1)


def _hy_filter_body(emb_ref, w1_ref, b1_ref, w2_ref, b2_ref, w3_ref, b3_ref, freq_ref, w4_ref, delta_ref, o_ref,
                    hid_s):
    @pl.when(pl.program_id(0) == 0)
    def _():
        freq = freq_ref[...]
        hid = jnp.sin(freq * (jnp.dot(emb_ref[...], w1_ref[...], preferred_element_type=F32) + b1_ref[...]))
        hid = jnp.sin(freq * (jnp.dot(hid, w2_ref[...], preferred_element_type=F32) + b2_ref[...]))
        hid_s[...] = jnp.sin(freq * (jnp.dot(hid, w3_ref[...], preferred_element_type=F32) + b3_ref[...]))

    h = jnp.dot(hid_s[...], w4_ref[...], preferred_element_type=F32)
    h = h * jnp.exp(-emb_ref[:, 0:1] * delta_ref[...])
    h = h / (jnp.sum(jnp.abs(h), axis=0, keepdims=True) + 1e-6)
    o_ref[...] = h.astype(o_ref.dtype)


def _hyena_filters(emb, w1, b1, w2, b2, w3, b3, w4, freq, deltas4):
    l, e = emb.shape
    hf = w1.shape[1]
    n4 = w4.shape[1]
    tn = _tile(n4, 512)

    def full(shape):
        return pl.BlockSpec(shape, lambda j: (0, 0))

    return pl.pallas_call(
        _hy_filter_body,
        grid=(n4 // tn,),
        in_specs=[full((l, e)), full((e, hf)), full((1, hf)), full((hf, hf)), full((1, hf)), full((hf, hf)),
                  full((1, hf)), full((1, hf)),
                  pl.BlockSpec((hf, tn), lambda j: (0, j)), pl.BlockSpec((1, tn), lambda j: (0, j))],
        out_specs=pl.BlockSpec((l, tn), lambda j: (0, j)),
        out_shape=jax.ShapeDtypeStruct((l, n4), BF16),
        scratch_shapes=[pltpu.VMEM((l, hf), F32)],
        compiler_params=_cparams("arbitrary"),
        name="hyena_filters",
    )(emb, w1, b1.reshape(1, hf), w2, b2.reshape(1, hf), w3, b3.reshape(1, hf), freq.reshape(1, hf), w4, deltas4)


def _dft_table(l, tm):
    n = 2 * l
    half = tm // 2
    r = np.arange(n)
    f_np = (r // tm) * half + (r % half)
    is_im_np = (r % tm) >= half
    sq = 1 << (int(math.log2(l)) // 2)
    hi = l // sq
    f = jnp.asarray(f_np, jnp.int32)[:, None]
    w0 = 2.0 * math.pi / n
    a_hi = ((f * (jnp.arange(hi, dtype=jnp.int32)[None, :] * sq)) % n).astype(F32) * w0
    a_lo = ((f * jnp.arange(sq, dtype=jnp.int32)[None, :]) % n).astype(F32) * w0
    ch, sh = jnp.cos(a_hi)[:, :, None], jnp.sin(a_hi)[:, :, None]
    cl, sl = jnp.cos(a_lo)[:, None, :], jnp.sin(a_lo)[:, None, :]
    cosv = (ch * cl - sh * sl).reshape(n, l)
    sinv = (sh * cl + ch * sl).reshape(n, l)
    nyq = jnp.asarray(np.where(np.arange(l) % 2 == 0, 1.0, -1.0), F32)[None, :]
    is_im = jnp.asarray(is_im_np)[:, None]
    table = jnp.where(is_im, jnp.where(f == 0, nyq, -sinv), cosv)
    return table.astype(BF16)


def _spec_body(a_ref, hf_ref, hb_ref, skip_ref, gr_ref, gi_ref, *, n):
    i = pl.program_id(0)
    a = a_ref[...]
    half = a.shape[0] // 2
    ff = jnp.dot(a, hf_ref[...], preferred_element_type=F32)
    fb = jnp.dot(a, hb_ref[...], preferred_element_type=F32)
    skip = skip_ref[...]
    row = lax.broadcasted_iota(jnp.int32, (half, 1), 0)
    is_dc = jnp.logical_and(row == 0, i == 0)
    sc = jnp.where(is_dc, 1.0 / n, 2.0 / n)
    gr = ff[:half] + fb[:half] + skip
    gi = jnp.where(is_dc, ff[half:] + fb[half:] + skip, ff[half:] - fb[half:])
    gr_ref[...] = gr * sc
    gi_ref[...] = gi * sc


def _hyena_filter_spectrum(table, filt, skip, c, tm):
    n, l = table.shape
    order = skip.shape[0]
    tn = _tile(c, 512)
    cb = c // tn
    half = tm // 2
    out = pl.BlockSpec((None, half, tn), lambda i, o, j: (o, i, j))
    return pl.pallas_call(
        functools.partial(_spec_body, n=n),
        grid=(n // tm, order, cb),
        in_specs=[
            pl.BlockSpec((tm, l), lambda i, o, j: (i, 0)),
            pl.BlockSpec((l, tn), lambda i, o, j: (0, (2 * o) * cb + j)),
            pl.BlockSpec((l, tn), lambda i, o, j: (0, (2 * o + 1) * cb + j)),
            pl.BlockSpec((None, 1, tn), lambda i, o, j: (o, 0, j)),
        ],
        out_specs=[out, out],
        out_shape=[jax.ShapeDtypeStruct((order, l, c), F32)] * 2,
        compiler_params=_cparams("parallel", "arbitrary", "arbitrary"),
        name="hyena_filter_spectrum",
    )(table, filt, filt, skip.reshape(order, 1, c))


def _fwd_body(a_ref, u_ref, gr_ref, gi_ref, o_ref):
    i = pl.program_id(0)
    half = a_ref.shape[0] // 2
    z = jnp.dot(a_ref[...], u_ref[...], preferred_element_type=F32)
    zr, zi = z[:half], z[half:]
    gr, gi = gr_ref[...], gi_ref[...]
    row = lax.broadcasted_iota(jnp.int32, (half, 1), 0)
    is_dc = jnp.logical_and(row == 0, i == 0)
    yr = jnp.where(is_dc, zr * gr, zr * gr - zi * gi)
    yi = jnp.where(is_dc, zi * gi, zr * gi + zi * gr)
    o_ref[:half, :] = yr.astype(o_ref.dtype)
    o_ref[half:, :] = yi.astype(o_ref.dtype)


def _hyena_fwd_mul(table, u, gr, gi, order, c, tm):
    n, l = table.shape
    ncol = u.shape[1]
    tn = _tile(c, 1024)
    cb = c // tn
    half = tm // 2
    gspec = pl.BlockSpec((None, half, tn), lambda i, j: (order, i, j % cb))
    return pl.pallas_call(
        _fwd_body,
        grid=(n // tm, ncol // tn),
        in_specs=[pl.BlockSpec((tm, l), lambda i, j: (i, 0)), pl.BlockSpec((l, tn), lambda i, j: (0, j)),
                  gspec, gspec],
        out_specs=pl.BlockSpec((tm, tn), lambda i, j: (i, j)),
        out_shape=jax.ShapeDtypeStruct((n, ncol), BF16),
        compiler_params=_cparams("parallel", "arbitrary"),
        name="hyena_fwd_dft",
    )(table, u, gr, gi)


def _inv_body(a_ref, y_ref, x_ref, o_ref, acc_ref):
    kk = pl.program_id(2)
    tn_dims = (((0,), (0,)), ((), ()))
    prod = lax.dot_general(a_ref[...], y_ref[...], tn_dims, preferred_element_type=F32)

    @pl.when(kk == 0)
    def _():
        acc_ref[...] = prod

    @pl.when(kk > 0)
    def _():
        acc_ref[...] += prod

    @pl.when(kk == pl.num_programs(2) - 1)
    def _():
        o_ref[...] = (x_ref[...] * acc_ref[...]).astype(o_ref.dtype)


def _hyena_inv_gate(table, y, xg, c, *, token_major):
    n, l = table.shape
    ncol = y.shape[1]
    tmt = _tile(l, 1024)
    tk = _tile(n, 2048)
    tn = c if token_major else _tile(c, 1024)
    if token_major:
        out_spec = pl.BlockSpec((None, tmt, c), lambda i, j, kk: (j, i, 0))
        out_shape = jax.ShapeDtypeStruct((ncol // c, l, c), BF16)
    else:
        out_spec = pl.BlockSpec((tmt, tn), lambda i, j, kk: (i, j))
        out_shape = jax.ShapeDtypeStruct((l, ncol), BF16)
    return pl.pallas_call(
        _inv_body,
        grid=(l // tmt, ncol // tn, n // tk),
        in_specs=[pl.BlockSpec((tk, tmt), lambda i, j, kk: (kk, i)),
                  pl.BlockSpec((tk, tn), lambda i, j, kk: (kk, j)),
                  pl.BlockSpec((tmt, tn), lambda i, j, kk: (i, j))],
        out_specs=out_spec,
        out_shape=out_shape,
        scratch_shapes=[pltpu.VMEM((tmt, tn), F32)],
        compiler_params=_cparams("parallel", "parallel", "arbitrary"),
        name="hyena_inv_dft",
    )(table, y, xg)


def _hyena_group(p3, table, tm, emb, filt_w, short_w, short_b, skip, deltas4):
    c = skip.shape[1]
    filt = _hyena_filters(emb, *filt_w, deltas4)
    gr, gi = _hyena_filter_spectrum(table, filt, skip, c, tm)
    v, x1, x2 = _hyena_short_conv(p3, short_w, short_b, c)
    y1 = _hyena_fwd_mul(table, v, gr, gi, 0, c, tm)
    z1 = _hyena_inv_gate(table, y1, x1, c, token_major=False)
    y2 = _hyena_fwd_mul(table, z1, gr, gi, 1, c, tm)
    return _hyena_inv_gate(table, y2, x2, c, token_major=True)


def kernel(x, c, ctx, c_ctx, w_mod_down, w_mod_up, b_mod, w_in, w_out, hy_short_w, hy_short_b, hy_filt_w1, hy_filt_b1, hy_filt_w2, hy_filt_b2, hy_filt_w3, hy_filt_b3, hy_filt_w4, hy_filt_freq, hy_skip, na_rpb, cf_dw_w, cf_dw_b, cf_norm_g, cf_norm_b, w_ffn_gate, w_ffn_up, w_ffn_down, ln_mix_g, ln_mix_b, ln_ffn_g, ln_ffn_b):
    bsz, s, d = x.shape
    lc = ctx.shape[1]
    depth = w_in.shape[0]
    in_cols = w_in.shape[2]
    w_hy = hy_skip.shape[2]
    w_cf = cf_dw_w.shape[2]
    w_na = (in_cols - 3 * w_hy - 2 * w_cf) // 3
    heads = w_na // NA_HEAD_DIM
    off_na = 3 * w_hy
    off_kv = off_na + w_na
    off_conf = off_na + 3 * w_na
    alpha = (2 * depth) ** 0.25
    dh = NA_HEAD_DIM
    assert bsz + 1 <= 8 and off_conf % w_cf == 0 and off_na % dh == 0

    cos, sin = _rope_tables(s)
    tm_lat = min(1024, 2 * s)
    tm_ctx = min(1024, 2 * lc)
    table_lat = _dft_table(s, tm_lat)
    table_ctx = _dft_table(lc, tm_ctx)
    emb_lat = _hyena_embedding(s)
    emb_ctx = _hyena_embedding(lc)
    deltas = jnp.abs(jnp.linspace(HYENA_MIN_DECAY, HYENA_MAX_DECAY, w_hy, dtype=F32))
    deltas4 = jnp.tile(deltas, 4)[None, :]

    cond8 = jnp.zeros((8, d), F32).at[:bsz].set(c).at[bsz].set(c_ctx)
    mods = _modulation_all(cond8, w_mod_down, w_mod_up, b_mod).reshape(depth, 8, N_MOD, d)

    def mod_lat(l, i):
        return mods[l, :bsz, i][:, None, :]

    def mod_ctx(l, i):
        return jnp.broadcast_to(mods[l, bsz, i][None, None, :], (bsz, 1, d))

    h = _modulate(x, mod_lat(0, 0), mod_lat(0, 1))
    hc = _modulate(ctx, mod_ctx(0, 0), mod_ctx(0, 1))

    for l in range(depth):
        last = l == depth - 1
        filt_w = (hy_filt_w1[l], hy_filt_b1[l], hy_filt_w2[l], hy_filt_b2[l], hy_filt_w3[l], hy_filt_b3[l],
                  hy_filt_w4[l], hy_filt_freq[l])
        bias_cls = _na_bias_classes(_na_bias_table(na_rpb[l]))

        hc2 = hc.reshape(bsz * lc, d)
        if last:
            kvc = _matmul(hc2, w_in, l, col0=off_kv, ncols=2 * w_na).reshape(bsz, lc, 2 * w_na)
            off_kc, off_vc = 0, heads
        else:
            pc = _matmul(hc2, w_in, l).reshape(bsz, lc, in_cols)
            kvc = pc
            off_kc, off_vc = off_kv // dh, (off_kv + w_na) // dh

        p = _matmul(h.reshape(bsz * s, d), w_in, l).reshape(bsz, s, in_cols)
        y_hy = _hyena_group(p, table_lat, tm_lat, emb_lat, filt_w, hy_short_w[l], hy_short_b[l], hy_skip[l], deltas4)
        y_na = _neighbourhood_attention(p, kvc, bias_cls, cos, sin, off_q=off_na // dh, off_k=off_kv // dh,
                                        off_v=(off_kv + w_na) // dh, off_kc=off_kc, off_vc=off_vc, heads=heads)
        y_cf = _conformer(p, cf_dw_w[l], cf_dw_b[l], cf_norm_g[l], cf_norm_b[l], off=off_conf // w_cf)
        y = _matmul_concat3(y_hy.reshape(bsz * s, w_hy), y_na.reshape(bsz * s, w_na), y_cf.reshape(bsz * s, w_cf),
                            w_out, l).reshape(bsz, s, d)
        x, h = _ln_residual(x, y, mod_lat(l, 2), ln_mix_g[l], ln_mix_b[l], alpha, mod_lat(l, 3), mod_lat(l, 4))

        if not last:
            yc_hy = _hyena_group(pc, table_ctx, tm_ctx, emb_ctx, filt_w, hy_short_w[l], hy_short_b[l], hy_skip[l],
                                 deltas4)
            yc_na = _context_attention(pc, off_q=off_na // dh, off_k=off_kv // dh, off_v=(off_kv + w_na) // dh,
                                       heads=heads)
            yc_cf = _conformer(pc, cf_dw_w[l], cf_dw_b[l], cf_norm_g[l], cf_norm_b[l], off=off_conf // w_cf)
            yc = _matmul_concat3(yc_hy.reshape(bsz * lc, w_hy), yc_na.reshape(bsz * lc, w_na),
                                 yc_cf.reshape(bsz * lc, w_cf), w_out, l).reshape(bsz, lc, d)
            ctx, hc = _ln_residual(ctx, yc, mod_ctx(l, 2), ln_mix_g[l], ln_mix_b[l], alpha, mod_ctx(l, 3),
                                   mod_ctx(l, 4))
            ac = _matmul_swiglu_in(hc.reshape(bsz * lc, d), w_ffn_gate, w_ffn_up, l)
            fc = _matmul_down(ac, w_ffn_down, l).reshape(bsz, lc, d)
            ctx, hc = _ln_residual(ctx, fc, mod_ctx(l, 5), ln_ffn_g[l], ln_ffn_b[l], alpha, mod_ctx(l + 1, 0),
                                   mod_ctx(l + 1, 1))

        a = _matmul_swiglu_in(h.reshape(bsz * s, d), w_ffn_gate, w_ffn_up, l)
        f = _matmul_down(a, w_ffn_down, l).reshape(bsz, s, d)
        if last:
            x, _ = _ln_residual(x, f, mod_lat(l, 5), ln_ffn_g[l], ln_ffn_b[l], alpha)
        else:
            x, h = _ln_residual(x, f, mod_lat(l, 5), ln_ffn_g[l], ln_ffn_b[l], alpha, mod_lat(l + 1, 0),
                                mod_lat(l + 1, 1))
    return x
```

```python
import functools
import math

import numpy as np
import jax
import jax.numpy as jnp
from jax import lax
from jax.experimental import pallas as pl
from jax.experimental.pallas import tpu as pltpu

F32 = jnp.float32
BF16 = jnp.bfloat16

GRID_W = 64
NA_HEAD_DIM = 128
NA_KH = 8
NA_KW = 16
NA_QROWS = 4
ROPE_THETA = 10000.0
HYENA_EMB_DIM = 33
HYENA_BLOCK = 1024
HYENA_TARGET = 1e-2
HYENA_MIN_DECAY = math.log(HYENA_TARGET) / 1.5
HYENA_MAX_DECAY = math.log(HYENA_TARGET) / 0.3
N_MOD = 6
LN_EPS = 1e-5
MASK_VALUE = -1e30

V7X_VMEM_LIMIT_BYTES = 56 * 1024 * 1024
V7X_VMEM_LIMIT_DOWN_BYTES = 60 * 1024 * 1024
SUBLANES = 8
CONV_HALO_ROWS = 16


def _cparams(*sem, vmem=V7X_VMEM_LIMIT_BYTES):
    return pltpu.CompilerParams(dimension_semantics=sem, vmem_limit_bytes=vmem)


def _tile(dim, pref, align=128):
    t = (min(dim, pref) // align) * align
    while t >= align:
        if dim % t == 0:
            return t
        t -= align
    return dim


def _sigmoid(z):
    return jax.nn.sigmoid(z)


def _mod_body(cond_ref, wd_ref, wu_ref, b_ref, o_ref, t_ref):
    @pl.when(pl.program_id(1) == 0)
    def _():
        cnd = cond_ref[...]
        t_ref[...] = jnp.dot(cnd * _sigmoid(cnd), wd_ref[...], preferred_element_type=F32)

    o_ref[...] = jnp.dot(t_ref[...], wu_ref[...], preferred_element_type=F32) + b_ref[...]


def _modulation_all(cond8, w_down, w_up, b_mod):
    depth, d, rank = w_down.shape
    n_out = w_up.shape[2]
    tn = _tile(n_out, 2048)
    return pl.pallas_call(
        _mod_body,
        grid=(depth, n_out // tn),
        in_specs=[
            pl.BlockSpec((8, d), lambda l, j: (0, 0)),
            pl.BlockSpec((None, d, rank), lambda l, j: (l, 0, 0)),
            pl.BlockSpec((None, rank, tn), lambda l, j: (l, 0, j)),
            pl.BlockSpec((None, 1, tn), lambda l, j: (l, 0, j)),
        ],
        out_specs=pl.BlockSpec((None, 8, tn), lambda l, j: (l, 0, j)),
        out_shape=jax.ShapeDtypeStruct((depth, 8, n_out), F32),
        scratch_shapes=[pltpu.VMEM((8, rank), F32)],
        compiler_params=_cparams("arbitrary", "arbitrary"),
        name="modulation",
    )(cond8, w_down, w_up, b_mod.reshape(depth, 1, n_out))


def _modulate_body(x_ref, shift_ref, scale_ref, o_ref):
    o_ref[...] = (x_ref[...] * (1.0 + scale_ref[...]) + shift_ref[...]).astype(o_ref.dtype)


def _modulate(x3, shift, scale):
    b, s, d = x3.shape
    ts = _tile(s, 256)
    tok = pl.BlockSpec((None, ts, d), lambda bi, i: (bi, i, 0))
    vec = pl.BlockSpec((None, 1, d), lambda bi, i: (bi, 0, 0))
    return pl.pallas_call(
        _modulate_body,
        grid=(b, s // ts),
        in_specs=[tok, vec, vec],
        out_specs=tok,
        out_shape=jax.ShapeDtypeStruct((b, s, d), BF16),
        compiler_params=_cparams("parallel", "parallel"),
        name="modulate",
    )(x3, shift, scale)


def _ln_body(*refs, alpha, with_h):
    if with_h:
        x_ref, y_ref, gate_ref, g_ref, b_ref, shift_ref, scale_ref, xo_ref, ho_ref = refs
    else:
        x_ref, y_ref, gate_ref, g_ref, b_ref, xo_ref = refs
    z = alpha * x_ref[...] + gate_ref[...] * y_ref[...].astype(F32)
    mu = jnp.mean(z, axis=-1, keepdims=True)
    zc = z - mu
    var = jnp.mean(zc * zc, axis=-1, keepdims=True)
    xn = zc * lax.rsqrt(var + LN_EPS) * g_ref[...] + b_ref[...]
    xo_ref[...] = xn
    if with_h:
        ho_ref[...] = (xn * (1.0 + scale_ref[...]) + shift_ref[...]).astype(ho_ref.dtype)


def _ln_residual(x3, y3, gate, g, b, alpha, shift=None, scale=None):
    bsz, s, d = x3.shape
    ts = _tile(s, 256)
    with_h = shift is not None
    tok = pl.BlockSpec((None, ts, d), lambda bi, i: (bi, i, 0))
    vec = pl.BlockSpec((None, 1, d), lambda bi, i: (bi, 0, 0))
    par = pl.BlockSpec((1, d), lambda bi, i: (0, 0))
    in_specs = [tok, tok, vec, par, par]
    args = [x3, y3, gate, g.reshape(1, d), b.reshape(1, d)]
    out_specs = [tok]
    out_shape = [jax.ShapeDtypeStruct((bsz, s, d), F32)]
    if with_h:
        in_specs += [vec, vec]
        args += [shift, scale]
        out_specs.append(tok)
        out_shape.append(jax.ShapeDtypeStruct((bsz, s, d), BF16))
    res = pl.pallas_call(
        functools.partial(_ln_body, alpha=alpha, with_h=with_h),
        grid=(bsz, s // ts),
        in_specs=in_specs,
        out_specs=out_specs,
        out_shape=out_shape,
        compiler_params=_cparams("parallel", "parallel"),
        name="ln_residual",
    )(*args)
    return (res[0], res[1]) if with_h else (res[0], None)


def _row_spec(tm, k, single_buffer):
    mode = dict(pipeline_mode=pl.Buffered(1)) if single_buffer else {}
    return pl.BlockSpec((tm, k), lambda i, j: (i, 0), **mode)


def _mm_tiles(m, ncols, tn_big=256):
    if m >= 2048:
        return _tile(m, 2048), _tile(ncols, tn_big)
    return m, _tile(ncols, 1024)


def _mm_body(a_ref, b_ref, o_ref):
    o_ref[...] = jnp.dot(a_ref[...], b_ref[...].astype(BF16), preferred_element_type=F32).astype(o_ref.dtype)


def _matmul(a, w, layer, *, col0=0, ncols=None, out_dtype=F32):
    m, k = a.shape
    ncols = w.shape[2] - col0 if ncols is None else ncols
    tm, tn = _mm_tiles(m, math.gcd(ncols, col0), tn_big=512)
    jb = col0 // tn
    return pl.pallas_call(
        _mm_body,
        grid=(m // tm, ncols // tn),
        in_specs=[
            _row_spec(tm, k, m > tm),
            pl.BlockSpec((None, k, tn), lambda i, j: (layer, 0, j + jb)),
        ],
        out_specs=pl.BlockSpec((tm, tn), lambda i, j: (i, j)),
        out_shape=jax.ShapeDtypeStruct((m, ncols), out_dtype),
        compiler_params=_cparams("parallel", "arbitrary"),
        name="matmul",
    )(a, w)


def _mm3_body(a1_ref, a2_ref, a3_ref, w_ref, o_ref):
    c1 = a1_ref.shape[1]
    c2 = a2_ref.shape[1]
    acc = jnp.dot(a1_ref[...], w_ref[0:c1, :].astype(BF16), preferred_element_type=F32)
    acc += jnp.dot(a2_ref[...], w_ref[c1:c1 + c2, :].astype(BF16), preferred_element_type=F32)
    acc += jnp.dot(a3_ref[...], w_ref[c1 + c2:, :].astype(BF16), preferred_element_type=F32)
    o_ref[...] = acc.astype(o_ref.dtype)


def _matmul_concat3(a1, a2, a3, w, layer):
    m = a1.shape[0]
    k, n = w.shape[1], w.shape[2]
    assert a1.shape[1] + a2.shape[1] + a3.shape[1] == k
    tm, tn = _mm_tiles(m, n)
    return pl.pallas_call(
        _mm3_body,
        grid=(m // tm, n // tn),
        in_specs=[
            _row_spec(tm, a1.shape[1], m > tm),
            _row_spec(tm, a2.shape[1], m > tm),
            _row_spec(tm, a3.shape[1], m > tm),
            pl.BlockSpec((None, k, tn), lambda i, j: (layer, 0, j)),
        ],
        out_specs=pl.BlockSpec((tm, tn), lambda i, j: (i, j)),
        out_shape=jax.ShapeDtypeStruct((m, n), BF16),
        compiler_params=_cparams("parallel", "arbitrary"),
        name="matmul_out",
    )(a1, a2, a3, w)


def _gu_body(a_ref, wg_ref, wu_ref, o_ref):
    a = a_ref[...]
    g = jnp.dot(a, wg_ref[...].astype(BF16), preferred_element_type=F32)
    u = jnp.dot(a, wu_ref[...].astype(BF16), preferred_element_type=F32)
    o_ref[...] = (g * _sigmoid(g) * u).astype(o_ref.dtype)


def _matmul_swiglu_in(a, wg, wu, layer):
    m, k = a.shape
    n = wg.shape[2]
    tm, tn = _mm_tiles(m, n)
    tn = min(tn, _tile(n, 512))
    wspec = pl.BlockSpec((None, k, tn), lambda i, j: (layer, 0, j))
    return pl.pallas_call(
        _gu_body,
        grid=(m // tm, n // tn),
        in_specs=[_row_spec(tm, k, m > tm), wspec, wspec],
        out_specs=pl.BlockSpec((tm, tn), lambda i, j: (i, j)),
        out_shape=jax.ShapeDtypeStruct((m, n), BF16),
        compiler_params=_cparams("parallel", "arbitrary"),
        name="matmul_swiglu_in",
    )(a, wg, wu)


def _matmul_down(a, w, layer):
    m, k = a.shape
    n = w.shape[2]
    tm = _tile(m, 1024)
    tn = _tile(n, 256)
    return pl.pallas_call(
        _mm_body,
        grid=(m // tm, n // tn),
        in_specs=[_row_spec(tm, k, m > tm), pl.BlockSpec((None, k, tn), lambda i, j: (layer, 0, j))],
        out_specs=pl.BlockSpec((tm, tn), lambda i, j: (i, j)),
        out_shape=jax.ShapeDtypeStruct((m, n), BF16),
        compiler_params=_cparams("parallel", "arbitrary", vmem=V7X_VMEM_LIMIT_DOWN_BYTES),
        name="matmul_down",
    )(a, w)


def _rope_tables(s):
    t = jnp.arange(s)
    row = (t // GRID_W).astype(F32)[:, None]
    col = (t % GRID_W).astype(F32)[:, None]
    axis_dim = NA_HEAD_DIM // 2
    inv_freq = ROPE_THETA ** (-jnp.arange(0, axis_dim, 2, dtype=F32) / axis_dim)
    ang = jnp.concatenate([row * inv_freq, row * inv_freq, col * inv_freq, col * inv_freq], axis=-1)
    quarter = NA_HEAD_DIM // 4
    negate = (jnp.arange(NA_HEAD_DIM) % (2 * quarter)) < quarter
    sin = jnp.sin(ang)
    return jnp.cos(ang), jnp.where(negate[None, :], -sin, sin)


def _bias_table_body(rpb_ref, onehot_ref, o_ref):
    o_ref[...] = jnp.dot(rpb_ref[...], onehot_ref[...], preferred_element_type=F32,
                         precision=lax.Precision.HIGHEST)


def _na_bias_table(rpb):
    h, nr, nc = rpb.shape
    cq = np.arange(GRID_W)[:, None]
    ck = np.arange(GRID_W)[None, :]
    d_col = (np.clip(ck - cq, -(NA_KW - 1), NA_KW - 1) + (NA_KW - 1)).reshape(-1)
    ncp = -(-nc // 8) * 8
    rows_p = -(-(h * nr) // 8) * 8
    onehot = np.zeros((ncp, GRID_W * GRID_W), np.float32)
    onehot[d_col, np.arange(GRID_W * GRID_W)] = 1.0
    rpb2 = jnp.pad(rpb.reshape(h * nr, nc), ((0, rows_p - h * nr), (0, ncp - nc)))
    out = pl.pallas_call(
        _bias_table_body,
        out_shape=jax.ShapeDtypeStruct((rows_p, GRID_W * GRID_W), F32),
        compiler_params=_cparams(),
        name="na_bias_table",
    )(rpb2, jnp.asarray(onehot))
    return out[:h * nr].reshape(h, nr, GRID_W, GRID_W)


def _na_bias_classes(table):
    h = table.shape[0]
    span = NA_QROWS + NA_KH - 1
    tp = jnp.pad(table, ((0, 0), (span, span), (0, 0), (0, 0)))
    bases = (0, -(NA_KH // 2), -(NA_KH - 1))
    valid = np.zeros((3, NA_QROWS, span), bool)
    tiles = []
    for cls, base in enumerate(bases):
        for a in range(NA_QROWS):
            lo = base - a + (NA_KH - 1) + span
            tiles.append(tp[:, lo:lo + span])
            for j in range(span):
                rel = base + j - a
                if cls == 0:
                    valid[cls, a, j] = j < NA_KH
                elif cls == 1:
                    valid[cls, a, j] = -(NA_KH // 2) <= rel < NA_KH - NA_KH // 2
                else:
                    valid[cls, a, j] = j >= span - NA_KH
    cq = np.arange(GRID_W)[:, None]
    ck = np.arange(GRID_W)[None, :]
    col_start = np.clip(cq - NA_KW // 2, 0, GRID_W - NA_KW)
    col_ok = (ck >= col_start) & (ck < col_start + NA_KW)
    t = jnp.stack(tiles, axis=1).reshape(h, 3, NA_QROWS, span, GRID_W, GRID_W)
    mask = valid[None, :, :, :, None, None] & col_ok[None, None, None, None, :, :]
    t = jnp.where(mask, t, MASK_VALUE)
    return t.transpose(0, 1, 2, 4, 3, 5).reshape(h, 3, NA_QROWS * GRID_W, span * GRID_W)


def _na_body(q_ref, k_ref, v_ref, kc_ref, vc_ref, bias_ref, cos_ref, sin_ref, o_ref,
             qr_s, kr_s, qb_s, vb_s, *, rows, scale):
    cos = cos_ref[...]
    sin = sin_ref[...]
    quarter = NA_HEAD_DIM // 4
    lane = lax.broadcasted_iota(jnp.int32, cos.shape, 1)
    first = (lane % (2 * quarter)) < quarter

    def rope(z):
        rot = jnp.where(first, pltpu.roll(z, NA_HEAD_DIM - quarter, 1), pltpu.roll(z, quarter, 1))
        return z * cos + rot * sin

    q = q_ref[...]
    qr_s[...] = (rope(q) * scale).astype(BF16)
    qb_s[...] = (q * scale).astype(BF16)
    kr_s[...] = rope(k_ref[...]).astype(BF16)
    vb_s[...] = v_ref[...].astype(BF16)
    kc = kc_ref[...].astype(BF16)
    vc = vc_ref[...].astype(BF16)
    nt = (((1,), (1,)), ((), ()))
    groups = rows // NA_QROWS
    span = NA_QROWS + NA_KH - 1
    qn = NA_QROWS * GRID_W
    kn = span * GRID_W

    def body(g, carry):
        ks = jnp.clip(g * NA_QROWS - NA_KH // 2, 0, rows - span)
        cls = jnp.where(g == 0, 0, jnp.where(g == groups - 1, 2, 1))
        q0 = pl.multiple_of(g * qn, qn)
        k0 = pl.multiple_of(ks * GRID_W, GRID_W)
        s_win = lax.dot_general(qr_s[pl.ds(q0, qn), :], kr_s[pl.ds(k0, kn), :], nt,
                                preferred_element_type=F32) + bias_ref[cls]
        s_ctx = lax.dot_general(qb_s[pl.ds(q0, qn), :], kc, nt, preferred_element_type=F32)
        m = jnp.maximum(jnp.max(s_win, axis=-1, keepdims=True), jnp.max(s_ctx, axis=-1, keepdims=True))
        e_win = jnp.exp(s_win - m)
        e_ctx = jnp.exp(s_ctx - m)
        den = jnp.sum(e_win, axis=-1, keepdims=True) + jnp.sum(e_ctx, axis=-1, keepdims=True)
        o = jnp.dot(e_win.astype(BF16), vb_s[pl.ds(k0, kn), :], preferred_element_type=F32)
        o += jnp.dot(e_ctx.astype(BF16), vc, preferred_element_type=F32)
        o_ref[pl.ds(q0, qn), :] = (o / den).astype(o_ref.dtype)
        return carry

    lax.fori_loop(0, groups, body, 0, unroll=4)


def _neighbourhood_attention(p3, kvc3, bias_cls, cos, sin, *, off_q, off_k, off_v, off_kc, off_vc, heads):
    bsz, s, _ = p3.shape
    lc = kvc3.shape[1]
    rows = s // GRID_W
    span = NA_QROWS + NA_KH - 1
    assert s % GRID_W == 0 and rows % NA_QROWS == 0 and rows // NA_QROWS >= 3
    dh = NA_HEAD_DIM

    def col(off):
        return pl.BlockSpec((None, s, dh), lambda h, b: (b, 0, off + h))

    def colc(off):
        return pl.BlockSpec((None, lc, dh), lambda h, b: (b, 0, off + h))

    tab = pl.BlockSpec((s, dh), lambda h, b: (0, 0))
    return pl.pallas_call(
        functools.partial(_na_body, rows=rows, scale=dh ** -0.5),
        grid=(heads, bsz),
        in_specs=[col(off_q), col(off_k), col(off_v), colc(off_kc), colc(off_vc),
                  pl.BlockSpec((None, 3, NA_QROWS * GRID_W, span * GRID_W), lambda h, b: (h, 0, 0, 0)), tab, tab],
        out_specs=pl.BlockSpec((None, s, dh), lambda h, b: (b, 0, h)),
        out_shape=jax.ShapeDtypeStruct((bsz, s, heads * dh), BF16),
        scratch_shapes=[pltpu.VMEM((s, dh), BF16)] * 4,
        compiler_params=_cparams("parallel", "arbitrary"),
        name="neighbourhood_attention",
    )(p3, p3, p3, kvc3, kvc3, bias_cls, cos, sin)


def _ctx_attn_body(q_ref, k_ref, v_ref, o_ref, *, scale):
    nt = (((1,), (1,)), ((), ()))
    s = lax.dot_general(q_ref[...].astype(BF16), k_ref[...].astype(BF16), nt, preferred_element_type=F32) * scale
    m = jnp.max(s, axis=-1, keepdims=True)
    e = jnp.exp(s - m)
    den = jnp.sum(e, axis=-1, keepdims=True)
    o = jnp.dot(e.astype(BF16), v_ref[...].astype(BF16), preferred_element_type=F32)
    o_ref[...] = (o / den).astype(o_ref.dtype)


def _context_attention(pc3, *, off_q, off_k, off_v, heads):
    bsz, lc, _ = pc3.shape
    dh = NA_HEAD_DIM

    def col(off):
        return pl.BlockSpec((None, lc, dh), lambda b, h: (b, 0, off + h))

    return pl.pallas_call(
        functools.partial(_ctx_attn_body, scale=dh ** -0.5),
        grid=(bsz, heads),
        in_specs=[col(off_q), col(off_k), col(off_v)],
        out_specs=pl.BlockSpec((None, lc, dh), lambda b, h: (b, 0, h)),
        out_shape=jax.ShapeDtypeStruct((bsz, lc, heads * dh), BF16),
        compiler_params=_cparams("parallel", "parallel"),
        name="context_attention",
    )(pc3, pc3, pc3)


def _conformer_body(a_ref, g_ref, pa_ref, pg_ref, na_ref, ng_ref, w_ref, wb_ref, lg_ref, lb_ref, o_ref, u_s, sh_s,
                    *, ts, taps):
    i = pl.program_id(1)
    halo = CONV_HALO_ROWS
    a = a_ref[...]
    g = g_ref[...]
    u_s[halo:halo + ts, :] = a * _sigmoid(g)
    prev = pa_ref[...] * _sigmoid(pg_ref[...])
    u_s[0:halo, :] = jnp.where(i > 0, prev, 0.0)
    nxt = na_ref[...] * _sigmoid(ng_ref[...])
    u_s[halo + ts:, :] = jnp.where(i < pl.num_programs(1) - 1, nxt, 0.0)
    span = ts + 2 * halo - SUBLANES
    for r in range(SUBLANES):
        sh_s[r] = u_s[r:r + span, :]
    base = halo - taps // 2
    z = wb_ref[...]
    for t in range(taps):
        a8, r = divmod(base + t, SUBLANES)
        z += w_ref[t:t + 1, :] * sh_s[r, a8 * SUBLANES:a8 * SUBLANES + ts, :]
    mu = jnp.mean(z, axis=-1, keepdims=True)
    zc = z - mu
    var = jnp.mean(zc * zc, axis=-1, keepdims=True)
    zn = zc * lax.rsqrt(var + LN_EPS) * lg_ref[...] + lb_ref[...]
    o_ref[...] = (zn * _sigmoid(zn)).astype(o_ref.dtype)


def _conformer(p3, dw_w, dw_b, ln_g, ln_b, *, off):
    bsz, l, _ = p3.shape
    taps, c = dw_w.shape
    halo = CONV_HALO_ROWS
    assert taps // 2 < halo and l % halo == 0
    ts = _tile(l, 256)
    hb = ts // halo
    nh = l // halo

    def cur(o):
        return pl.BlockSpec((None, ts, c), lambda b, i: (b, i, o))

    def prev(o):
        return pl.BlockSpec((None, halo, c), lambda b, i: (b, jnp.maximum(i * hb - 1, 0), o))

    def nxt(o):
        return pl.BlockSpec((None, halo, c), lambda b, i: (b, jnp.minimum((i + 1) * hb, nh - 1), o))

    par = pl.BlockSpec((1, c), lambda b, i: (0, 0))
    return pl.pallas_call(
        functools.partial(_conformer_body, ts=ts, taps=taps),
        grid=(bsz, l // ts),
        in_specs=[cur(off), cur(off + 1), prev(off), prev(off + 1), nxt(off), nxt(off + 1),
                  pl.BlockSpec((taps, c), lambda b, i: (0, 0)), par, par, par],
        out_specs=pl.BlockSpec((None, ts, c), lambda b, i: (b, i, 0)),
        out_shape=jax.ShapeDtypeStruct((bsz, l, c), BF16),
        scratch_shapes=[pltpu.VMEM((ts + 2 * halo, c), F32),
                        pltpu.VMEM((SUBLANES, ts + 2 * halo - SUBLANES, c), F32)],
        compiler_params=_cparams("parallel", "arbitrary"),
        name="conformer",
    )(p3, p3, p3, p3, p3, p3, dw_w, dw_b.reshape(1, c), ln_g.reshape(1, c), ln_b.reshape(1, c))


def _hy_short_body(cur_ref, prev_ref, next_ref, w_ref, b_ref, v_ref, x1_ref, x2_ref, u_s, *, ts, c):
    i = pl.program_id(1)
    halo = CONV_HALO_ROWS
    u_s[halo:halo + ts, :] = cur_ref[...]
    u_s[0:halo, :] = jnp.where(i > 0, prev_ref[...], 0.0)
    u_s[halo + ts:, :] = jnp.where(i < pl.num_programs(1) - 1, next_ref[...], 0.0)
    z = (b_ref[...] + w_ref[0:1, :] * u_s[halo - 1:halo - 1 + ts, :] + w_ref[1:2, :] * u_s[halo:halo + ts, :]
         + w_ref[2:3, :] * u_s[halo + 1:halo + 1 + ts, :])
    v_ref[...] = z[:, 0:c].astype(v_ref.dtype)
    x1_ref[...] = z[:, c:2 * c]
    x2_ref[...] = z[:, 2 * c:3 * c]


def _hyena_short_conv(p3, short_w, short_b, c):
    bsz, l, _ = p3.shape
    halo = CONV_HALO_ROWS
    ts = _tile(l, 256)
    hb = ts // halo
    nh = l // halo
    w3 = 3 * c
    out = pl.BlockSpec((ts, c), lambda b, i: (i, b))
    return pl.pallas_call(
        functools.partial(_hy_short_body, ts=ts, c=c),
        grid=(bsz, l // ts),
        in_specs=[
            pl.BlockSpec((None, ts, w3), lambda b, i: (b, i, 0)),
            pl.BlockSpec((None, halo, w3), lambda b, i: (b, jnp.maximum(i * hb - 1, 0), 0)),
            pl.BlockSpec((None, halo, w3), lambda b, i: (b, jnp.minimum((i + 1) * hb, nh - 1), 0)),
            pl.BlockSpec((3, w3), lambda b, i: (0, 0)),
            pl.BlockSpec((1, w3), lambda b, i: (0, 0)),
        ],
        out_specs=[out, out, out],
        out_shape=[jax.ShapeDtypeStruct((l, bsz * c), BF16), jax.ShapeDtypeStruct((l, bsz * c), F32),
                   jax.ShapeDtypeStruct((l, bsz * c), F32)],
        scratch_shapes=[pltpu.VMEM((ts + 2 * halo, w3), F32)],
        compiler_params=_cparams("parallel", "arbitrary"),
        name="hyena_short_conv",
    )(p3, p3, p3, short_w, short_b.reshape(1, w3))


def _hyena_embedding(l):
    t = jnp.linspace(0.0, 1.0, l, dtype=F32)[:, None]
    bands = (HYENA_EMB_DIM - 1) // 2
    omega = (2.0 * math.pi / l) * jnp.arange(l, dtype=F32)[:, None]
    f = jnp.linspace(1e-4, bands - 1, bands, dtype=F32)[None, :]
    return jnp.concatenate([t, jnp.cos(f * omega), -jnp.sin(f * omega)], axis=-1)


def _hy_filter_body(emb_ref, w1_ref, b1_ref, w2_ref, b2_ref, w3_ref, b3_ref, freq_ref, w4_ref, delta_ref, o_ref,
                    hid_s):
    @pl.when(pl.program_id(0) == 0)
    def _():
        freq = freq_ref[...]
        hid = jnp.sin(freq * (jnp.dot(emb_ref[...], w1_ref[...], preferred_element_type=F32) + b1_ref[...]))
        hid = jnp.sin(freq * (jnp.dot(hid, w2_ref[...], preferred_element_type=F32) + b2_ref[...]))
        hid_s[...] = jnp.sin(freq * (jnp.dot(hid, w3_ref[...], preferred_element_type=F32) + b3_ref[...]))

    h = jnp.dot(hid_s[...], w4_ref[...], preferred_element_type=F32)
    h = h * jnp.exp(-emb_ref[:, 0:1] * delta_ref[...])
    h = h / (jnp.sum(jnp.abs(h), axis=0, keepdims=True) + 1e-6)
    o_ref[...] = h.astype(o_ref.dtype)


def _hyena_filters(emb, w1, b1, w2, b2, w3, b3, w4, freq, deltas4):
    l, e = emb.shape
    hf = w1.shape[1]
    n4 = w4.shape[1]
    tn = _tile(n4, 512)

    def full(shape):
        return pl.BlockSpec(shape, lambda j: (0, 0))

    return pl.pallas_call(
        _hy_filter_body,
        grid=(n4 // tn,),
        in_specs=[full((l, e)), full((e, hf)), full((1, hf)), full((hf, hf)), full((1, hf)), full((hf, hf)),
                  full((1, hf)), full((1, hf)),
                  pl.BlockSpec((hf, tn), lambda j: (0, j)), pl.BlockSpec((1, tn), lambda j: (0, j))],
        out_specs=pl.BlockSpec((l, tn), lambda j: (0, j)),
        out_shape=jax.ShapeDtypeStruct((l, n4), BF16),
        scratch_shapes=[pltpu.VMEM((l, hf), F32)],
        compiler_params=_cparams("arbitrary"),
        name="hyena_filters",
    )(emb, w1, b1.reshape(1, hf), w2, b2.reshape(1, hf), w3, b3.reshape(1, hf), freq.reshape(1, hf), w4, deltas4)


def _dft_table(l, tm):
    n = 2 * l
    half = tm // 2
    r = np.arange(n)
    f_np = (r // tm) * half + (r % half)
    is_im_np = (r % tm) >= half
    sq = 1 << (int(math.log2(l)) // 2)
    hi = l // sq
    f = jnp.asarray(f_np, jnp.int32)[:, None]
    w0 = 2.0 * math.pi / n
    a_hi = ((f * (jnp.arange(hi, dtype=jnp.int32)[None, :] * sq)) % n).astype(F32) * w0
    a_lo = ((f * jnp.arange(sq, dtype=jnp.int32)[None, :]) % n).astype(F32) * w0
    ch, sh = jnp.cos(a_hi)[:, :, None], jnp.sin(a_hi)[:, :, None]
    cl, sl = jnp.cos(a_lo)[:, None, :], jnp.sin(a_lo)[:, None, :]
    cosv = (ch * cl - sh * sl).reshape(n, l)
    sinv = (sh * cl + ch * sl).reshape(n, l)
    nyq = jnp.asarray(np.where(np.arange(l) % 2 == 0, 1.0, -1.0), F32)[None, :]
    is_im = jnp.asarray(is_im_np)[:, None]
    table = jnp.where(is_im, jnp.where(f == 0, nyq, -sinv), cosv)
    return table.astype(BF16)


def _two_sided_filter(filt, c, order):
    seqs = []
    for o in range(order):
        hf = filt[:, 2 * o * c:(2 * o + 1) * c]
        hb = filt[:, (2 * o + 1) * c:(2 * o + 2) * c]
        neg = jnp.concatenate([jnp.zeros((1, c), filt.dtype), jnp.flip(hb[1:], axis=0)], axis=0)
        seqs.append(jnp.concatenate([neg, hf], axis=0))
    return jnp.stack(seqs)


def _spec_body(a_ref, cur_ref, prev_ref, skip_ref, hb0_ref, gr_ref, gi_ref, *, n, lag0):
    i = pl.program_id(0)
    mm = pl.program_id(2)
    a = a_ref[...]
    half = a.shape[0] // 2
    prev = prev_ref[...]
    sc_ = jnp.dot(a, cur_ref[...], preferred_element_type=F32)
    sp_ = jnp.dot(a, prev, preferred_element_type=F32)
    c_prev = prev[0:1, :].astype(F32)
    row = lax.broadcasted_iota(jnp.int32, (half, 1), 0)
    is_dc = jnp.logical_and(row == 0, i == 0)
    sigma = (1 - 2 * (row % 2)).astype(F32)
    gr = sc_[:half] + sigma * (sp_[:half] - c_prev)
    gi = jnp.where(is_dc, sc_[half:] + sp_[half:] - c_prev, sc_[half:] + sigma * sp_[half:])
    add = jnp.where(mm == lag0, skip_ref[...] + hb0_ref[...].astype(F32), 0.0)
    gr = gr + add
    gi = jnp.where(is_dc, gi + add, gi)
    w = jnp.where(is_dc, 1.0 / n, 2.0 / n)
    gr_ref[...] = gr * w
    gi_ref[...] = gi * w


def _hyena_filter_spectrum(table, gseq, skip, hb0, tm):
    n, t = table.shape
    order, l2, c = gseq.shape
    nblk = l2 // t
    nlag = nblk - 1
    tn = _tile(c, 512)
    half = tm // 2
    out = pl.BlockSpec((None, None, half, tn), lambda i, o, mm, j: (o, mm, i, j))
    vec = pl.BlockSpec((None, 1, tn), lambda i, o, mm, j: (o, 0, j))
    return pl.pallas_call(
        functools.partial(_spec_body, n=n, lag0=nblk // 2 - 1),
        grid=(n // tm, order, nlag, c // tn),
        in_specs=[
            pl.BlockSpec((tm, t), lambda i, o, mm, j: (i, 0)),
            pl.BlockSpec((None, t, tn), lambda i, o, mm, j: (o, mm + 1, j)),
            pl.BlockSpec((None, t, tn), lambda i, o, mm, j: (o, mm, j)),
            vec, vec,
        ],
        out_specs=[out, out],
        out_shape=[jax.ShapeDtypeStruct((order, nlag, t, c), F32)] * 2,
        compiler_params=_cparams("parallel", "arbitrary", "arbitrary", "arbitrary"),
        name="hyena_filter_spectrum",
    )(table, gseq, gseq, skip.reshape(order, 1, c), hb0)


def _fwd_body(a_ref, u_ref, o_ref):
    o_ref[...] = jnp.dot(a_ref[...], u_ref[...], preferred_element_type=F32)


def _hyena_fwd(table, u, tm):
    n, t = table.shape
    l, ncol = u.shape
    tn = _tile(ncol, 1024)
    return pl.pallas_call(
        _fwd_body,
        grid=(n // tm, l // t, ncol // tn),
        in_specs=[pl.BlockSpec((tm, t), lambda i, jb, cc: (i, 0)),
                  pl.BlockSpec((t, tn), lambda i, jb, cc: (jb, cc))],
        out_specs=pl.BlockSpec((None, tm, tn), lambda i, jb, cc: (jb, i, cc)),
        out_shape=jax.ShapeDtypeStruct((l // t, n, ncol), F32),
        compiler_params=_cparams("parallel", "arbitrary", "arbitrary"),
        name="hyena_fwd_dft",
    )(table, u)


def _mix_body(u_ref, gr_ref, gi_ref, o_ref, *, nblk, half, chunk):
    i = pl.program_id(0)

    def rows(r0, size):
        re = pl.ds(r0, size)
        im = pl.ds(half + r0, size)
        for io in range(nblk):
            acc_r = None
            acc_i = None
            for jj in range(nblk):
                lag = io - jj + nblk - 1
                ur, ui = u_ref[jj, re, :], u_ref[jj, im, :]
                gr, gi = gr_ref[lag, pl.ds(r0, size), :], gi_ref[lag, pl.ds(r0, size), :]
                pr = ur * gr - ui * gi
                pi = ur * gi + ui * gr
                acc_r = pr if acc_r is None else acc_r + pr
                acc_i = pi if acc_i is None else acc_i + pi
            o_ref[io, re, :] = acc_r.astype(o_ref.dtype)
            o_ref[io, im, :] = acc_i.astype(o_ref.dtype)

    def body(rc, carry):
        rows(pl.multiple_of(rc * chunk, chunk), chunk)
        return carry

    lax.fori_loop(0, half // chunk, body, 0)

    @pl.when(i == 0)
    def _():
        for io in range(nblk):
            dc = None
            ny = None
            for jj in range(nblk):
                lag = io - jj + nblk - 1
                pr = u_ref[jj, 0:1, :] * gr_ref[lag, 0:1, :]
                pi = u_ref[jj, half:half + 1, :] * gi_ref[lag, 0:1, :]
                dc = pr if dc is None else dc + pr
                ny = pi if ny is None else ny + pi
            o_ref[io, 0:1, :] = dc.astype(o_ref.dtype)
            o_ref[io, half:half + 1, :] = ny.astype(o_ref.dtype)


def _hyena_mix(u, gr, gi, order, tm):
    nblk, n, ncol = u.shape
    c = gr.shape[3]
    tc = _tile(c, 256)
    cb = c // tc
    half = tm // 2
    chunk = min(64, half)
    gspec = pl.BlockSpec((None, 2 * nblk - 1, half, tc), lambda i, cc: (order, 0, i, cc % cb))
    return pl.pallas_call(
        functools.partial(_mix_body, nblk=nblk, half=half, chunk=chunk),
        grid=(n // tm, ncol // tc),
        in_specs=[pl.BlockSpec((nblk, tm, tc), lambda i, cc: (0, i, cc)), gspec, gspec],
        out_specs=pl.BlockSpec((nblk, tm, tc), lambda i, cc: (0, i, cc)),
        out_shape=jax.ShapeDtypeStruct((nblk, n, ncol), BF16),
        compiler_params=_cparams("parallel", "arbitrary"),
        name="hyena_block_mix",
    )(u, gr, gi)


def _inv_body(a_ref, y_ref, x_ref, o_ref, acc_ref):
    k_axis = 3
    kk = pl.program_id(k_axis)
    tn_dims = (((0,), (0,)), ((), ()))
    prod = lax.dot_general(a_ref[...], y_ref[...], tn_dims, preferred_element_type=F32)

    @pl.when(kk == 0)
    def _():
        acc_ref[...] = prod

    @pl.when(kk > 0)
    def _():
        acc_ref[...] += prod

    @pl.when(kk == pl.num_programs(k_axis) - 1)
    def _():
        o_ref[...] = (x_ref[...] * acc_ref[...]).astype(o_ref.dtype)


def _hyena_inv_gate(table, y, xg, c, *, token_major):
    n, t = table.shape
    nblk, _, ncol = y.shape
    l = nblk * t
    tmt = _tile(t, 1024)
    tb = t // tmt
    tk = _tile(n, 2048)
    tn = c if token_major else _tile(c, 1024)
    if token_major:
        out_spec = pl.BlockSpec((None, tmt, c), lambda io, ti, j, kk: (j, io * tb + ti, 0))
        out_shape = jax.ShapeDtypeStruct((ncol // c, l, c), BF16)
    else:
        out_spec = pl.BlockSpec((tmt, tn), lambda io, ti, j, kk: (io * tb + ti, j))
        out_shape = jax.ShapeDtypeStruct((l, ncol), BF16)
    return pl.pallas_call(
        _inv_body,
        grid=(nblk, tb, ncol // tn, n // tk),
        in_specs=[pl.BlockSpec((tk, tmt), lambda io, ti, j, kk: (kk, ti)),
                  pl.BlockSpec((None, tk, tn), lambda io, ti, j, kk: (io, kk, j)),
                  pl.BlockSpec((tmt, tn), lambda io, ti, j, kk: (io * tb + ti, j))],
        out_specs=out_spec,
        out_shape=out_shape,
        scratch_shapes=[pltpu.VMEM((tmt, tn), F32)],
        compiler_params=_cparams("parallel", "parallel", "parallel", "arbitrary"),
        name="hyena_inv_dft",
    )(table, y, xg)


def _hyena_group(p3, table, tm, emb, filt_w, short_w, short_b, skip, deltas4):
    order, c = skip.shape
    filt = _hyena_filters(emb, *filt_w, deltas4)
    gseq = _two_sided_filter(filt, c, order)
    hb0 = jnp.stack([filt[0:1, (2 * o + 1) * c:(2 * o + 2) * c] for o in range(order)])
    gr, gi = _hyena_filter_spectrum(table, gseq, skip, hb0, tm)
    v, x1, x2 = _hyena_short_conv(p3, short_w, short_b, c)
    y1 = _hyena_mix(_hyena_fwd(table, v, tm), gr, gi, 0, tm)
    z1 = _hyena_inv_gate(table, y1, x1, c, token_major=False)
    y2 = _hyena_mix(_hyena_fwd(table, z1, tm), gr, gi, 1, tm)
    return _hyena_inv_gate(table, y2, x2, c, token_major=True)


def kernel(x, c, ctx, c_ctx, w_mod_down, w_mod_up, b_mod, w_in, w_out, hy_short_w, hy_short_b, hy_filt_w1, hy_filt_b1, hy_filt_w2, hy_filt_b2, hy_filt_w3, hy_filt_b3, hy_filt_w4, hy_filt_freq, hy_skip, na_rpb, cf_dw_w, cf_dw_b, cf_norm_g, cf_norm_b, w_ffn_gate, w_ffn_up, w_ffn_down, ln_mix_g, ln_mix_b, ln_ffn_g, ln_ffn_b):
    bsz, s, d = x.shape
    lc = ctx.shape[1]
    depth = w_in.shape[0]
    in_cols = w_in.shape[2]
    w_hy = hy_skip.shape[2]
    w_cf = cf_dw_w.shape[2]
    w_na = (in_cols - 3 * w_hy - 2 * w_cf) // 3
    heads = w_na // NA_HEAD_DIM
    off_na = 3 * w_hy
    off_kv = off_na + w_na
    off_conf = off_na + 3 * w_na
    alpha = (2 * depth) ** 0.25
    dh = NA_HEAD_DIM
    assert bsz + 1 <= 8 and off_conf % w_cf == 0 and off_na % dh == 0

    cos, sin = _rope_tables(s)
    t_lat = min(HYENA_BLOCK, s)
    t_ctx = min(HYENA_BLOCK, lc)
    assert s % t_lat == 0 and lc % t_ctx == 0
    tm_lat = min(1024, 2 * t_lat)
    tm_ctx = min(1024, 2 * t_ctx)
    table_lat = _dft_table(t_lat, tm_lat)
    table_ctx = _dft_table(t_ctx, tm_ctx)
    emb_lat = _hyena_embedding(s)
    emb_ctx = _hyena_embedding(lc)
    deltas = jnp.abs(jnp.linspace(HYENA_MIN_DECAY, HYENA_MAX_DECAY, w_hy, dtype=F32))
    deltas4 = jnp.tile(deltas, 4)[None, :]

    cond8 = jnp.zeros((8, d), F32).at[:bsz].set(c).at[bsz].set(c_ctx)
    mods = _modulation_all(cond8, w_mod_down, w_mod_up, b_mod).reshape(depth, 8, N_MOD, d)

    def mod_lat(l, i):
        return mods[l, :bsz, i][:, None, :]

    def mod_ctx(l, i):
        return jnp.broadcast_to(mods[l, bsz, i][None, None, :], (bsz, 1, d))

    h = _modulate(x, mod_lat(0, 0), mod_lat(0, 1))
    hc = _modulate(ctx, mod_ctx(0, 0), mod_ctx(0, 1))

    for l in range(depth):
        last = l == depth - 1
        filt_w = (hy_filt_w1[l], hy_filt_b1[l], hy_filt_w2[l], hy_filt_b2[l], hy_filt_w3[l], hy_filt_b3[l],
                  hy_filt_w4[l], hy_filt_freq[l])
        bias_cls = _na_bias_classes(_na_bias_table(na_rpb[l]))

        hc2 = hc.reshape(bsz * lc, d)
        if last:
            kvc = _matmul(hc2, w_in, l, col0=off_kv, ncols=2 * w_na).reshape(bsz, lc, 2 * w_na)
            off_kc, off_vc = 0, heads
        else:
            pc = _matmul(hc2, w_in, l).reshape(bsz, lc, in_cols)
            kvc = pc
            off_kc, off_vc = off_kv // dh, (off_kv + w_na) // dh

        p = _matmul(h.reshape(bsz * s, d), w_in, l).reshape(bsz, s, in_cols)
        y_hy = _hyena_group(p, table_lat, tm_lat, emb_lat, filt_w, hy_short_w[l], hy_short_b[l], hy_skip[l], deltas4)
        y_na = _neighbourhood_attention(p, kvc, bias_cls, cos, sin, off_q=off_na // dh, off_k=off_kv // dh,
                                        off_v=(off_kv + w_na) // dh, off_kc=off_kc, off_vc=off_vc, heads=heads)
        y_cf = _conformer(p, cf_dw_w[l], cf_dw_b[l], cf_norm_g[l], cf_norm_b[l], off=off_conf // w_cf)
        y = _matmul_concat3(y_hy.reshape(bsz * s, w_hy), y_na.reshape(bsz * s, w_na), y_cf.reshape(bsz * s, w_cf),
                            w_out, l).reshape(bsz, s, d)
        x, h = _ln_residual(x, y, mod_lat(l, 2), ln_mix_g[l], ln_mix_b[l], alpha, mod_lat(l, 3), mod_lat(l, 4))

        if not last:
            yc_hy = _hyena_group(pc, table_ctx, tm_ctx, emb_ctx, filt_w, hy_short_w[l], hy_short_b[l], hy_skip[l],
                                 deltas4)
            yc_na = _context_attention(pc, off_q=off_na // dh, off_k=off_kv // dh, off_v=(off_kv + w_na) // dh,
                                       heads=heads)
            yc_cf = _conformer(pc, cf_dw_w[l], cf_dw_b[l], cf_norm_g[l], cf_norm_b[l], off=off_conf // w_cf)
            yc = _matmul_concat3(yc_hy.reshape(bsz * lc, w_hy), yc_na.reshape(bsz * lc, w_na),
                                 yc_cf.reshape(bsz * lc, w_cf), w_out, l).reshape(bsz, lc, d)
            ctx, hc = _ln_residual(ctx, yc, mod_ctx(l, 2), ln_mix_g[l], ln_mix_b[l], alpha, mod_ctx(l, 3),
                                   mod_ctx(l, 4))
            ac = _matmul_swiglu_in(hc.reshape(bsz * lc, d), w_ffn_gate, w_ffn_up, l)
            fc = _matmul_down(ac, w_ffn_down, l).reshape(bsz, lc, d)
            ctx, hc = _ln_residual(ctx, fc, mod_ctx(l, 5), ln_ffn_g[l], ln_ffn_b[l], alpha, mod_ctx(l + 1, 0),
                                   mod_ctx(l + 1, 1))

        a = _matmul_swiglu_in(h.reshape(bsz * s, d), w_ffn_gate, w_ffn_up, l)
        f = _matmul_down(a, w_ffn_down, l).reshape(bsz, s, d)
        if last:
            x, _ = _ln_residual(x, f, mod_lat(l, 5), ln_ffn_g[l], ln_ffn_b[l], alpha)
        else:
            x, h = _ln_residual(x, f, mod_lat(l, 5), ln_ffn_g[l], ln_ffn_b[l], alpha, mod_lat(l + 1, 0),
                                mod_lat(l + 1, 1))
    return x
```

```python
import functools
import math

import numpy as np
import jax
import jax.numpy as jnp
from jax import lax
from jax.experimental import pallas as pl
from jax.experimental.pallas import tpu as pltpu

F32 = jnp.float32
BF16 = jnp.bfloat16

GRID_W = 64
NA_HEAD_DIM = 128
NA_KH = 8
NA_KW = 16
NA_QROWS = 4
ROPE_THETA = 10000.0
HYENA_EMB_DIM = 33
HYENA_BLOCK = 1024
HYENA_TARGET = 1e-2
HYENA_MIN_DECAY = math.log(HYENA_TARGET) / 1.5
HYENA_MAX_DECAY = math.log(HYENA_TARGET) / 0.3
N_MOD = 6
LN_EPS = 1e-5
MASK_VALUE = -1e30

V7X_VMEM_LIMIT_BYTES = 56 * 1024 * 1024
V7X_VMEM_LIMIT_DOWN_BYTES = 60 * 1024 * 1024
SUBLANES = 8
CONV_HALO_ROWS = 16


def _cparams(*sem, vmem=V7X_VMEM_LIMIT_BYTES):
    return pltpu.CompilerParams(dimension_semantics=sem, vmem_limit_bytes=vmem)


def _tile(dim, pref, align=128):
    t = (min(dim, pref) // align) * align
    while t >= align:
        if dim % t == 0:
            return t
        t -= align
    return dim


def _sigmoid(z):
    return jax.nn.sigmoid(z)


def _mod_body(cond_ref, wd_ref, wu_ref, b_ref, o_ref, t_ref):
    @pl.when(pl.program_id(1) == 0)
    def _():
        cnd = cond_ref[...]
        t_ref[...] = jnp.dot(cnd * _sigmoid(cnd), wd_ref[...], preferred_element_type=F32)

    o_ref[...] = jnp.dot(t_ref[...], wu_ref[...], preferred_element_type=F32) + b_ref[...]


def _modulation_all(cond8, w_down, w_up, b_mod):
    depth, d, rank = w_down.shape
    n_out = w_up.shape[2]
    tn = _tile(n_out, 2048)
    return pl.pallas_call(
        _mod_body,
        grid=(depth, n_out // tn),
        in_specs=[
            pl.BlockSpec((8, d), lambda l, j: (0, 0)),
            pl.BlockSpec((None, d, rank), lambda l, j: (l, 0, 0)),
            pl.BlockSpec((None, rank, tn), lambda l, j: (l, 0, j)),
            pl.BlockSpec((None, 1, tn), lambda l, j: (l, 0, j)),
        ],
        out_specs=pl.BlockSpec((None, 8, tn), lambda l, j: (l, 0, j)),
        out_shape=jax.ShapeDtypeStruct((depth, 8, n_out), F32),
        scratch_shapes=[pltpu.VMEM((8, rank), F32)],
        compiler_params=_cparams("arbitrary", "arbitrary"),
        name="modulation",
    )(cond8, w_down, w_up, b_mod.reshape(depth, 1, n_out))


def _modulate_body(x_ref, shift_ref, scale_ref, o_ref):
    o_ref[...] = (x_ref[...] * (1.0 + scale_ref[...]) + shift_ref[...]).astype(o_ref.dtype)


def _modulate(x3, shift, scale):
    b, s, d = x3.shape
    ts = _tile(s, 256)
    tok = pl.BlockSpec((None, ts, d), lambda bi, i: (bi, i, 0))
    vec = pl.BlockSpec((None, 1, d), lambda bi, i: (bi, 0, 0))
    return pl.pallas_call(
        _modulate_body,
        grid=(b, s // ts),
        in_specs=[tok, vec, vec],
        out_specs=tok,
        out_shape=jax.ShapeDtypeStruct((b, s, d), BF16),
        compiler_params=_cparams("parallel", "parallel"),
        name="modulate",
    )(x3, shift, scale)


def _ln_body(*refs, alpha, with_h):
    if with_h:
        x_ref, y_ref, gate_ref, g_ref, b_ref, shift_ref, scale_ref, xo_ref, ho_ref = refs
    else:
        x_ref, y_ref, gate_ref, g_ref, b_ref, xo_ref = refs
    z = alpha * x_ref[...] + gate_ref[...] * y_ref[...].astype(F32)
    mu = jnp.mean(z, axis=-1, keepdims=True)
    zc = z - mu
    var = jnp.mean(zc * zc, axis=-1, keepdims=True)
    xn = zc * lax.rsqrt(var + LN_EPS) * g_ref[...] + b_ref[...]
    xo_ref[...] = xn
    if with_h:
        ho_ref[...] = (xn * (1.0 + scale_ref[...]) + shift_ref[...]).astype(ho_ref.dtype)


def _ln_residual(x3, y3, gate, g, b, alpha, shift=None, scale=None):
    bsz, s, d = x3.shape
    ts = _tile(s, 256)
    with_h = shift is not None
    tok = pl.BlockSpec((None, ts, d), lambda bi, i: (bi, i, 0))
    vec = pl.BlockSpec((None, 1, d), lambda bi, i: (bi, 0, 0))
    par = pl.BlockSpec((1, d), lambda bi, i: (0, 0))
    in_specs = [tok, tok, vec, par, par]
    args = [x3, y3, gate, g.reshape(1, d), b.reshape(1, d)]
    out_specs = [tok]
    out_shape = [jax.ShapeDtypeStruct((bsz, s, d), F32)]
    if with_h:
        in_specs += [vec, vec]
        args += [shift, scale]
        out_specs.append(tok)
        out_shape.append(jax.ShapeDtypeStruct((bsz, s, d), BF16))
    res = pl.pallas_call(
        functools.partial(_ln_body, alpha=alpha, with_h=with_h),
        grid=(bsz, s // ts),
        in_specs=in_specs,
        out_specs=out_specs,
        out_shape=out_shape,
        compiler_params=_cparams("parallel", "parallel"),
        name="ln_residual",
    )(*args)
    return (res[0], res[1]) if with_h else (res[0], None)


def _row_spec(tm, k, single_buffer):
    mode = dict(pipeline_mode=pl.Buffered(1)) if single_buffer else {}
    return pl.BlockSpec((tm, k), lambda i, j: (i, 0), **mode)


def _mm_tiles(m, ncols, tn_big=256):
    if m >= 2048:
        return _tile(m, 2048), _tile(ncols, tn_big)
    return m, _tile(ncols, 1024)


def _mm_body(a_ref, b_ref, o_ref):
    o_ref[...] = jnp.dot(a_ref[...], b_ref[...].astype(BF16), preferred_element_type=F32).astype(o_ref.dtype)


def _matmul(a, w, layer, *, col0=0, ncols=None, out_dtype=F32):
    m, k = a.shape
    ncols = w.shape[2] - col0 if ncols is None else ncols
    tm, tn = _mm_tiles(m, math.gcd(ncols, col0), tn_big=512)
    jb = col0 // tn
    return pl.pallas_call(
        _mm_body,
        grid=(m // tm, ncols // tn),
        in_specs=[
            _row_spec(tm, k, m > tm),
            pl.BlockSpec((None, k, tn), lambda i, j: (layer, 0, j + jb)),
        ],
        out_specs=pl.BlockSpec((tm, tn), lambda i, j: (i, j)),
        out_shape=jax.ShapeDtypeStruct((m, ncols), out_dtype),
        compiler_params=_cparams("parallel", "arbitrary"),
        name="matmul",
    )(a, w)


def _mm3_body(a1_ref, a2_ref, a3_ref, w_ref, o_ref, cat_s):
    c1 = a1_ref.shape[1]
    c2 = a2_ref.shape[1]

    @pl.when(pl.program_id(1) == 0)
    def _():
        cat_s[:, 0:c1] = a1_ref[...]
        cat_s[:, c1:c1 + c2] = a2_ref[...]
        cat_s[:, c1 + c2:] = a3_ref[...]

    o_ref[...] = jnp.dot(cat_s[...], w_ref[...].astype(BF16), preferred_element_type=F32).astype(o_ref.dtype)


def _matmul_concat3(a1, a2, a3, w, layer):
    m = a1.shape[0]
    k, n = w.shape[1], w.shape[2]
    assert a1.shape[1] + a2.shape[1] + a3.shape[1] == k
    tm, tn = _mm_tiles(m, n)
    return pl.pallas_call(
        _mm3_body,
        grid=(m // tm, n // tn),
        in_specs=[
            _row_spec(tm, a1.shape[1], m > tm),
            _row_spec(tm, a2.shape[1], m > tm),
            _row_spec(tm, a3.shape[1], m > tm),
            pl.BlockSpec((None, k, tn), lambda i, j: (layer, 0, j)),
        ],
        out_specs=pl.BlockSpec((tm, tn), lambda i, j: (i, j)),
        out_shape=jax.ShapeDtypeStruct((m, n), BF16),
        scratch_shapes=[pltpu.VMEM((tm, k), BF16)],
        compiler_params=_cparams("arbitrary", "arbitrary"),
        name="matmul_out",
    )(a1, a2, a3, w)


def _gu_body(a_ref, wg_ref, wu_ref, o_ref):
    a = a_ref[...]
    g = jnp.dot(a, wg_ref[...].astype(BF16), preferred_element_type=F32)
    u = jnp.dot(a, wu_ref[...].astype(BF16), preferred_element_type=F32)
    o_ref[...] = (g * _sigmoid(g) * u).astype(o_ref.dtype)


def _matmul_swiglu_in(a, wg, wu, layer):
    m, k = a.shape
    n = wg.shape[2]
    tm, tn = _mm_tiles(m, n)
    tn = min(tn, _tile(n, 512))
    wspec = pl.BlockSpec((None, k, tn), lambda i, j: (layer, 0, j))
    return pl.pallas_call(
        _gu_body,
        grid=(m // tm, n // tn),
        in_specs=[_row_spec(tm, k, m > tm), wspec, wspec],
        out_specs=pl.BlockSpec((tm, tn), lambda i, j: (i, j)),
        out_shape=jax.ShapeDtypeStruct((m, n), BF16),
        compiler_params=_cparams("parallel", "arbitrary"),
        name="matmul_swiglu_in",
    )(a, wg, wu)


def _matmul_down(a, w, layer):
    m, k = a.shape
    n = w.shape[2]
    tm = _tile(m, 1024)
    tn = _tile(n, 256)
    return pl.pallas_call(
        _mm_body,
        grid=(m // tm, n // tn),
        in_specs=[_row_spec(tm, k, m > tm), pl.BlockSpec((None, k, tn), lambda i, j: (layer, 0, j))],
        out_specs=pl.BlockSpec((tm, tn), lambda i, j: (i, j)),
        out_shape=jax.ShapeDtypeStruct((m, n), BF16),
        compiler_params=_cparams("parallel", "arbitrary", vmem=V7X_VMEM_LIMIT_DOWN_BYTES),
        name="matmul_down",
    )(a, w)


def _rope_tables(s):
    t = jnp.arange(s)
    row = (t // GRID_W).astype(F32)[:, None]
    col = (t % GRID_W).astype(F32)[:, None]
    axis_dim = NA_HEAD_DIM // 2
    inv_freq = ROPE_THETA ** (-jnp.arange(0, axis_dim, 2, dtype=F32) / axis_dim)
    ang = jnp.concatenate([row * inv_freq, row * inv_freq, col * inv_freq, col * inv_freq], axis=-1)
    quarter = NA_HEAD_DIM // 4
    negate = (jnp.arange(NA_HEAD_DIM) % (2 * quarter)) < quarter
    sin = jnp.sin(ang)
    return jnp.cos(ang), jnp.where(negate[None, :], -sin, sin)


NA_SPAN = NA_QROWS + NA_KH - 1
NA_DROWS = 2 * NA_KH
NA_CLASS_BASES = (0, -(NA_KH // 2), -(NA_KH - 1))


def _bias_table_body(rpb_ref, onehot_ref, mask_ref, o_ref):
    o_ref[...] = jnp.dot(rpb_ref[...], onehot_ref[...], preferred_element_type=F32,
                         precision=lax.Precision.HIGHEST) + mask_ref[...]


def _na_bias_pairs(rpb):
    h, nr, nc = rpb.shape
    w = GRID_W
    cq = np.arange(w)[:, None]
    ck = np.arange(w)[None, :]
    d_col = np.clip(ck - cq, -(NA_KW - 1), NA_KW - 1) + (NA_KW - 1)
    col_start = np.clip(cq - NA_KW // 2, 0, w - NA_KW)
    col_ok = (ck >= col_start) & (ck < col_start + NA_KW)
    ncp = -(-nc // 8) * 8
    onehot = np.zeros((2 * ncp, w, 2, w), np.float32)
    for half in range(2):
        onehot[half * ncp + d_col, cq, half, ck] = 1.0
    maskadd = np.where(col_ok, 0.0, MASK_VALUE).astype(np.float32)
    maskadd = np.broadcast_to(maskadd[:, None, :], (w, 2, w)).reshape(1, 2 * w * w)
    rp = jnp.pad(rpb, ((0, 0), (1, NA_DROWS - nr), (0, ncp - nc)))
    rpb_pairs = jnp.concatenate([rp[:, :NA_DROWS], rp[:, 1:NA_DROWS + 1]], axis=-1).reshape(h * NA_DROWS, 2 * ncp)
    out = pl.pallas_call(
        _bias_table_body,
        out_shape=jax.ShapeDtypeStruct((h * NA_DROWS, 2 * w * w), F32),
        compiler_params=_cparams(),
        name="na_bias_table",
    )(rpb_pairs, jnp.asarray(onehot.reshape(2 * ncp, 2 * w * w)), jnp.asarray(maskadd))
    return out.reshape(h, NA_DROWS, w, 2 * w)


def _na_window_valid(cls, a, j):
    if j >= NA_SPAN:
        return False
    if cls == 0:
        return j < NA_KH
    if cls == 2:
        return j >= NA_SPAN - NA_KH
    return 0 <= NA_CLASS_BASES[1] + j - a + NA_KH // 2 < NA_KH


def _na_build_bias(pair_ref, bias_s):
    w = GRID_W
    left = lax.broadcasted_iota(jnp.int32, (w, 2 * w), 1) < w
    for cls, base in enumerate(NA_CLASS_BASES):
        for a in range(NA_QROWS):
            for pj in range(-(-NA_SPAN // 2)):
                j0 = 2 * pj
                v0, v1 = _na_window_valid(cls, a, j0), _na_window_valid(cls, a, j0 + 1)
                width = 2 * w if j0 + 1 < NA_SPAN else w
                if v0 or v1:
                    tile = pair_ref[base + j0 - a + NA_KH]
                    if not v1:
                        tile = jnp.where(left, tile, MASK_VALUE)
                    elif not v0:
                        tile = jnp.where(left, MASK_VALUE, tile)
                else:
                    tile = jnp.full((w, 2 * w), MASK_VALUE, F32)
                bias_s[cls, a * w:(a + 1) * w, j0 * w:j0 * w + width] = tile[:, :width]


def _na_body(q_ref, k_ref, v_ref, kc_ref, vc_ref, pair_ref, cos_ref, sin_ref, o_ref,
             qr_s, kr_s, qb_s, vb_s, bias_s, *, rows, scale):
    @pl.when(pl.program_id(1) == 0)
    def _():
        _na_build_bias(pair_ref, bias_s)

    cos = cos_ref[...]
    sin = sin_ref[...]
    quarter = NA_HEAD_DIM // 4
    lane = lax.broadcasted_iota(jnp.int32, cos.shape, 1)
    first = (lane % (2 * quarter)) < quarter

    def rope(z):
        rot = jnp.where(first, pltpu.roll(z, NA_HEAD_DIM - quarter, 1), pltpu.roll(z, quarter, 1))
        return z * cos + rot * sin

    q = q_ref[...]
    qr_s[...] = (rope(q) * scale).astype(BF16)
    qb_s[...] = (q * scale).astype(BF16)
    kr_s[...] = rope(k_ref[...]).astype(BF16)
    vb_s[...] = v_ref[...].astype(BF16)
    kc = kc_ref[...].astype(BF16)
    vc = vc_ref[...].astype(BF16)
    nt = (((1,), (1,)), ((), ()))
    groups = rows // NA_QROWS
    qn = NA_QROWS * GRID_W
    kn = NA_SPAN * GRID_W

    def body(g, carry):
        ks = jnp.clip(g * NA_QROWS + NA_CLASS_BASES[1], 0, rows - NA_SPAN)
        cls = jnp.where(g == 0, 0, jnp.where(g == groups - 1, 2, 1))
        q0 = pl.multiple_of(g * qn, qn)
        k0 = pl.multiple_of(ks * GRID_W, GRID_W)
        s_win = lax.dot_general(qr_s[pl.ds(q0, qn), :], kr_s[pl.ds(k0, kn), :], nt,
                                preferred_element_type=F32) + bias_s[cls]
        s_ctx = lax.dot_general(qb_s[pl.ds(q0, qn), :], kc, nt, preferred_element_type=F32)
        m = jnp.maximum(jnp.max(s_win, axis=-1, keepdims=True), jnp.max(s_ctx, axis=-1, keepdims=True))
        e_win = jnp.exp(s_win - m)
        e_ctx = jnp.exp(s_ctx - m)
        den = jnp.sum(e_win, axis=-1, keepdims=True) + jnp.sum(e_ctx, axis=-1, keepdims=True)
        o = jnp.dot(e_win.astype(BF16), vb_s[pl.ds(k0, kn), :], preferred_element_type=F32)
        o += jnp.dot(e_ctx.astype(BF16), vc, preferred_element_type=F32)
        o_ref[pl.ds(q0, qn), :] = (o / den).astype(o_ref.dtype)
        return carry

    lax.fori_loop(0, groups, body, 0, unroll=4)


def _neighbourhood_attention(p3, kvc3, bias_pairs, cos, sin, *, off_q, off_k, off_v, off_kc, off_vc, heads):
    bsz, s, _ = p3.shape
    lc = kvc3.shape[1]
    rows = s // GRID_W
    assert s % GRID_W == 0 and rows % NA_QROWS == 0 and rows // NA_QROWS >= 3
    dh = NA_HEAD_DIM

    def col(off):
        return pl.BlockSpec((None, s, dh), lambda h, b: (b, 0, off + h))

    def colc(off):
        return pl.BlockSpec((None, lc, dh), lambda h, b: (b, 0, off + h))

    tab = pl.BlockSpec((s, dh), lambda h, b: (0, 0))
    return pl.pallas_call(
        functools.partial(_na_body, rows=rows, scale=dh ** -0.5),
        grid=(heads, bsz),
        in_specs=[col(off_q), col(off_k), col(off_v), colc(off_kc), colc(off_vc),
                  pl.BlockSpec((None, NA_DROWS, GRID_W, 2 * GRID_W), lambda h, b: (h, 0, 0, 0)), tab, tab],
        out_specs=pl.BlockSpec((None, s, dh), lambda h, b: (b, 0, h)),
        out_shape=jax.ShapeDtypeStruct((bsz, s, heads * dh), BF16),
        scratch_shapes=[pltpu.VMEM((s, dh), BF16)] * 4
        + [pltpu.VMEM((len(NA_CLASS_BASES), NA_QROWS * GRID_W, NA_SPAN * GRID_W), F32)],
        compiler_params=_cparams("arbitrary", "arbitrary"),
        name="neighbourhood_attention",
    )(p3, p3, p3, kvc3, kvc3, bias_pairs, cos, sin)


def _ctx_attn_body(q_ref, k_ref, v_ref, o_ref, *, scale):
    nt = (((1,), (1,)), ((), ()))
    s = lax.dot_general(q_ref[...].astype(BF16), k_ref[...].astype(BF16), nt, preferred_element_type=F32) * scale
    m = jnp.max(s, axis=-1, keepdims=True)
    e = jnp.exp(s - m)
    den = jnp.sum(e, axis=-1, keepdims=True)
    o = jnp.dot(e.astype(BF16), v_ref[...].astype(BF16), preferred_element_type=F32)
    o_ref[...] = (o / den).astype(o_ref.dtype)


def _context_attention(pc3, *, off_q, off_k, off_v, heads):
    bsz, lc, _ = pc3.shape
    dh = NA_HEAD_DIM

    def col(off):
        return pl.BlockSpec((None, lc, dh), lambda b, h: (b, 0, off + h))

    return pl.pallas_call(
        functools.partial(_ctx_attn_body, scale=dh ** -0.5),
        grid=(bsz, heads),
        in_specs=[col(off_q), col(off_k), col(off_v)],
        out_specs=pl.BlockSpec((None, lc, dh), lambda b, h: (b, 0, h)),
        out_shape=jax.ShapeDtypeStruct((bsz, lc, heads * dh), BF16),
        compiler_params=_cparams("parallel", "parallel"),
        name="context_attention",
    )(pc3, pc3, pc3)


def _conformer_body(a_ref, g_ref, pa_ref, pg_ref, na_ref, ng_ref, w_ref, wb_ref, lg_ref, lb_ref, o_ref, u_s, sh_s,
                    *, ts, taps):
    i = pl.program_id(1)
    halo = CONV_HALO_ROWS
    a = a_ref[...]
    g = g_ref[...]
    u_s[halo:halo + ts, :] = a * _sigmoid(g)
    prev = pa_ref[...] * _sigmoid(pg_ref[...])
    u_s[0:halo, :] = jnp.where(i > 0, prev, 0.0)
    nxt = na_ref[...] * _sigmoid(ng_ref[...])
    u_s[halo + ts:, :] = jnp.where(i < pl.num_programs(1) - 1, nxt, 0.0)
    span = ts + 2 * halo - SUBLANES
    for r in range(SUBLANES):
        sh_s[r] = u_s[r:r + span, :]
    base = halo - taps // 2
    z = wb_ref[...]
    for t in range(taps):
        a8, r = divmod(base + t, SUBLANES)
        z += w_ref[t:t + 1, :] * sh_s[r, a8 * SUBLANES:a8 * SUBLANES + ts, :]
    mu = jnp.mean(z, axis=-1, keepdims=True)
    zc = z - mu
    var = jnp.mean(zc * zc, axis=-1, keepdims=True)
    zn = zc * lax.rsqrt(var + LN_EPS) * lg_ref[...] + lb_ref[...]
    o_ref[...] = (zn * _sigmoid(zn)).astype(o_ref.dtype)


def _conformer(p3, dw_w, dw_b, ln_g, ln_b, *, off):
    bsz, l, _ = p3.shape
    taps, c = dw_w.shape
    halo = CONV_HALO_ROWS
    assert taps // 2 < halo and l % halo == 0
    ts = _tile(l, 256)
    hb = ts // halo
    nh = l // halo

    def cur(o):
        return pl.BlockSpec((None, ts, c), lambda b, i: (b, i, o))

    def prev(o):
        return pl.BlockSpec((None, halo, c), lambda b, i: (b, jnp.maximum(i * hb - 1, 0), o))

    def nxt(o):
        return pl.BlockSpec((None, halo, c), lambda b, i: (b, jnp.minimum((i + 1) * hb, nh - 1), o))

    par = pl.BlockSpec((1, c), lambda b, i: (0, 0))
    return pl.pallas_call(
        functools.partial(_conformer_body, ts=ts, taps=taps),
        grid=(bsz, l // ts),
        in_specs=[cur(off), cur(off + 1), prev(off), prev(off + 1), nxt(off), nxt(off + 1),
                  pl.BlockSpec((taps, c), lambda b, i: (0, 0)), par, par, par],
        out_specs=pl.BlockSpec((None, ts, c), lambda b, i: (b, i, 0)),
        out_shape=jax.ShapeDtypeStruct((bsz, l, c), BF16),
        scratch_shapes=[pltpu.VMEM((ts + 2 * halo, c), F32),
                        pltpu.VMEM((SUBLANES, ts + 2 * halo - SUBLANES, c), F32)],
        compiler_params=_cparams("parallel", "arbitrary"),
        name="conformer",
    )(p3, p3, p3, p3, p3, p3, dw_w, dw_b.reshape(1, c), ln_g.reshape(1, c), ln_b.reshape(1, c))


def _hy_short_body(cur_ref, prev_ref, next_ref, w_ref, b_ref, v_ref, x1_ref, x2_ref, u_s, *, ts, c):
    i = pl.program_id(1)
    halo = CONV_HALO_ROWS
    u_s[halo:halo + ts, :] = cur_ref[...]
    u_s[0:halo, :] = jnp.where(i > 0, prev_ref[...], 0.0)
    u_s[halo + ts:, :] = jnp.where(i < pl.num_programs(1) - 1, next_ref[...], 0.0)
    z = (b_ref[...] + w_ref[0:1, :] * u_s[halo - 1:halo - 1 + ts, :] + w_ref[1:2, :] * u_s[halo:halo + ts, :]
         + w_ref[2:3, :] * u_s[halo + 1:halo + 1 + ts, :])
    v_ref[...] = z[:, 0:c].astype(v_ref.dtype)
    x1_ref[...] = z[:, c:2 * c]
    x2_ref[...] = z[:, 2 * c:3 * c]


def _hyena_short_conv(p3, short_w, short_b, c):
    bsz, l, _ = p3.shape
    halo = CONV_HALO_ROWS
    ts = _tile(l, 256)
    hb = ts // halo
    nh = l // halo
    w3 = 3 * c
    out = pl.BlockSpec((ts, c), lambda b, i: (i, b))
    return pl.pallas_call(
        functools.partial(_hy_short_body, ts=ts, c=c),
        grid=(bsz, l // ts),
        in_specs=[
            pl.BlockSpec((None, ts, w3), lambda b, i: (b, i, 0)),
            pl.BlockSpec((None, halo, w3), lambda b, i: (b, jnp.maximum(i * hb - 1, 0), 0)),
            pl.BlockSpec((None, halo, w3), lambda b, i: (b, jnp.minimum((i + 1) * hb, nh - 1), 0)),
            pl.BlockSpec((3, w3), lambda b, i: (0, 0)),
            pl.BlockSpec((1, w3), lambda b, i: (0, 0)),
        ],
        out_specs=[out, out, out],
        out_shape=[jax.ShapeDtypeStruct((l, bsz * c), BF16), jax.ShapeDtypeStruct((l, bsz * c), F32),
                   jax.ShapeDtypeStruct((l, bsz * c), F32)],
        scratch_shapes=[pltpu.VMEM((ts + 2 * halo, w3), F32)],
        compiler_params=_cparams("parallel", "arbitrary"),
        name="hyena_short_conv",
    )(p3, p3, p3, short_w, short_b.reshape(1, w3))


def _hyena_embedding(l):
    t = jnp.linspace(0.0, 1.0, l, dtype=F32)[:, None]
    bands = (HYENA_EMB_DIM - 1) // 2
    omega = (2.0 * math.pi / l) * jnp.arange(l, dtype=F32)[:, None]
    f = jnp.linspace(1e-4, bands - 1, bands, dtype=F32)[None, :]
    return jnp.concatenate([t, jnp.cos(f * omega), -jnp.sin(f * omega)], axis=-1)


def _hyena_embedding_pair(l):
    emb = _hyena_embedding(l)
    return jnp.stack([jnp.concatenate([emb[0:1], jnp.flip(emb[1:], axis=0)], axis=0), emb])


def _hy_filter_body(emb_ref, w1_ref, b1_ref, w2_ref, b2_ref, w3_ref, b3_ref, freq_ref, w4_ref, delta_ref,
                    g_ref, hb0_ref, hid_s):
    dirn = pl.program_id(2)
    first = jnp.logical_and(jnp.logical_and(pl.program_id(0) == 0, pl.program_id(1) == 0), dirn == 0)

    @pl.when(first)
    def _():
        freq = freq_ref[...]
        for d in range(2):
            hid = jnp.sin(freq * (jnp.dot(emb_ref[d], w1_ref[...], preferred_element_type=F32) + b1_ref[...]))
            hid = jnp.sin(freq * (jnp.dot(hid, w2_ref[...], preferred_element_type=F32) + b2_ref[...]))
            hid_s[d] = jnp.sin(freq * (jnp.dot(hid, w3_ref[...], preferred_element_type=F32) + b3_ref[...]))

    t = emb_ref[dirn][:, 0:1]
    h = jnp.dot(hid_s[dirn], w4_ref[...], preferred_element_type=F32)
    h = h * jnp.exp(-t * delta_ref[...])
    h = h / (jnp.sum(jnp.abs(h), axis=0, keepdims=True) + 1e-6)
    is_bwd = dirn == 0

    @pl.when(is_bwd)
    def _():
        hb0_ref[...] = h[0:1, :].astype(hb0_ref.dtype)

    row = lax.broadcasted_iota(jnp.int32, (h.shape[0], 1), 0)
    g_ref[...] = jnp.where(jnp.logical_and(row == 0, is_bwd), 0.0, h).astype(g_ref.dtype)


def _hyena_filters(emb2, w1, b1, w2, b2, w3, b3, w4, freq, deltas, order):
    _, l, e = emb2.shape
    hf = w1.shape[1]
    c = w4.shape[1] // (2 * order)
    tn = _tile(c, 512)
    cb = c // tn

    def full(shape):
        return pl.BlockSpec(shape, lambda o, ct, dn: (0,) * len(shape))

    return pl.pallas_call(
        _hy_filter_body,
        grid=(order, cb, 2),
        in_specs=[full((2, l, e)), full((e, hf)), full((1, hf)), full((hf, hf)), full((1, hf)), full((hf, hf)),
                  full((1, hf)), full((1, hf)),
                  pl.BlockSpec((hf, tn), lambda o, ct, dn: (0, (2 * o + 1 - dn) * cb + ct)),
                  pl.BlockSpec((1, tn), lambda o, ct, dn: (0, ct))],
        out_specs=[pl.BlockSpec((None, l, tn), lambda o, ct, dn: (o, dn, ct)),
                   pl.BlockSpec((None, 1, tn), lambda o, ct, dn: (o, 0, ct))],
        out_shape=[jax.ShapeDtypeStruct((order, 2 * l, c), BF16), jax.ShapeDtypeStruct((order, 1, c), BF16)],
        scratch_shapes=[pltpu.VMEM((2, l, hf), F32)],
        compiler_params=_cparams("arbitrary", "arbitrary", "arbitrary"),
        name="hyena_filters",
    )(emb2, w1, b1.reshape(1, hf), w2, b2.reshape(1, hf), w3, b3.reshape(1, hf), freq.reshape(1, hf), w4, deltas)


def _dft_table(l, tm):
    n = 2 * l
    half = tm // 2
    r = np.arange(n)
    f_np = (r // tm) * half + (r % half)
    is_im_np = (r % tm) >= half
    sq = 1 << (int(math.log2(l)) // 2)
    hi = l // sq
    f = jnp.asarray(f_np, jnp.int32)[:, None]
    w0 = 2.0 * math.pi / n
    a_hi = ((f * (jnp.arange(hi, dtype=jnp.int32)[None, :] * sq)) % n).astype(F32) * w0
    a_lo = ((f * jnp.arange(sq, dtype=jnp.int32)[None, :]) % n).astype(F32) * w0
    ch, sh = jnp.cos(a_hi)[:, :, None], jnp.sin(a_hi)[:, :, None]
    cl, sl = jnp.cos(a_lo)[:, None, :], jnp.sin(a_lo)[:, None, :]
    cosv = (ch * cl - sh * sl).reshape(n, l)
    sinv = (sh * cl + ch * sl).reshape(n, l)
    nyq = jnp.asarray(np.where(np.arange(l) % 2 == 0, 1.0, -1.0), F32)[None, :]
    is_im = jnp.asarray(is_im_np)[:, None]
    table = jnp.where(is_im, jnp.where(f == 0, nyq, -sinv), cosv)
    return table.astype(BF16)


def _spec_body(a_ref, cur_ref, prev_ref, skip_ref, hb0_ref, gr_ref, gi_ref, *, n, lag0):
    i = pl.program_id(0)
    mm = pl.program_id(2)
    a = a_ref[...]
    half = a.shape[0] // 2
    prev = prev_ref[...]
    sc_ = jnp.dot(a, cur_ref[...], preferred_element_type=F32)
    sp_ = jnp.dot(a, prev, preferred_element_type=F32)
    c_prev = prev[0:1, :].astype(F32)
    row = lax.broadcasted_iota(jnp.int32, (half, 1), 0)
    is_dc = jnp.logical_and(row == 0, i == 0)
    sigma = (1 - 2 * (row % 2)).astype(F32)
    gr = sc_[:half] + sigma * (sp_[:half] - c_prev)
    gi = jnp.where(is_dc, sc_[half:] + sp_[half:] - c_prev, sc_[half:] + sigma * sp_[half:])
    add = jnp.where(mm == lag0, skip_ref[...] + hb0_ref[...].astype(F32), 0.0)
    gr = gr + add
    gi = jnp.where(is_dc, gi + add, gi)
    w = jnp.where(is_dc, 1.0 / n, 2.0 / n)
    gr_ref[...] = gr * w
    gi_ref[...] = gi * w


def _hyena_filter_spectrum(table, gseq, skip, hb0, tm):
    n, t = table.shape
    order, l2, c = gseq.shape
    nblk = l2 // t
    nlag = nblk - 1
    tn = _tile(c, 512)
    half = tm // 2
    out = pl.BlockSpec((None, None, half, tn), lambda i, o, mm, j: (o, mm, i, j))
    vec = pl.BlockSpec((None, 1, tn), lambda i, o, mm, j: (o, 0, j))
    return pl.pallas_call(
        functools.partial(_spec_body, n=n, lag0=nblk // 2 - 1),
        grid=(n // tm, order, nlag, c // tn),
        in_specs=[
            pl.BlockSpec((tm, t), lambda i, o, mm, j: (i, 0)),
            pl.BlockSpec((None, t, tn), lambda i, o, mm, j: (o, mm + 1, j)),
            pl.BlockSpec((None, t, tn), lambda i, o, mm, j: (o, mm, j)),
            vec, vec,
        ],
        out_specs=[out, out],
        out_shape=[jax.ShapeDtypeStruct((order, nlag, t, c), F32)] * 2,
        compiler_params=_cparams("parallel", "arbitrary", "arbitrary", "arbitrary"),
        name="hyena_filter_spectrum",
    )(table, gseq, gseq, skip.reshape(order, 1, c), hb0)


def _fwd_body(a_ref, u_ref, o_ref):
    o_ref[...] = jnp.dot(a_ref[...], u_ref[...], preferred_element_type=F32)


def _hyena_fwd(table, u, tm):
    n, t = table.shape
    l, ncol = u.shape
    tn = _tile(ncol, 1024)
    return pl.pallas_call(
        _fwd_body,
        grid=(n // tm, l // t, ncol // tn),
        in_specs=[pl.BlockSpec((tm, t), lambda i, jb, cc: (i, 0)),
                  pl.BlockSpec((t, tn), lambda i, jb, cc: (jb, cc))],
        out_specs=pl.BlockSpec((None, tm, tn), lambda i, jb, cc: (jb, i, cc)),
        out_shape=jax.ShapeDtypeStruct((l // t, n, ncol), F32),
        compiler_params=_cparams("parallel", "arbitrary", "arbitrary"),
        name="hyena_fwd_dft",
    )(table, u)


def _mix_body(u_ref, gr_ref, gi_ref, o_ref, *, nblk, half, chunk):
    i = pl.program_id(0)

    def rows(r0, size):
        re = pl.ds(r0, size)
        im = pl.ds(half + r0, size)
        for io in range(nblk):
            acc_r = None
            acc_i = None
            for jj in range(nblk):
                lag = io - jj + nblk - 1
                ur, ui = u_ref[jj, re, :], u_ref[jj, im, :]
                gr, gi = gr_ref[lag, pl.ds(r0, size), :], gi_ref[lag, pl.ds(r0, size), :]
                pr = ur * gr - ui * gi
                pi = ur * gi + ui * gr
                acc_r = pr if acc_r is None else acc_r + pr
                acc_i = pi if acc_i is None else acc_i + pi
            o_ref[io, re, :] = acc_r.astype(o_ref.dtype)
            o_ref[io, im, :] = acc_i.astype(o_ref.dtype)

    def body(rc, carry):
        rows(pl.multiple_of(rc * chunk, chunk), chunk)
        return carry

    lax.fori_loop(0, half // chunk, body, 0)

    @pl.when(i == 0)
    def _():
        for io in range(nblk):
            dc = None
            ny = None
            for jj in range(nblk):
                lag = io - jj + nblk - 1
                pr = u_ref[jj, 0:1, :] * gr_ref[lag, 0:1, :]
                pi = u_ref[jj, half:half + 1, :] * gi_ref[lag, 0:1, :]
                dc = pr if dc is None else dc + pr
                ny = pi if ny is None else ny + pi
            o_ref[io, 0:1, :] = dc.astype(o_ref.dtype)
            o_ref[io, half:half + 1, :] = ny.astype(o_ref.dtype)


def _hyena_mix(u, gr, gi, order, tm):
    nblk, n, ncol = u.shape
    c = gr.shape[3]
    tc = _tile(c, 256)
    cb = c // tc
    half = tm // 2
    chunk = min(64, half)
    gspec = pl.BlockSpec((None, 2 * nblk - 1, half, tc), lambda i, cc: (order, 0, i, cc % cb))
    return pl.pallas_call(
        functools.partial(_mix_body, nblk=nblk, half=half, chunk=chunk),
        grid=(n // tm, ncol // tc),
        in_specs=[pl.BlockSpec((nblk, tm, tc), lambda i, cc: (0, i, cc)), gspec, gspec],
        out_specs=pl.BlockSpec((nblk, tm, tc), lambda i, cc: (0, i, cc)),
        out_shape=jax.ShapeDtypeStruct((nblk, n, ncol), BF16),
        compiler_params=_cparams("parallel", "arbitrary"),
        name="hyena_block_mix",
    )(u, gr, gi)


def _inv_body(a_ref, y_ref, x_ref, o_ref, acc_ref):
    k_axis = 3
    kk = pl.program_id(k_axis)
    tn_dims = (((0,), (0,)), ((), ()))
    prod = lax.dot_general(a_ref[...], y_ref[...], tn_dims, preferred_element_type=F32)

    @pl.when(kk == 0)
    def _():
        acc_ref[...] = prod

    @pl.when(kk > 0)
    def _():
        acc_ref[...] += prod

    @pl.when(kk == pl.num_programs(k_axis) - 1)
    def _():
        o_ref[...] = (x_ref[...] * acc_ref[...]).astype(o_ref.dtype)


def _hyena_inv_gate(table, y, xg, c, *, token_major):
    n, t = table.shape
    nblk, _, ncol = y.shape
    l = nblk * t
    tmt = _tile(t, 1024)
    tb = t // tmt
    tk = _tile(n, 2048)
    tn = c if token_major else _tile(c, 1024)
    if token_major:
        out_spec = pl.BlockSpec((None, tmt, c), lambda io, ti, j, kk: (j, io * tb + ti, 0))
        out_shape = jax.ShapeDtypeStruct((ncol // c, l, c), BF16)
    else:
        out_spec = pl.BlockSpec((tmt, tn), lambda io, ti, j, kk: (io * tb + ti, j))
        out_shape = jax.ShapeDtypeStruct((l, ncol), BF16)
    return pl.pallas_call(
        _inv_body,
        grid=(nblk, tb, ncol // tn, n // tk),
        in_specs=[pl.BlockSpec((tk, tmt), lambda io, ti, j, kk: (kk, ti)),
                  pl.BlockSpec((None, tk, tn), lambda io, ti, j, kk: (io, kk, j)),
                  pl.BlockSpec((tmt, tn), lambda io, ti, j, kk: (io * tb + ti, j))],
        out_specs=out_spec,
        out_shape=out_shape,
        scratch_shapes=[pltpu.VMEM((tmt, tn), F32)],
        compiler_params=_cparams("parallel", "parallel", "parallel", "arbitrary"),
        name="hyena_inv_dft",
    )(table, y, xg)


def _hyena_group(p3, table, tm, emb2, filt_w, short_w, short_b, skip, deltas):
    order, c = skip.shape
    gseq, hb0 = _hyena_filters(emb2, *filt_w, deltas, order)
    gr, gi = _hyena_filter_spectrum(table, gseq, skip, hb0, tm)
    v, x1, x2 = _hyena_short_conv(p3, short_w, short_b, c)
    y1 = _hyena_mix(_hyena_fwd(table, v, tm), gr, gi, 0, tm)
    z1 = _hyena_inv_gate(table, y1, x1, c, token_major=False)
    y2 = _hyena_mix(_hyena_fwd(table, z1, tm), gr, gi, 1, tm)
    return _hyena_inv_gate(table, y2, x2, c, token_major=True)


def kernel(x, c, ctx, c_ctx, w_mod_down, w_mod_up, b_mod, w_in, w_out, hy_short_w, hy_short_b, hy_filt_w1, hy_filt_b1, hy_filt_w2, hy_filt_b2, hy_filt_w3, hy_filt_b3, hy_filt_w4, hy_filt_freq, hy_skip, na_rpb, cf_dw_w, cf_dw_b, cf_norm_g, cf_norm_b, w_ffn_gate, w_ffn_up, w_ffn_down, ln_mix_g, ln_mix_b, ln_ffn_g, ln_ffn_b):
    bsz, s, d = x.shape
    lc = ctx.shape[1]
    depth = w_in.shape[0]
    in_cols = w_in.shape[2]
    w_hy = hy_skip.shape[2]
    w_cf = cf_dw_w.shape[2]
    w_na = (in_cols - 3 * w_hy - 2 * w_cf) // 3
    heads = w_na // NA_HEAD_DIM
    off_na = 3 * w_hy
    off_kv = off_na + w_na
    off_conf = off_na + 3 * w_na
    alpha = (2 * depth) ** 0.25
    dh = NA_HEAD_DIM
    assert bsz + 1 <= 8 and off_conf % w_cf == 0 and off_na % dh == 0

    cos, sin = _rope_tables(s)
    t_lat = min(HYENA_BLOCK, s)
    t_ctx = min(HYENA_BLOCK, lc)
    assert s % t_lat == 0 and lc % t_ctx == 0
    tm_lat = min(1024, 2 * t_lat)
    tm_ctx = min(1024, 2 * t_ctx)
    table_lat = _dft_table(t_lat, tm_lat)
    table_ctx = _dft_table(t_ctx, tm_ctx)
    emb_lat = _hyena_embedding_pair(s)
    emb_ctx = _hyena_embedding_pair(lc)
    deltas = jnp.abs(jnp.linspace(HYENA_MIN_DECAY, HYENA_MAX_DECAY, w_hy, dtype=F32))
    deltas = deltas[None, :]

    cond8 = jnp.zeros((8, d), F32).at[:bsz].set(c).at[bsz].set(c_ctx)
    mods = _modulation_all(cond8, w_mod_down, w_mod_up, b_mod).reshape(depth, 8, N_MOD, d)

    def mod_lat(l, i):
        return mods[l, :bsz, i][:, None, :]

    def mod_ctx(l, i):
        return jnp.broadcast_to(mods[l, bsz, i][None, None, :], (bsz, 1, d))

    h = _modulate(x, mod_lat(0, 0), mod_lat(0, 1))
    hc = _modulate(ctx, mod_ctx(0, 0), mod_ctx(0, 1))

    for l in range(depth):
        last = l == depth - 1
        filt_w = (hy_filt_w1[l], hy_filt_b1[l], hy_filt_w2[l], hy_filt_b2[l], hy_filt_w3[l], hy_filt_b3[l],
                  hy_filt_w4[l], hy_filt_freq[l])
        bias_cls = _na_bias_pairs(na_rpb[l])

        hc2 = hc.reshape(bsz * lc, d)
        if last:
            kvc = _matmul(hc2, w_in, l, col0=off_kv, ncols=2 * w_na).reshape(bsz, lc, 2 * w_na)
            off_kc, off_vc = 0, heads
        else:
            pc = _matmul(hc2, w_in, l).reshape(bsz, lc, in_cols)
            kvc = pc
            off_kc, off_vc = off_kv // dh, (off_kv + w_na) // dh

        p = _matmul(h.reshape(bsz * s, d), w_in, l).reshape(bsz, s, in_cols)
        y_hy = _hyena_group(p, table_lat, tm_lat, emb_lat, filt_w, hy_short_w[l], hy_short_b[l], hy_skip[l], deltas)
        y_na = _neighbourhood_attention(p, kvc, bias_cls, cos, sin, off_q=off_na // dh, off_k=off_kv // dh,
                                        off_v=(off_kv + w_na) // dh, off_kc=off_kc, off_vc=off_vc, heads=heads)
        y_cf = _conformer(p, cf_dw_w[l], cf_dw_b[l], cf_norm_g[l], cf_norm_b[l], off=off_conf // w_cf)
        y = _matmul_concat3(y_hy.reshape(bsz * s, w_hy), y_na.reshape(bsz * s, w_na), y_cf.reshape(bsz * s, w_cf),
                            w_out, l).reshape(bsz, s, d)
        x, h = _ln_residual(x, y, mod_lat(l, 2), ln_mix_g[l], ln_mix_b[l], alpha, mod_lat(l, 3), mod_lat(l, 4))

        if not last:
            yc_hy = _hyena_group(pc, table_ctx, tm_ctx, emb_ctx, filt_w, hy_short_w[l], hy_short_b[l], hy_skip[l],
                                 deltas)
            yc_na = _context_attention(pc, off_q=off_na // dh, off_k=off_kv // dh, off_v=(off_kv + w_na) // dh,
                                       heads=heads)
            yc_cf = _conformer(pc, cf_dw_w[l], cf_dw_b[l], cf_norm_g[l], cf_norm_b[l], off=off_conf // w_cf)
            yc = _matmul_concat3(yc_hy.reshape(bsz * lc, w_hy), yc_na.reshape(bsz * lc, w_na),
                                 yc_cf.reshape(bsz * lc, w_cf), w_out, l).reshape(bsz, lc, d)
            ctx, hc = _ln_residual(ctx, yc, mod_ctx(l, 2), ln_mix_g[l], ln_mix_b[l], alpha, mod_ctx(l, 3),
                                   mod_ctx(l, 4))
            ac = _matmul_swiglu_in(hc.reshape(bsz * lc, d), w_ffn_gate, w_ffn_up, l)
            fc = _matmul_down(ac, w_ffn_down, l).reshape(bsz, lc, d)
            ctx, hc = _ln_residual(ctx, fc, mod_ctx(l, 5), ln_ffn_g[l], ln_ffn_b[l], alpha, mod_ctx(l + 1, 0),
                                   mod_ctx(l + 1, 1))

        a = _matmul_swiglu_in(h.reshape(bsz * s, d), w_ffn_gate, w_ffn_up, l)
        f = _matmul_down(a, w_ffn_down, l).reshape(bsz, s, d)
        if last:
            x, _ = _ln_residual(x, f, mod_lat(l, 5), ln_ffn_g[l], ln_ffn_b[l], alpha)
        else:
            x, h = _ln_residual(x, f, mod_lat(l, 5), ln_ffn_g[l], ln_ffn_b[l], alpha, mod_lat(l + 1, 0),
                                mod_lat(l + 1, 1))
    return x
```

```python
import functools
import math

import numpy as np
import jax
import jax.numpy as jnp
from jax import lax
from jax.experimental import pallas as pl
from jax.experimental.pallas import tpu as pltpu

F32 = jnp.float32
BF16 = jnp.bfloat16

GRID_W = 64
NA_HEAD_DIM = 128
NA_KH = 8
NA_KW = 16
NA_QROWS = 4
ROPE_THETA = 10000.0
HYENA_EMB_DIM = 33
HYENA_BLOCK = 1024
HYENA_TARGET = 1e-2
HYENA_MIN_DECAY = math.log(HYENA_TARGET) / 1.5
HYENA_MAX_DECAY = math.log(HYENA_TARGET) / 0.3
N_MOD = 6
LN_EPS = 1e-5
MASK_VALUE = -1e30

V7X_VMEM_LIMIT_BYTES = 56 * 1024 * 1024
V7X_VMEM_LIMIT_DOWN_BYTES = 60 * 1024 * 1024
SUBLANES = 8
CONV_HALO_ROWS = 16


def _cparams(*sem, vmem=V7X_VMEM_LIMIT_BYTES):
    return pltpu.CompilerParams(dimension_semantics=sem, vmem_limit_bytes=vmem)


def _tile(dim, pref, align=128):
    t = (min(dim, pref) // align) * align
    while t >= align:
        if dim % t == 0:
            return t
        t -= align
    return dim


def _sigmoid(z):
    return jax.nn.sigmoid(z)


def _mod_body(cond_ref, wd_ref, wu_ref, b_ref, o_ref, t_ref):
    @pl.when(pl.program_id(1) == 0)
    def _():
        cnd = cond_ref[...]
        t_ref[...] = jnp.dot(cnd * _sigmoid(cnd), wd_ref[...], preferred_element_type=F32)

    o_ref[...] = jnp.dot(t_ref[...], wu_ref[...], preferred_element_type=F32) + b_ref[...]


def _modulation_all(cond8, w_down, w_up, b_mod):
    depth, d, rank = w_down.shape
    n_out = w_up.shape[2]
    tn = _tile(n_out, 2048)
    return pl.pallas_call(
        _mod_body,
        grid=(depth, n_out // tn),
        in_specs=[
            pl.BlockSpec((8, d), lambda l, j: (0, 0)),
            pl.BlockSpec((None, d, rank), lambda l, j: (l, 0, 0)),
            pl.BlockSpec((None, rank, tn), lambda l, j: (l, 0, j)),
            pl.BlockSpec((None, 1, tn), lambda l, j: (l, 0, j)),
        ],
        out_specs=pl.BlockSpec((None, 8, tn), lambda l, j: (l, 0, j)),
        out_shape=jax.ShapeDtypeStruct((depth, 8, n_out), F32),
        scratch_shapes=[pltpu.VMEM((8, rank), F32)],
        compiler_params=_cparams("arbitrary", "arbitrary"),
        name="modulation",
    )(cond8, w_down, w_up, b_mod.reshape(depth, 1, n_out))


def _modulate_body(x_ref, shift_ref, scale_ref, o_ref):
    o_ref[...] = (x_ref[...] * (1.0 + scale_ref[...]) + shift_ref[...]).astype(o_ref.dtype)


def _modulate(x3, shift, scale):
    b, s, d = x3.shape
    ts = _tile(s, 256)
    tok = pl.BlockSpec((None, ts, d), lambda bi, i: (bi, i, 0))
    vec = pl.BlockSpec((None, 1, d), lambda bi, i: (bi, 0, 0))
    return pl.pallas_call(
        _modulate_body,
        grid=(b, s // ts),
        in_specs=[tok, vec, vec],
        out_specs=tok,
        out_shape=jax.ShapeDtypeStruct((b, s, d), BF16),
        compiler_params=_cparams("parallel", "parallel"),
        name="modulate",
    )(x3, shift, scale)


def _ln_body(*refs, with_h):
    if with_h:
        z_ref, g_ref, b_ref, shift_ref, scale_ref, xo_ref, ho_ref = refs
    else:
        z_ref, g_ref, b_ref, xo_ref = refs
    z = z_ref[...]
    mu = jnp.mean(z, axis=-1, keepdims=True)
    zc = z - mu
    var = jnp.mean(zc * zc, axis=-1, keepdims=True)
    xn = zc * lax.rsqrt(var + LN_EPS) * g_ref[...] + b_ref[...]
    xo_ref[...] = xn
    if with_h:
        ho_ref[...] = (xn * (1.0 + scale_ref[...]) + shift_ref[...]).astype(ho_ref.dtype)


def _layer_norm(z3, g, b, shift=None, scale=None):
    bsz, s, d = z3.shape
    ts = _tile(s, 256)
    with_h = shift is not None
    tok = pl.BlockSpec((None, ts, d), lambda bi, i: (bi, i, 0))
    vec = pl.BlockSpec((None, 1, d), lambda bi, i: (bi, 0, 0))
    par = pl.BlockSpec((1, d), lambda bi, i: (0, 0))
    in_specs = [tok, par, par]
    args = [z3, g.reshape(1, d), b.reshape(1, d)]
    out_specs = [tok]
    out_shape = [jax.ShapeDtypeStruct((bsz, s, d), F32)]
    if with_h:
        in_specs += [vec, vec]
        args += [shift, scale]
        out_specs.append(tok)
        out_shape.append(jax.ShapeDtypeStruct((bsz, s, d), BF16))
    res = pl.pallas_call(
        functools.partial(_ln_body, with_h=with_h),
        grid=(bsz, s // ts),
        in_specs=in_specs,
        out_specs=out_specs,
        out_shape=out_shape,
        compiler_params=_cparams("parallel", "parallel"),
        name="ln_residual",
    )(*args)
    return (res[0], res[1]) if with_h else (res[0], None)


def _row_spec(tm, k, single_buffer):
    mode = dict(pipeline_mode=pl.Buffered(1)) if single_buffer else {}
    return pl.BlockSpec((tm, k), lambda i, j: (i, 0), **mode)


def _mm_tiles(m, ncols, tn_big=256):
    if m >= 2048:
        return _tile(m, 2048), _tile(ncols, tn_big)
    return m, _tile(ncols, 1024)


def _mm_body(a_ref, b_ref, o_ref):
    o_ref[...] = jnp.dot(a_ref[...], b_ref[...].astype(BF16), preferred_element_type=F32).astype(o_ref.dtype)


def _matmul(a, w, layer, *, col0=0, ncols=None, out_dtype=F32):
    m, k = a.shape
    ncols = w.shape[2] - col0 if ncols is None else ncols
    tm, tn = _mm_tiles(m, math.gcd(ncols, col0), tn_big=512)
    jb = col0 // tn
    return pl.pallas_call(
        _mm_body,
        grid=(m // tm, ncols // tn),
        in_specs=[
            _row_spec(tm, k, m > tm),
            pl.BlockSpec((None, k, tn), lambda i, j: (layer, 0, j + jb)),
        ],
        out_specs=pl.BlockSpec((tm, tn), lambda i, j: (i, j)),
        out_shape=jax.ShapeDtypeStruct((m, ncols), out_dtype),
        compiler_params=_cparams("parallel", "arbitrary"),
        name="matmul",
    )(a, w)


def _res_tiles(bsz, rows, tr_big):
    if bsz * rows <= 1024:
        return bsz, rows
    return 1, _tile(rows, tr_big)


def _res_specs(bt, tr, tn, single_buffer):
    mode = dict(pipeline_mode=pl.Buffered(1)) if single_buffer else {}

    def rows(width):
        return pl.BlockSpec((bt, tr, width), lambda b, i, j: (b, i, 0), **mode)

    tile = pl.BlockSpec((bt, tr, tn), lambda b, i, j: (b, i, j))
    gate = pl.BlockSpec((bt, 1, tn), lambda b, i, j: (b, 0, j))
    return rows, tile, gate


def _mm3_res_body(a1_ref, a2_ref, a3_ref, w_ref, x_ref, gate_ref, o_ref, *, alpha):
    bt, tr, c1 = a1_ref.shape
    c2 = a2_ref.shape[2]
    c3 = a3_ref.shape[2]
    y = jnp.dot(a1_ref[...].reshape(bt * tr, c1), w_ref[0:c1, :].astype(BF16), preferred_element_type=F32)
    y += jnp.dot(a2_ref[...].reshape(bt * tr, c2), w_ref[c1:c1 + c2, :].astype(BF16), preferred_element_type=F32)
    y += jnp.dot(a3_ref[...].reshape(bt * tr, c3), w_ref[c1 + c2:, :].astype(BF16), preferred_element_type=F32)
    o_ref[...] = alpha * x_ref[...] + gate_ref[...] * y.reshape(bt, tr, y.shape[1])


def _matmul_concat3_residual(a1, a2, a3, w, layer, x3, gate, alpha):
    bsz, r, _ = a1.shape
    k, n = w.shape[1], w.shape[2]
    assert a1.shape[2] + a2.shape[2] + a3.shape[2] == k
    bt, tr = _res_tiles(bsz, r, 2048)
    tn = _tile(n, 256 if bt == 1 else 1024)
    rows, tile, gspec = _res_specs(bt, tr, tn, bsz * r > bt * tr)
    return pl.pallas_call(
        functools.partial(_mm3_res_body, alpha=alpha),
        grid=(bsz // bt, r // tr, n // tn),
        in_specs=[rows(a1.shape[2]), rows(a2.shape[2]), rows(a3.shape[2]),
                  pl.BlockSpec((None, k, tn), lambda b, i, j: (layer, 0, j)), tile, gspec],
        out_specs=tile,
        out_shape=jax.ShapeDtypeStruct((bsz, r, n), F32),
        compiler_params=_cparams("parallel", "parallel", "arbitrary"),
        name="matmul_out",
    )(a1, a2, a3, w, x3, gate)


def _gu_body(a_ref, wg_ref, wu_ref, o_ref):
    a = a_ref[...]
    g = jnp.dot(a, wg_ref[...].astype(BF16), preferred_element_type=F32)
    u = jnp.dot(a, wu_ref[...].astype(BF16), preferred_element_type=F32)
    o_ref[...] = (g * _sigmoid(g) * u).astype(o_ref.dtype)


def _matmul_swiglu_in(a, wg, wu, layer):
    m, k = a.shape
    n = wg.shape[2]
    tm, tn = _mm_tiles(m, n)
    tn = min(tn, _tile(n, 512))
    wspec = pl.BlockSpec((None, k, tn), lambda i, j: (layer, 0, j))
    return pl.pallas_call(
        _gu_body,
        grid=(m // tm, n // tn),
        in_specs=[_row_spec(tm, k, m > tm), wspec, wspec],
        out_specs=pl.BlockSpec((tm, tn), lambda i, j: (i, j)),
        out_shape=jax.ShapeDtypeStruct((m, n), BF16),
        compiler_params=_cparams("parallel", "arbitrary"),
        name="matmul_swiglu_in",
    )(a, wg, wu)


def _mm_res_body(a_ref, w_ref, x_ref, gate_ref, o_ref, *, alpha):
    bt, tr, k = a_ref.shape
    y = jnp.dot(a_ref[...].reshape(bt * tr, k), w_ref[...].astype(BF16), preferred_element_type=F32)
    o_ref[...] = alpha * x_ref[...] + gate_ref[...] * y.reshape(bt, tr, y.shape[1])


def _matmul_down_residual(a, w, layer, x3, gate, alpha):
    bsz, r, k = a.shape
    n = w.shape[2]
    bt, tr = _res_tiles(bsz, r, 1024)
    tn = _tile(n, 256)
    rows, tile, gspec = _res_specs(bt, tr, tn, bsz * r > bt * tr)
    return pl.pallas_call(
        functools.partial(_mm_res_body, alpha=alpha),
        grid=(bsz // bt, r // tr, n // tn),
        in_specs=[rows(k), pl.BlockSpec((None, k, tn), lambda b, i, j: (layer, 0, j)), tile, gspec],
        out_specs=tile,
        out_shape=jax.ShapeDtypeStruct((bsz, r, n), F32),
        compiler_params=_cparams("arbitrary", "arbitrary", "arbitrary", vmem=V7X_VMEM_LIMIT_DOWN_BYTES),
        name="matmul_down",
    )(a, w, x3, gate)


def _rope_tables(s):
    t = jnp.arange(s)
    row = (t // GRID_W).astype(F32)[:, None]
    col = (t % GRID_W).astype(F32)[:, None]
    axis_dim = NA_HEAD_DIM // 2
    inv_freq = ROPE_THETA ** (-jnp.arange(0, axis_dim, 2, dtype=F32) / axis_dim)
    ang = jnp.concatenate([row * inv_freq, row * inv_freq, col * inv_freq, col * inv_freq], axis=-1)
    quarter = NA_HEAD_DIM // 4
    negate = (jnp.arange(NA_HEAD_DIM) % (2 * quarter)) < quarter
    sin = jnp.sin(ang)
    return jnp.cos(ang), jnp.where(negate[None, :], -sin, sin)


NA_SPAN = NA_QROWS + NA_KH - 1
NA_DROWS = 2 * NA_KH
NA_CLASS_BASES = (0, -(NA_KH // 2), -(NA_KH - 1))


def _bias_table_body(rpb_ref, onehot_ref, mask_ref, o_ref):
    o_ref[...] = jnp.dot(rpb_ref[...], onehot_ref[...], preferred_element_type=F32,
                         precision=lax.Precision.HIGHEST) + mask_ref[...]


def _na_bias_pairs(rpb):
    h, nr, nc = rpb.shape
    w = GRID_W
    cq = np.arange(w)[:, None]
    ck = np.arange(w)[None, :]
    d_col = np.clip(ck - cq, -(NA_KW - 1), NA_KW - 1) + (NA_KW - 1)
    col_start = np.clip(cq - NA_KW // 2, 0, w - NA_KW)
    col_ok = (ck >= col_start) & (ck < col_start + NA_KW)
    ncp = -(-nc // 8) * 8
    onehot = np.zeros((2 * ncp, w, 2, w), np.float32)
    for half in range(2):
        onehot[half * ncp + d_col, cq, half, ck] = 1.0
    maskadd = np.where(col_ok, 0.0, MASK_VALUE).astype(np.float32)
    maskadd = np.broadcast_to(maskadd[:, None, :], (w, 2, w)).reshape(1, 2 * w * w)
    rp = jnp.pad(rpb, ((0, 0), (1, NA_DROWS - nr), (0, ncp - nc)))
    rpb_pairs = jnp.concatenate([rp[:, :NA_DROWS], rp[:, 1:NA_DROWS + 1]], axis=-1).reshape(h * NA_DROWS, 2 * ncp)
    out = pl.pallas_call(
        _bias_table_body,
        out_shape=jax.ShapeDtypeStruct((h * NA_DROWS, 2 * w * w), F32),
        compiler_params=_cparams(),
        name="na_bias_table",
    )(rpb_pairs, jnp.asarray(onehot.reshape(2 * ncp, 2 * w * w)), jnp.asarray(maskadd))
    return out.reshape(h, NA_DROWS, w, 2 * w)


def _na_window_valid(cls, a, j):
    if j >= NA_SPAN:
        return False
    if cls == 0:
        return j < NA_KH
    if cls == 2:
        return j >= NA_SPAN - NA_KH
    return 0 <= NA_CLASS_BASES[1] + j - a + NA_KH // 2 < NA_KH


def _na_build_bias(pair_ref, bias_s):
    w = GRID_W
    left = lax.broadcasted_iota(jnp.int32, (w, 2 * w), 1) < w
    for cls, base in enumerate(NA_CLASS_BASES):
        for a in range(NA_QROWS):
            for pj in range(-(-NA_SPAN // 2)):
                j0 = 2 * pj
                v0, v1 = _na_window_valid(cls, a, j0), _na_window_valid(cls, a, j0 + 1)
                width = 2 * w if j0 + 1 < NA_SPAN else w
                if v0 or v1:
                    tile = pair_ref[base + j0 - a + NA_KH]
                    if not v1:
                        tile = jnp.where(left, tile, MASK_VALUE)
                    elif not v0:
                        tile = jnp.where(left, MASK_VALUE, tile)
                else:
                    tile = jnp.full((w, 2 * w), MASK_VALUE, F32)
                bias_s[cls, a * w:(a + 1) * w, j0 * w:j0 * w + width] = tile[:, :width]


def _na_body(q_ref, k_ref, v_ref, kc_ref, vc_ref, pair_ref, cos_ref, sin_ref, o_ref,
             qr_s, kr_s, qb_s, vb_s, bias_s, *, rows, scale):
    @pl.when(pl.program_id(1) == 0)
    def _():
        _na_build_bias(pair_ref, bias_s)

    cos = cos_ref[...]
    sin = sin_ref[...]
    quarter = NA_HEAD_DIM // 4
    lane = lax.broadcasted_iota(jnp.int32, cos.shape, 1)
    first = (lane % (2 * quarter)) < quarter

    def rope(z):
        rot = jnp.where(first, pltpu.roll(z, NA_HEAD_DIM - quarter, 1), pltpu.roll(z, quarter, 1))
        return z * cos + rot * sin

    q = q_ref[...]
    qr_s[...] = (rope(q) * scale).astype(BF16)
    qb_s[...] = (q * scale).astype(BF16)
    kr_s[...] = rope(k_ref[...]).astype(BF16)
    vb_s[...] = v_ref[...].astype(BF16)
    kc = kc_ref[...].astype(BF16)
    vc = vc_ref[...].astype(BF16)
    nt = (((1,), (1,)), ((), ()))
    groups = rows // NA_QROWS
    qn = NA_QROWS * GRID_W
    kn = NA_SPAN * GRID_W

    def body(g, carry):
        ks = jnp.clip(g * NA_QROWS + NA_CLASS_BASES[1], 0, rows - NA_SPAN)
        cls = jnp.where(g == 0, 0, jnp.where(g == groups - 1, 2, 1))
        q0 = pl.multiple_of(g * qn, qn)
        k0 = pl.multiple_of(ks * GRID_W, GRID_W)
        s_win = lax.dot_general(qr_s[pl.ds(q0, qn), :], kr_s[pl.ds(k0, kn), :], nt,
                                preferred_element_type=F32) + bias_s[cls]
        s_ctx = lax.dot_general(qb_s[pl.ds(q0, qn), :], kc, nt, preferred_element_type=F32)
        m = jnp.maximum(jnp.max(s_win, axis=-1, keepdims=True), jnp.max(s_ctx, axis=-1, keepdims=True))
        e_win = jnp.exp(s_win - m)
        e_ctx = jnp.exp(s_ctx - m)
        den = jnp.sum(e_win, axis=-1, keepdims=True) + jnp.sum(e_ctx, axis=-1, keepdims=True)
        o = jnp.dot(e_win.astype(BF16), vb_s[pl.ds(k0, kn), :], preferred_element_type=F32)
        o += jnp.dot(e_ctx.astype(BF16), vc, preferred_element_type=F32)
        o_ref[pl.ds(q0, qn), :] = (o / den).astype(o_ref.dtype)
        return carry

    lax.fori_loop(0, groups, body, 0, unroll=4)


def _neighbourhood_attention(p3, kvc3, bias_pairs, cos, sin, *, off_q, off_k, off_v, off_kc, off_vc, heads):
    bsz, s, _ = p3.shape
    lc = kvc3.shape[1]
    rows = s // GRID_W
    assert s % GRID_W == 0 and rows % NA_QROWS == 0 and rows // NA_QROWS >= 3
    dh = NA_HEAD_DIM

    def col(off):
        return pl.BlockSpec((None, s, dh), lambda h, b: (b, 0, off + h))

    def colc(off):
        return pl.BlockSpec((None, lc, dh), lambda h, b: (b, 0, off + h))

    tab = pl.BlockSpec((s, dh), lambda h, b: (0, 0))
    return pl.pallas_call(
        functools.partial(_na_body, rows=rows, scale=dh ** -0.5),
        grid=(heads, bsz),
        in_specs=[col(off_q), col(off_k), col(off_v), colc(off_kc), colc(off_vc),
                  pl.BlockSpec((None, NA_DROWS, GRID_W, 2 * GRID_W), lambda h, b: (h, 0, 0, 0)), tab, tab],
        out_specs=pl.BlockSpec((None, s, dh), lambda h, b: (b, 0, h)),
        out_shape=jax.ShapeDtypeStruct((bsz, s, heads * dh), BF16),
        scratch_shapes=[pltpu.VMEM((s, dh), BF16)] * 4
        + [pltpu.VMEM((len(NA_CLASS_BASES), NA_QROWS * GRID_W, NA_SPAN * GRID_W), F32)],
        compiler_params=_cparams("arbitrary", "arbitrary"),
        name="neighbourhood_attention",
    )(p3, p3, p3, kvc3, kvc3, bias_pairs, cos, sin)


def _ctx_attn_body(q_ref, k_ref, v_ref, o_ref, *, scale):
    nt = (((1,), (1,)), ((), ()))
    s = lax.dot_general(q_ref[...].astype(BF16), k_ref[...].astype(BF16), nt, preferred_element_type=F32) * scale
    m = jnp.max(s, axis=-1, keepdims=True)
    e = jnp.exp(s - m)
    den = jnp.sum(e, axis=-1, keepdims=True)
    o = jnp.dot(e.astype(BF16), v_ref[...].astype(BF16), preferred_element_type=F32)
    o_ref[...] = (o / den).astype(o_ref.dtype)


def _context_attention(pc3, *, off_q, off_k, off_v, heads):
    bsz, lc, _ = pc3.shape
    dh = NA_HEAD_DIM

    def col(off):
        return pl.BlockSpec((None, lc, dh), lambda b, h: (b, 0, off + h))

    return pl.pallas_call(
        functools.partial(_ctx_attn_body, scale=dh ** -0.5),
        grid=(bsz, heads),
        in_specs=[col(off_q), col(off_k), col(off_v)],
        out_specs=pl.BlockSpec((None, lc, dh), lambda b, h: (b, 0, h)),
        out_shape=jax.ShapeDtypeStruct((bsz, lc, heads * dh), BF16),
        compiler_params=_cparams("parallel", "parallel"),
        name="context_attention",
    )(pc3, pc3, pc3)


def _conformer_body(a_ref, g_ref, pa_ref, pg_ref, na_ref, ng_ref, w_ref, wb_ref, lg_ref, lb_ref, o_ref, u_s, sh_s,
                    *, ts, taps):
    i = pl.program_id(1)
    halo = CONV_HALO_ROWS
    a = a_ref[...]
    g = g_ref[...]
    u_s[halo:halo + ts, :] = a * _sigmoid(g)
    prev = pa_ref[...] * _sigmoid(pg_ref[...])
    u_s[0:halo, :] = jnp.where(i > 0, prev, 0.0)
    nxt = na_ref[...] * _sigmoid(ng_ref[...])
    u_s[halo + ts:, :] = jnp.where(i < pl.num_programs(1) - 1, nxt, 0.0)
    span = ts + 2 * halo - SUBLANES
    for r in range(SUBLANES):
        sh_s[r] = u_s[r:r + span, :]
    base = halo - taps // 2
    z = wb_ref[...]
    for t in range(taps):
        a8, r = divmod(base + t, SUBLANES)
        z += w_ref[t:t + 1, :] * sh_s[r, a8 * SUBLANES:a8 * SUBLANES + ts, :]
    mu = jnp.mean(z, axis=-1, keepdims=True)
    zc = z - mu
    var = jnp.mean(zc * zc, axis=-1, keepdims=True)
    zn = zc * lax.rsqrt(var + LN_EPS) * lg_ref[...] + lb_ref[...]
    o_ref[...] = (zn * _sigmoid(zn)).astype(o_ref.dtype)


def _conformer(p3, dw_w, dw_b, ln_g, ln_b, *, off):
    bsz, l, _ = p3.shape
    taps, c = dw_w.shape
    halo = CONV_HALO_ROWS
    assert taps // 2 < halo and l % halo == 0
    ts = _tile(l, 256)
    hb = ts // halo
    nh = l // halo

    def cur(o):
        return pl.BlockSpec((None, ts, c), lambda b, i: (b, i, o))

    def prev(o):
        return pl.BlockSpec((None, halo, c), lambda b, i: (b, jnp.maximum(i * hb - 1, 0), o))

    def nxt(o):
        return pl.BlockSpec((None, halo, c), lambda b, i: (b, jnp.minimum((i + 1) * hb, nh - 1), o))

    par = pl.BlockSpec((1, c), lambda b, i: (0, 0))
    return pl.pallas_call(
        functools.partial(_conformer_body, ts=ts, taps=taps),
        grid=(bsz, l // ts),
        in_specs=[cur(off), cur(off + 1), prev(off), prev(off + 1), nxt(off), nxt(off + 1),
                  pl.BlockSpec((taps, c), lambda b, i: (0, 0)), par, par, par],
        out_specs=pl.BlockSpec((None, ts, c), lambda b, i: (b, i, 0)),
        out_shape=jax.ShapeDtypeStruct((bsz, l, c), BF16),
        scratch_shapes=[pltpu.VMEM((ts + 2 * halo, c), F32),
                        pltpu.VMEM((SUBLANES, ts + 2 * halo - SUBLANES, c), F32)],
        compiler_params=_cparams("parallel", "arbitrary"),
        name="conformer",
    )(p3, p3, p3, p3, p3, p3, dw_w, dw_b.reshape(1, c), ln_g.reshape(1, c), ln_b.reshape(1, c))


def _hy_short_body(cur_ref, prev_ref, next_ref, w_ref, b_ref, v_ref, x1_ref, x2_ref, u_s, *, ts, c):
    i = pl.program_id(1)
    halo = CONV_HALO_ROWS
    u_s[halo:halo + ts, :] = cur_ref[...]
    u_s[0:halo, :] = jnp.where(i > 0, prev_ref[...], 0.0)
    u_s[halo + ts:, :] = jnp.where(i < pl.num_programs(1) - 1, next_ref[...], 0.0)
    z = (b_ref[...] + w_ref[0:1, :] * u_s[halo - 1:halo - 1 + ts, :] + w_ref[1:2, :] * u_s[halo:halo + ts, :]
         + w_ref[2:3, :] * u_s[halo + 1:halo + 1 + ts, :])
    v_ref[...] = z[:, 0:c].astype(v_ref.dtype)
    x1_ref[...] = z[:, c:2 * c]
    x2_ref[...] = z[:, 2 * c:3 * c]


def _hyena_short_conv(p3, short_w, short_b, c):
    bsz, l, _ = p3.shape
    halo = CONV_HALO_ROWS
    ts = _tile(l, 256)
    hb = ts // halo
    nh = l // halo
    w3 = 3 * c
    out = pl.BlockSpec((ts, c), lambda b, i: (i, b))
    return pl.pallas_call(
        functools.partial(_hy_short_body, ts=ts, c=c),
        grid=(bsz, l // ts),
        in_specs=[
            pl.BlockSpec((None, ts, w3), lambda b, i: (b, i, 0)),
            pl.BlockSpec((None, halo, w3), lambda b, i: (b, jnp.maximum(i * hb - 1, 0), 0)),
            pl.BlockSpec((None, halo, w3), lambda b, i: (b, jnp.minimum((i + 1) * hb, nh - 1), 0)),
            pl.BlockSpec((3, w3), lambda b, i: (0, 0)),
            pl.BlockSpec((1, w3), lambda b, i: (0, 0)),
        ],
        out_specs=[out, out, out],
        out_shape=[jax.ShapeDtypeStruct((l, bsz * c), BF16), jax.ShapeDtypeStruct((l, bsz * c), F32),
                   jax.ShapeDtypeStruct((l, bsz * c), F32)],
        scratch_shapes=[pltpu.VMEM((ts + 2 * halo, w3), F32)],
        compiler_params=_cparams("parallel", "arbitrary"),
        name="hyena_short_conv",
    )(p3, p3, p3, short_w, short_b.reshape(1, w3))


def _hyena_embedding(l):
    t = jnp.linspace(0.0, 1.0, l, dtype=F32)[:, None]
    bands = (HYENA_EMB_DIM - 1) // 2
    omega = (2.0 * math.pi / l) * jnp.arange(l, dtype=F32)[:, None]
    f = jnp.linspace(1e-4, bands - 1, bands, dtype=F32)[None, :]
    return jnp.concatenate([t, jnp.cos(f * omega), -jnp.sin(f * omega)], axis=-1)


def _hyena_embedding_pair(l):
    emb = _hyena_embedding(l)
    return jnp.stack([jnp.concatenate([emb[0:1], jnp.flip(emb[1:], axis=0)], axis=0), emb])


def _hy_filter_body(emb_ref, embt_ref, w1_ref, b1_ref, w2_ref, b2_ref, w3_ref, b3_ref, freq_ref, w4_ref, delta_ref,
                    g_ref, hb0_ref, hid_s):
    dirn = pl.program_id(2)
    first = jnp.logical_and(jnp.logical_and(pl.program_id(0) == 0, pl.program_id(1) == 0), dirn == 0)
    tn_dims = (((0,), (0,)), ((), ()))

    @pl.when(first)
    def _():
        freq = freq_ref[...]
        for d in range(2):
            hid = lax.dot_general(w1_ref[...], embt_ref[d], tn_dims, preferred_element_type=F32)
            hid = jnp.sin(freq * (hid + b1_ref[...]))
            hid = jnp.sin(freq * (lax.dot_general(w2_ref[...], hid, tn_dims, preferred_element_type=F32) + b2_ref[...]))
            hid_s[d] = jnp.sin(freq * (lax.dot_general(w3_ref[...], hid, tn_dims, preferred_element_type=F32)
                                       + b3_ref[...]))

    t = emb_ref[dirn][:, 0:1]
    h = lax.dot_general(hid_s[dirn], w4_ref[...], tn_dims, preferred_element_type=F32)
    h = h * jnp.exp(-t * delta_ref[...])
    h = h / (jnp.sum(jnp.abs(h), axis=0, keepdims=True) + 1e-6)
    is_bwd = dirn == 0

    @pl.when(is_bwd)
    def _():
        hb0_ref[...] = h[0:1, :].astype(hb0_ref.dtype)

    row = lax.broadcasted_iota(jnp.int32, (h.shape[0], 1), 0)
    g_ref[...] = jnp.where(jnp.logical_and(row == 0, is_bwd), 0.0, h).astype(g_ref.dtype)


def _hyena_filters(emb2, w1, b1, w2, b2, w3, b3, w4, freq, deltas, order):
    _, l, e = emb2.shape
    hf = w1.shape[1]
    c = w4.shape[1] // (2 * order)
    tn = _tile(c, 512)
    cb = c // tn

    def full(shape):
        return pl.BlockSpec(shape, lambda o, ct, dn: (0,) * len(shape))

    return pl.pallas_call(
        _hy_filter_body,
        grid=(order, cb, 2),
        in_specs=[full((2, l, e)), full((2, e, l)), full((e, hf)), full((hf, 1)), full((hf, hf)), full((hf, 1)),
                  full((hf, hf)), full((hf, 1)), full((hf, 1)),
                  pl.BlockSpec((hf, tn), lambda o, ct, dn: (0, (2 * o + 1 - dn) * cb + ct)),
                  pl.BlockSpec((1, tn), lambda o, ct, dn: (0, ct))],
        out_specs=[pl.BlockSpec((None, l, tn), lambda o, ct, dn: (o, dn, ct)),
                   pl.BlockSpec((None, 1, tn), lambda o, ct, dn: (o, 0, ct))],
        out_shape=[jax.ShapeDtypeStruct((order, 2 * l, c), BF16), jax.ShapeDtypeStruct((order, 1, c), BF16)],
        scratch_shapes=[pltpu.VMEM((2, hf, l), F32)],
        compiler_params=_cparams("arbitrary", "arbitrary", "arbitrary"),
        name="hyena_filters",
    )(emb2, jnp.swapaxes(emb2, 1, 2), w1, b1.reshape(hf, 1), w2, b2.reshape(hf, 1), w3, b3.reshape(hf, 1),
      freq.reshape(hf, 1), w4, deltas)


def _dft_table(l, tm):
    n = 2 * l
    half = tm // 2
    r = np.arange(n)
    f_np = (r // tm) * half + (r % half)
    is_im_np = (r % tm) >= half
    sq = 1 << (int(math.log2(l)) // 2)
    hi = l // sq
    f = jnp.asarray(f_np, jnp.int32)[:, None]
    w0 = 2.0 * math.pi / n
    a_hi = ((f * (jnp.arange(hi, dtype=jnp.int32)[None, :] * sq)) % n).astype(F32) * w0
    a_lo = ((f * jnp.arange(sq, dtype=jnp.int32)[None, :]) % n).astype(F32) * w0
    ch, sh = jnp.cos(a_hi)[:, :, None], jnp.sin(a_hi)[:, :, None]
    cl, sl = jnp.cos(a_lo)[:, None, :], jnp.sin(a_lo)[:, None, :]
    cosv = (ch * cl - sh * sl).reshape(n, l)
    sinv = (sh * cl + ch * sl).reshape(n, l)
    nyq = jnp.asarray(np.where(np.arange(l) % 2 == 0, 1.0, -1.0), F32)[None, :]
    is_im = jnp.asarray(is_im_np)[:, None]
    table = jnp.where(is_im, jnp.where(f == 0, nyq, -sinv), cosv)
    return table.astype(BF16)


def _spec_body(a_ref, cur_ref, prev_ref, skip_ref, hb0_ref, gr_ref, gi_ref, *, n, lag0):
    i = pl.program_id(0)
    mm = pl.program_id(2)
    a = a_ref[...]
    half = a.shape[0] // 2
    prev = prev_ref[...]
    sc_ = jnp.dot(a, cur_ref[...], preferred_element_type=F32)
    sp_ = jnp.dot(a, prev, preferred_element_type=F32)
    c_prev = prev[0:1, :].astype(F32)
    row = lax.broadcasted_iota(jnp.int32, (half, 1), 0)
    is_dc = jnp.logical_and(row == 0, i == 0)
    sigma = (1 - 2 * (row % 2)).astype(F32)
    gr = sc_[:half] + sigma * (sp_[:half] - c_prev)
    gi = jnp.where(is_dc, sc_[half:] + sp_[half:] - c_prev, sc_[half:] + sigma * sp_[half:])
    add = jnp.where(mm == lag0, skip_ref[...] + hb0_ref[...].astype(F32), 0.0)
    gr = gr + add
    gi = jnp.where(is_dc, gi + add, gi)
    w = jnp.where(is_dc, 1.0 / n, 2.0 / n)
    gr_ref[...] = gr * w
    gi_ref[...] = gi * w


def _hyena_filter_spectrum(table, gseq, skip, hb0, tm):
    n, t = table.shape
    order, l2, c = gseq.shape
    nblk = l2 // t
    nlag = nblk - 1
    tn = _tile(c, 512)
    half = tm // 2
    out = pl.BlockSpec((None, None, half, tn), lambda i, o, mm, j: (o, mm, i, j))
    vec = pl.BlockSpec((None, 1, tn), lambda i, o, mm, j: (o, 0, j))
    return pl.pallas_call(
        functools.partial(_spec_body, n=n, lag0=nblk // 2 - 1),
        grid=(n // tm, order, nlag, c // tn),
        in_specs=[
            pl.BlockSpec((tm, t), lambda i, o, mm, j: (i, 0)),
            pl.BlockSpec((None, t, tn), lambda i, o, mm, j: (o, mm + 1, j)),
            pl.BlockSpec((None, t, tn), lambda i, o, mm, j: (o, mm, j)),
            vec, vec,
        ],
        out_specs=[out, out],
        out_shape=[jax.ShapeDtypeStruct((order, nlag, t, c), F32)] * 2,
        compiler_params=_cparams("parallel", "arbitrary", "arbitrary", "arbitrary"),
        name="hyena_filter_spectrum",
    )(table, gseq, gseq, skip.reshape(order, 1, c), hb0)


def _fwd_body(a_ref, u_ref, o_ref):
    o_ref[...] = jnp.dot(a_ref[...], u_ref[...], preferred_element_type=F32)


def _hyena_fwd(table, u, tm):
    n, t = table.shape
    l, ncol = u.shape
    tn = _tile(ncol, 1024)
    return pl.pallas_call(
        _fwd_body,
        grid=(n // tm, l // t, ncol // tn),
        in_specs=[pl.BlockSpec((tm, t), lambda i, jb, cc: (i, 0)),
                  pl.BlockSpec((t, tn), lambda i, jb, cc: (jb, cc))],
        out_specs=pl.BlockSpec((None, tm, tn), lambda i, jb, cc: (jb, i, cc)),
        out_shape=jax.ShapeDtypeStruct((l // t, n, ncol), F32),
        compiler_params=_cparams("parallel", "arbitrary", "arbitrary"),
        name="hyena_fwd_dft",
    )(table, u)


def _mix_body(u_ref, gr_ref, gi_ref, o_ref, *, nblk, half, chunk):
    i = pl.program_id(0)

    def rows(r0, size):
        re = pl.ds(r0, size)
        im = pl.ds(half + r0, size)
        for io in range(nblk):
            acc_r = None
            acc_i = None
            for jj in range(nblk):
                lag = io - jj + nblk - 1
                ur, ui = u_ref[jj, re, :], u_ref[jj, im, :]
                gr, gi = gr_ref[lag, pl.ds(r0, size), :], gi_ref[lag, pl.ds(r0, size), :]
                pr = ur * gr - ui * gi
                pi = ur * gi + ui * gr
                acc_r = pr if acc_r is None else acc_r + pr
                acc_i = pi if acc_i is None else acc_i + pi
            o_ref[io, re, :] = acc_r.astype(o_ref.dtype)
            o_ref[io, im, :] = acc_i.astype(o_ref.dtype)

    def body(rc, carry):
        rows(pl.multiple_of(rc * chunk, chunk), chunk)
        return carry

    lax.fori_loop(0, half // chunk, body, 0)

    @pl.when(i == 0)
    def _():
        for io in range(nblk):
            dc = None
            ny = None
            for jj in range(nblk):
                lag = io - jj + nblk - 1
                pr = u_ref[jj, 0:1, :] * gr_ref[lag, 0:1, :]
                pi = u_ref[jj, half:half + 1, :] * gi_ref[lag, 0:1, :]
                dc = pr if dc is None else dc + pr
                ny = pi if ny is None else ny + pi
            o_ref[io, 0:1, :] = dc.astype(o_ref.dtype)
            o_ref[io, half:half + 1, :] = ny.astype(o_ref.dtype)


def _hyena_mix(u, gr, gi, order, tm):
    nblk, n, ncol = u.shape
    c = gr.shape[3]
    tc = _tile(c, 256)
    cb = c // tc
    half = tm // 2
    chunk = min(64, half)
    gspec = pl.BlockSpec((None, 2 * nblk - 1, half, tc), lambda i, cc: (order, 0, i, cc % cb))
    return pl.pallas_call(
        functools.partial(_mix_body, nblk=nblk, half=half, chunk=chunk),
        grid=(n // tm, ncol // tc),
        in_specs=[pl.BlockSpec((nblk, tm, tc), lambda i, cc: (0, i, cc)), gspec, gspec],
        out_specs=pl.BlockSpec((nblk, tm, tc), lambda i, cc: (0, i, cc)),
        out_shape=jax.ShapeDtypeStruct((nblk, n, ncol), BF16),
        compiler_params=_cparams("parallel", "arbitrary"),
        name="hyena_block_mix",
    )(u, gr, gi)


def _inv_body(a_ref, y_ref, x_ref, o_ref, acc_ref):
    k_axis = 3
    kk = pl.program_id(k_axis)
    tn_dims = (((0,), (0,)), ((), ()))
    prod = lax.dot_general(a_ref[...], y_ref[...], tn_dims, preferred_element_type=F32)

    @pl.when(kk == 0)
    def _():
        acc_ref[...] = prod

    @pl.when(kk > 0)
    def _():
        acc_ref[...] += prod

    @pl.when(kk == pl.num_programs(k_axis) - 1)
    def _():
        o_ref[...] = (x_ref[...] * acc_ref[...]).astype(o_ref.dtype)


def _hyena_inv_gate(table, y, xg, c, *, token_major):
    n, t = table.shape
    nblk, _, ncol = y.shape
    l = nblk * t
    tmt = _tile(t, 1024)
    tb = t // tmt
    tk = _tile(n, 2048)
    tn = c if token_major else _tile(c, 1024)
    if token_major:
        out_spec = pl.BlockSpec((None, tmt, c), lambda io, ti, j, kk: (j, io * tb + ti, 0))
        out_shape = jax.ShapeDtypeStruct((ncol // c, l, c), BF16)
    else:
        out_spec = pl.BlockSpec((tmt, tn), lambda io, ti, j, kk: (io * tb + ti, j))
        out_shape = jax.ShapeDtypeStruct((l, ncol), BF16)
    return pl.pallas_call(
        _inv_body,
        grid=(nblk, tb, ncol // tn, n // tk),
        in_specs=[pl.BlockSpec((tk, tmt), lambda io, ti, j, kk: (kk, ti)),
                  pl.BlockSpec((None, tk, tn), lambda io, ti, j, kk: (io, kk, j)),
                  pl.BlockSpec((tmt, tn), lambda io, ti, j, kk: (io * tb + ti, j))],
        out_specs=out_spec,
        out_shape=out_shape,
        scratch_shapes=[pltpu.VMEM((tmt, tn), F32)],
        compiler_params=_cparams("parallel", "parallel", "parallel", "arbitrary"),
        name="hyena_inv_dft",
    )(table, y, xg)


def _hyena_group(p3, table, tm, emb2, filt_w, short_w, short_b, skip, deltas):
    order, c = skip.shape
    gseq, hb0 = _hyena_filters(emb2, *filt_w, deltas, order)
    gr, gi = _hyena_filter_spectrum(table, gseq, skip, hb0, tm)
    v, x1, x2 = _hyena_short_conv(p3, short_w, short_b, c)
    y1 = _hyena_mix(_hyena_fwd(table, v, tm), gr, gi, 0, tm)
    z1 = _hyena_inv_gate(table, y1, x1, c, token_major=False)
    y2 = _hyena_mix(_hyena_fwd(table, z1, tm), gr, gi, 1, tm)
    return _hyena_inv_gate(table, y2, x2, c, token_major=True)


def kernel(x, c, ctx, c_ctx, w_mod_down, w_mod_up, b_mod, w_in, w_out, hy_short_w, hy_short_b, hy_filt_w1, hy_filt_b1, hy_filt_w2, hy_filt_b2, hy_filt_w3, hy_filt_b3, hy_filt_w4, hy_filt_freq, hy_skip, na_rpb, cf_dw_w, cf_dw_b, cf_norm_g, cf_norm_b, w_ffn_gate, w_ffn_up, w_ffn_down, ln_mix_g, ln_mix_b, ln_ffn_g, ln_ffn_b):
    bsz, s, d = x.shape
    lc = ctx.shape[1]
    depth = w_in.shape[0]
    in_cols = w_in.shape[2]
    w_hy = hy_skip.shape[2]
    w_cf = cf_dw_w.shape[2]
    w_na = (in_cols - 3 * w_hy - 2 * w_cf) // 3
    heads = w_na // NA_HEAD_DIM
    off_na = 3 * w_hy
    off_kv = off_na + w_na
    off_conf = off_na + 3 * w_na
    alpha = (2 * depth) ** 0.25
    dh = NA_HEAD_DIM
    assert bsz + 1 <= 8 and off_conf % w_cf == 0 and off_na % dh == 0

    cos, sin = _rope_tables(s)
    t_lat = min(HYENA_BLOCK, s)
    t_ctx = min(HYENA_BLOCK, lc)
    assert s % t_lat == 0 and lc % t_ctx == 0
    tm_lat = min(1024, 2 * t_lat)
    tm_ctx = min(1024, 2 * t_ctx)
    table_lat = _dft_table(t_lat, tm_lat)
    table_ctx = _dft_table(t_ctx, tm_ctx)
    emb_lat = _hyena_embedding_pair(s)
    emb_ctx = _hyena_embedding_pair(lc)
    deltas = jnp.abs(jnp.linspace(HYENA_MIN_DECAY, HYENA_MAX_DECAY, w_hy, dtype=F32))
    deltas = deltas[None, :]

    cond8 = jnp.zeros((8, d), F32).at[:bsz].set(c).at[bsz].set(c_ctx)
    mods = _modulation_all(cond8, w_mod_down, w_mod_up, b_mod).reshape(depth, 8, N_MOD, d)

    def mod_lat(l, i):
        return mods[l, :bsz, i][:, None, :]

    def mod_ctx(l, i):
        return jnp.broadcast_to(mods[l, bsz, i][None, None, :], (bsz, 1, d))

    h = _modulate(x, mod_lat(0, 0), mod_lat(0, 1))
    hc = _modulate(ctx, mod_ctx(0, 0), mod_ctx(0, 1))

    for l in range(depth):
        last = l == depth - 1
        filt_w = (hy_filt_w1[l], hy_filt_b1[l], hy_filt_w2[l], hy_filt_b2[l], hy_filt_w3[l], hy_filt_b3[l],
                  hy_filt_w4[l], hy_filt_freq[l])
        bias_cls = _na_bias_pairs(na_rpb[l])

        hc2 = hc.reshape(bsz * lc, d)
        if last:
            kvc = _matmul(hc2, w_in, l, col0=off_kv, ncols=2 * w_na).reshape(bsz, lc, 2 * w_na)
            off_kc, off_vc = 0, heads
        else:
            pc = _matmul(hc2, w_in, l).reshape(bsz, lc, in_cols)
            kvc = pc
            off_kc, off_vc = off_kv // dh, (off_kv + w_na) // dh

        p = _matmul(h.reshape(bsz * s, d), w_in, l).reshape(bsz, s, in_cols)
        y_hy = _hyena_group(p, table_lat, tm_lat, emb_lat, filt_w, hy_short_w[l], hy_short_b[l], hy_skip[l], deltas)
        y_na = _neighbourhood_attention(p, kvc, bias_cls, cos, sin, off_q=off_na // dh, off_k=off_kv // dh,
                                        off_v=(off_kv + w_na) // dh, off_kc=off_kc, off_vc=off_vc, heads=heads)
        y_cf = _conformer(p, cf_dw_w[l], cf_dw_b[l], cf_norm_g[l], cf_norm_b[l], off=off_conf // w_cf)
        z = _matmul_concat3_residual(y_hy, y_na, y_cf, w_out, l, x, mod_lat(l, 2), alpha)
        x, h = _layer_norm(z, ln_mix_g[l], ln_mix_b[l], mod_lat(l, 3), mod_lat(l, 4))

        if not last:
            yc_hy = _hyena_group(pc, table_ctx, tm_ctx, emb_ctx, filt_w, hy_short_w[l], hy_short_b[l], hy_skip[l],
                                 deltas)
            yc_na = _context_attention(pc, off_q=off_na // dh, off_k=off_kv // dh, off_v=(off_kv + w_na) // dh,
                                       heads=heads)
            yc_cf = _conformer(pc, cf_dw_w[l], cf_dw_b[l], cf_norm_g[l], cf_norm_b[l], off=off_conf // w_cf)
            zc = _matmul_concat3_residual(yc_hy, yc_na, yc_cf, w_out, l, ctx, mod_ctx(l, 2), alpha)
            ctx, hc = _layer_norm(zc, ln_mix_g[l], ln_mix_b[l], mod_ctx(l, 3), mod_ctx(l, 4))
            ac = _matmul_swiglu_in(hc.reshape(bsz * lc, d), w_ffn_gate, w_ffn_up, l).reshape(bsz, lc, -1)
            zc = _matmul_down_residual(ac, w_ffn_down, l, ctx, mod_ctx(l, 5), alpha)
            ctx, hc = _layer_norm(zc, ln_ffn_g[l], ln_ffn_b[l], mod_ctx(l + 1, 0), mod_ctx(l + 1, 1))

        a = _matmul_swiglu_in(h.reshape(bsz * s, d), w_ffn_gate, w_ffn_up, l).reshape(bsz, s, -1)
        z = _matmul_down_residual(a, w_ffn_down, l, x, mod_lat(l, 5), alpha)
        if last:
            x, _ = _layer_norm(z, ln_ffn_g[l], ln_ffn_b[l])
        else:
            x, h = _layer_norm(z, ln_ffn_g[l], ln_ffn_b[l], mod_lat(l + 1, 0), mod_lat(l + 1, 1))
    return x
```

```python
import functools
import math

import numpy as np
import jax
import jax.numpy as jnp
from jax import lax
from jax.experimental import pallas as pl
from jax.experimental.pallas import tpu as pltpu

F32 = jnp.float32
BF16 = jnp.bfloat16

GRID_W = 64
NA_HEAD_DIM = 128
NA_KH = 8
NA_KW = 16
NA_QROWS = 4
ROPE_THETA = 10000.0
HYENA_EMB_DIM = 33
HYENA_BLOCK = 1024
HYENA_TARGET = 1e-2
HYENA_MIN_DECAY = math.log(HYENA_TARGET) / 1.5
HYENA_MAX_DECAY = math.log(HYENA_TARGET) / 0.3
N_MOD = 6
LN_EPS = 1e-5
MASK_VALUE = -1e30

V7X_VMEM_LIMIT_BYTES = 56 * 1024 * 1024
V7X_VMEM_LIMIT_DOWN_BYTES = 60 * 1024 * 1024
SUBLANES = 8
CONV_HALO_ROWS = 16


def _cparams(*sem, vmem=V7X_VMEM_LIMIT_BYTES):
    return pltpu.CompilerParams(dimension_semantics=sem, vmem_limit_bytes=vmem)


def _tile(dim, pref, align=128):
    t = (min(dim, pref) // align) * align
    while t >= align:
        if dim % t == 0:
            return t
        t -= align
    return dim


def _sigmoid(z):
    return jax.nn.sigmoid(z)


def _mod_body(cond_ref, wd_ref, wu_ref, b_ref, o_ref, t_ref):
    @pl.when(pl.program_id(1) == 0)
    def _():
        cnd = cond_ref[...]
        t_ref[...] = jnp.dot(cnd * _sigmoid(cnd), wd_ref[...], preferred_element_type=F32)

    o_ref[...] = jnp.dot(t_ref[...], wu_ref[...], preferred_element_type=F32) + b_ref[...]


def _modulation_all(cond8, w_down, w_up, b_mod):
    depth, d, rank = w_down.shape
    n_out = w_up.shape[2]
    tn = _tile(n_out, 2048)
    return pl.pallas_call(
        _mod_body,
        grid=(depth, n_out // tn),
        in_specs=[
            pl.BlockSpec((8, d), lambda l, j: (0, 0)),
            pl.BlockSpec((None, d, rank), lambda l, j: (l, 0, 0)),
            pl.BlockSpec((None, rank, tn), lambda l, j: (l, 0, j)),
            pl.BlockSpec((None, 1, tn), lambda l, j: (l, 0, j)),
        ],
        out_specs=pl.BlockSpec((None, 8, tn), lambda l, j: (l, 0, j)),
        out_shape=jax.ShapeDtypeStruct((depth, 8, n_out), F32),
        scratch_shapes=[pltpu.VMEM((8, rank), F32)],
        compiler_params=_cparams("arbitrary", "arbitrary"),
        name="modulation",
    )(cond8, w_down, w_up, b_mod.reshape(depth, 1, n_out))


def _modulate_body(x_ref, shift_ref, scale_ref, o_ref):
    o_ref[...] = (x_ref[...] * (1.0 + scale_ref[...]) + shift_ref[...]).astype(o_ref.dtype)


def _modulate(x3, shift, scale):
    b, s, d = x3.shape
    ts = _tile(s, 256)
    tok = pl.BlockSpec((None, ts, d), lambda bi, i: (bi, i, 0))
    vec = pl.BlockSpec((None, 1, d), lambda bi, i: (bi, 0, 0))
    return pl.pallas_call(
        _modulate_body,
        grid=(b, s // ts),
        in_specs=[tok, vec, vec],
        out_specs=tok,
        out_shape=jax.ShapeDtypeStruct((b, s, d), BF16),
        compiler_params=_cparams("parallel", "parallel"),
        name="modulate",
    )(x3, shift, scale)


def _ln_body(*refs, with_h):
    if with_h:
        z_ref, g_ref, b_ref, shift_ref, scale_ref, xo_ref, ho_ref = refs
    else:
        z_ref, g_ref, b_ref, xo_ref = refs
    z = z_ref[...]
    mu = jnp.mean(z, axis=-1, keepdims=True)
    zc = z - mu
    var = jnp.mean(zc * zc, axis=-1, keepdims=True)
    xn = zc * lax.rsqrt(var + LN_EPS) * g_ref[...] + b_ref[...]
    xo_ref[...] = xn
    if with_h:
        ho_ref[...] = (xn * (1.0 + scale_ref[...]) + shift_ref[...]).astype(ho_ref.dtype)


def _layer_norm(z3, g, b, shift=None, scale=None):
    bsz, s, d = z3.shape
    ts = _tile(s, 256)
    with_h = shift is not None
    tok = pl.BlockSpec((None, ts, d), lambda bi, i: (bi, i, 0))
    vec = pl.BlockSpec((None, 1, d), lambda bi, i: (bi, 0, 0))
    par = pl.BlockSpec((1, d), lambda bi, i: (0, 0))
    in_specs = [tok, par, par]
    args = [z3, g.reshape(1, d), b.reshape(1, d)]
    out_specs = [tok]
    out_shape = [jax.ShapeDtypeStruct((bsz, s, d), F32)]
    if with_h:
        in_specs += [vec, vec]
        args += [shift, scale]
        out_specs.append(tok)
        out_shape.append(jax.ShapeDtypeStruct((bsz, s, d), BF16))
    res = pl.pallas_call(
        functools.partial(_ln_body, with_h=with_h),
        grid=(bsz, s // ts),
        in_specs=in_specs,
        out_specs=out_specs,
        out_shape=out_shape,
        compiler_params=_cparams("parallel", "parallel"),
        name="ln_residual",
    )(*args)
    return (res[0], res[1]) if with_h else (res[0], None)


def _row_spec(tm, k, single_buffer):
    mode = dict(pipeline_mode=pl.Buffered(1)) if single_buffer else {}
    return pl.BlockSpec((tm, k), lambda i, j: (i, 0), **mode)


def _mm_tiles(m, ncols, tn_big=256):
    if m >= 2048:
        return _tile(m, 2048), _tile(ncols, tn_big)
    return m, _tile(ncols, 1024)


def _mm_body(a_ref, b_ref, o_ref):
    o_ref[...] = jnp.dot(a_ref[...], b_ref[...].astype(BF16), preferred_element_type=F32).astype(o_ref.dtype)


def _matmul(a, w, layer, *, col0=0, ncols=None, out_dtype=F32):
    m, k = a.shape
    ncols = w.shape[2] - col0 if ncols is None else ncols
    tm, tn = _mm_tiles(m, math.gcd(ncols, col0), tn_big=512)
    jb = col0 // tn
    return pl.pallas_call(
        _mm_body,
        grid=(m // tm, ncols // tn),
        in_specs=[
            _row_spec(tm, k, m > tm),
            pl.BlockSpec((None, k, tn), lambda i, j: (layer, 0, j + jb)),
        ],
        out_specs=pl.BlockSpec((tm, tn), lambda i, j: (i, j)),
        out_shape=jax.ShapeDtypeStruct((m, ncols), out_dtype),
        compiler_params=_cparams("parallel", "arbitrary"),
        name="matmul",
    )(a, w)


def _res_tiles(bsz, rows, tr_big):
    if bsz * rows <= 1024:
        return bsz, rows
    return 1, _tile(rows, tr_big)


def _res_specs(bt, tr, tn, single_buffer):
    mode = dict(pipeline_mode=pl.Buffered(1)) if single_buffer else {}

    def rows(width):
        return pl.BlockSpec((bt, tr, width), lambda b, i, j: (b, i, 0), **mode)

    tile = pl.BlockSpec((bt, tr, tn), lambda b, i, j: (b, i, j))
    gate = pl.BlockSpec((bt, 1, tn), lambda b, i, j: (b, 0, j))
    return rows, tile, gate


def _mm3_res_body(a1_ref, a2_ref, a3_ref, w_ref, x_ref, gate_ref, o_ref, *, alpha):
    bt, tr, c1 = a1_ref.shape
    c2 = a2_ref.shape[2]
    c3 = a3_ref.shape[2]
    y = jnp.dot(a1_ref[...].reshape(bt * tr, c1), w_ref[0:c1, :].astype(BF16), preferred_element_type=F32)
    y += jnp.dot(a2_ref[...].reshape(bt * tr, c2), w_ref[c1:c1 + c2, :].astype(BF16), preferred_element_type=F32)
    y += jnp.dot(a3_ref[...].reshape(bt * tr, c3), w_ref[c1 + c2:, :].astype(BF16), preferred_element_type=F32)
    o_ref[...] = alpha * x_ref[...] + gate_ref[...] * y.reshape(bt, tr, y.shape[1])


def _matmul_concat3_residual(a1, a2, a3, w, layer, x3, gate, alpha):
    bsz, r, _ = a1.shape
    k, n = w.shape[1], w.shape[2]
    assert a1.shape[2] + a2.shape[2] + a3.shape[2] == k
    bt, tr = _res_tiles(bsz, r, 2048)
    tn = _tile(n, 256 if bt == 1 else 1024)
    rows, tile, gspec = _res_specs(bt, tr, tn, bsz * r > bt * tr)
    return pl.pallas_call(
        functools.partial(_mm3_res_body, alpha=alpha),
        grid=(bsz // bt, r // tr, n // tn),
        in_specs=[rows(a1.shape[2]), rows(a2.shape[2]), rows(a3.shape[2]),
                  pl.BlockSpec((None, k, tn), lambda b, i, j: (layer, 0, j)), tile, gspec],
        out_specs=tile,
        out_shape=jax.ShapeDtypeStruct((bsz, r, n), F32),
        compiler_params=_cparams("parallel", "parallel", "arbitrary"),
        name="matmul_out",
    )(a1, a2, a3, w, x3, gate)


def _gu_body(a_ref, wg_ref, wu_ref, o_ref):
    a = a_ref[...]
    g = jnp.dot(a, wg_ref[...].astype(BF16), preferred_element_type=F32)
    u = jnp.dot(a, wu_ref[...].astype(BF16), preferred_element_type=F32)
    o_ref[...] = (g * _sigmoid(g) * u).astype(o_ref.dtype)


def _matmul_swiglu_in(a, wg, wu, layer):
    m, k = a.shape
    n = wg.shape[2]
    tm, tn = _mm_tiles(m, n)
    tn = min(tn, _tile(n, 512))
    wspec = pl.BlockSpec((None, k, tn), lambda i, j: (layer, 0, j))
    return pl.pallas_call(
        _gu_body,
        grid=(m // tm, n // tn),
        in_specs=[_row_spec(tm, k, m > tm), wspec, wspec],
        out_specs=pl.BlockSpec((tm, tn), lambda i, j: (i, j)),
        out_shape=jax.ShapeDtypeStruct((m, n), BF16),
        compiler_params=_cparams("parallel", "arbitrary"),
        name="matmul_swiglu_in",
    )(a, wg, wu)


def _mm_res_body(a_ref, w_ref, x_ref, gate_ref, o_ref, *, alpha):
    bt, tr, k = a_ref.shape
    y = jnp.dot(a_ref[...].reshape(bt * tr, k), w_ref[...].astype(BF16), preferred_element_type=F32)
    o_ref[...] = alpha * x_ref[...] + gate_ref[...] * y.reshape(bt, tr, y.shape[1])


def _matmul_down_residual(a, w, layer, x3, gate, alpha):
    bsz, r, k = a.shape
    n = w.shape[2]
    bt, tr = _res_tiles(bsz, r, 1024)
    tn = _tile(n, 256)
    rows, tile, gspec = _res_specs(bt, tr, tn, bsz * r > bt * tr)
    return pl.pallas_call(
        functools.partial(_mm_res_body, alpha=alpha),
        grid=(bsz // bt, r // tr, n // tn),
        in_specs=[rows(k), pl.BlockSpec((None, k, tn), lambda b, i, j: (layer, 0, j)), tile, gspec],
        out_specs=tile,
        out_shape=jax.ShapeDtypeStruct((bsz, r, n), F32),
        compiler_params=_cparams("arbitrary", "arbitrary", "arbitrary", vmem=V7X_VMEM_LIMIT_DOWN_BYTES),
        name="matmul_down",
    )(a, w, x3, gate)


def _rope_tables(s):
    t = jnp.arange(s)
    row = (t // GRID_W).astype(F32)[:, None]
    col = (t % GRID_W).astype(F32)[:, None]
    axis_dim = NA_HEAD_DIM // 2
    inv_freq = ROPE_THETA ** (-jnp.arange(0, axis_dim, 2, dtype=F32) / axis_dim)
    ang = jnp.concatenate([row * inv_freq, row * inv_freq, col * inv_freq, col * inv_freq], axis=-1)
    return jnp.cos(ang), jnp.sin(ang)


def _rotate_half_matrix():
    quarter = NA_HEAD_DIM // 4
    p = np.zeros((NA_HEAD_DIM, NA_HEAD_DIM), np.float32)
    for i in range(NA_HEAD_DIM):
        if (i % (2 * quarter)) < quarter:
            p[i + quarter, i] = -1.0
        else:
            p[i - quarter, i] = 1.0
    return jnp.asarray(p, BF16)


NA_SPAN = NA_QROWS + NA_KH - 1
NA_DROWS = 2 * NA_KH
NA_CLASS_BASES = (0, -(NA_KH // 2), -(NA_KH - 1))


def _bias_table_body(rpb_ref, onehot_ref, mask_ref, o_ref):
    o_ref[...] = jnp.dot(rpb_ref[...], onehot_ref[...], preferred_element_type=F32,
                         precision=lax.Precision.HIGHEST) + mask_ref[...]


def _na_bias_pairs(rpb):
    h, nr, nc = rpb.shape
    w = GRID_W
    cq = np.arange(w)[:, None]
    ck = np.arange(w)[None, :]
    d_col = np.clip(ck - cq, -(NA_KW - 1), NA_KW - 1) + (NA_KW - 1)
    col_start = np.clip(cq - NA_KW // 2, 0, w - NA_KW)
    col_ok = (ck >= col_start) & (ck < col_start + NA_KW)
    ncp = -(-nc // 8) * 8
    onehot = np.zeros((2 * ncp, w, 2, w), np.float32)
    for half in range(2):
        onehot[half * ncp + d_col, cq, half, ck] = 1.0
    maskadd = np.where(col_ok, 0.0, MASK_VALUE).astype(np.float32)
    maskadd = np.broadcast_to(maskadd[:, None, :], (w, 2, w)).reshape(1, 2 * w * w)
    rp = jnp.pad(rpb, ((0, 0), (1, NA_DROWS - nr), (0, ncp - nc)))
    rpb_pairs = jnp.concatenate([rp[:, :NA_DROWS], rp[:, 1:NA_DROWS + 1]], axis=-1).reshape(h * NA_DROWS, 2 * ncp)
    out = pl.pallas_call(
        _bias_table_body,
        out_shape=jax.ShapeDtypeStruct((h * NA_DROWS, 2 * w * w), F32),
        compiler_params=_cparams(),
        name="na_bias_table",
    )(rpb_pairs, jnp.asarray(onehot.reshape(2 * ncp, 2 * w * w)), jnp.asarray(maskadd))
    return out.reshape(h, NA_DROWS, w, 2 * w)


def _na_window_valid(cls, a, j):
    if j >= NA_SPAN:
        return False
    if cls == 0:
        return j < NA_KH
    if cls == 2:
        return j >= NA_SPAN - NA_KH
    return 0 <= NA_CLASS_BASES[1] + j - a + NA_KH // 2 < NA_KH


def _na_build_bias(pair_ref, bias_s):
    w = GRID_W
    left = lax.broadcasted_iota(jnp.int32, (w, 2 * w), 1) < w
    for cls, base in enumerate(NA_CLASS_BASES):
        for a in range(NA_QROWS):
            for pj in range(-(-NA_SPAN // 2)):
                j0 = 2 * pj
                v0, v1 = _na_window_valid(cls, a, j0), _na_window_valid(cls, a, j0 + 1)
                width = 2 * w if j0 + 1 < NA_SPAN else w
                if v0 or v1:
                    tile = pair_ref[base + j0 - a + NA_KH]
                    if not v1:
                        tile = jnp.where(left, tile, MASK_VALUE)
                    elif not v0:
                        tile = jnp.where(left, MASK_VALUE, tile)
                else:
                    tile = jnp.full((w, 2 * w), MASK_VALUE, F32)
                bias_s[cls, a * w:(a + 1) * w, j0 * w:j0 * w + width] = tile[:, :width]


def _na_body(q_ref, k_ref, v_ref, kc_ref, vc_ref, pair_ref, cos_ref, sin_ref, rot_ref, o_ref,
             qr_s, kr_s, qb_s, vb_s, bias_s, *, rows, scale):
    @pl.when(pl.program_id(1) == 0)
    def _():
        _na_build_bias(pair_ref, bias_s)

    cos = cos_ref[...]
    sin = sin_ref[...]

    def rope(z, zb):
        return z * cos + jnp.dot(zb, rot_ref[...], preferred_element_type=F32) * sin

    q = q_ref[...] * scale
    qb = q.astype(BF16)
    qb_s[...] = qb
    qr_s[...] = rope(q, qb).astype(BF16)
    k = k_ref[...]
    kr_s[...] = rope(k, k.astype(BF16)).astype(BF16)
    vb_s[...] = v_ref[...].astype(BF16)
    kc = kc_ref[...].astype(BF16)
    vc = vc_ref[...].astype(BF16)
    nt = (((1,), (1,)), ((), ()))
    groups = rows // NA_QROWS
    qn = NA_QROWS * GRID_W
    kn = NA_SPAN * GRID_W

    def body(g, carry):
        ks = jnp.clip(g * NA_QROWS + NA_CLASS_BASES[1], 0, rows - NA_SPAN)
        cls = jnp.where(g == 0, 0, jnp.where(g == groups - 1, 2, 1))
        q0 = pl.multiple_of(g * qn, qn)
        k0 = pl.multiple_of(ks * GRID_W, GRID_W)
        s_win = lax.dot_general(qr_s[pl.ds(q0, qn), :], kr_s[pl.ds(k0, kn), :], nt,
                                preferred_element_type=F32) + bias_s[cls]
        s_ctx = lax.dot_general(qb_s[pl.ds(q0, qn), :], kc, nt, preferred_element_type=F32)
        m = jnp.maximum(jnp.max(s_win, axis=-1, keepdims=True), jnp.max(s_ctx, axis=-1, keepdims=True))
        e_win = jnp.exp(s_win - m)
        e_ctx = jnp.exp(s_ctx - m)
        den = jnp.sum(e_win, axis=-1, keepdims=True) + jnp.sum(e_ctx, axis=-1, keepdims=True)
        o = jnp.dot(e_win.astype(BF16), vb_s[pl.ds(k0, kn), :], preferred_element_type=F32)
        o += jnp.dot(e_ctx.astype(BF16), vc, preferred_element_type=F32)
        o_ref[pl.ds(q0, qn), :] = (o / den).astype(o_ref.dtype)
        return carry

    lax.fori_loop(0, groups, body, 0, unroll=4)


def _neighbourhood_attention(p3, kvc3, bias_pairs, cos, sin, *, off_q, off_k, off_v, off_kc, off_vc, heads):
    bsz, s, _ = p3.shape
    lc = kvc3.shape[1]
    rows = s // GRID_W
    assert s % GRID_W == 0 and rows % NA_QROWS == 0 and rows // NA_QROWS >= 3
    dh = NA_HEAD_DIM

    def col(off):
        return pl.BlockSpec((None, s, dh), lambda h, b: (b, 0, off + h))

    def colc(off):
        return pl.BlockSpec((None, lc, dh), lambda h, b: (b, 0, off + h))

    tab = pl.BlockSpec((s, dh), lambda h, b: (0, 0))
    return pl.pallas_call(
        functools.partial(_na_body, rows=rows, scale=dh ** -0.5),
        grid=(heads, bsz),
        in_specs=[col(off_q), col(off_k), col(off_v), colc(off_kc), colc(off_vc),
                  pl.BlockSpec((None, NA_DROWS, GRID_W, 2 * GRID_W), lambda h, b: (h, 0, 0, 0)), tab, tab,
                  pl.BlockSpec((dh, dh), lambda h, b: (0, 0))],
        out_specs=pl.BlockSpec((None, s, dh), lambda h, b: (b, 0, h)),
        out_shape=jax.ShapeDtypeStruct((bsz, s, heads * dh), BF16),
        scratch_shapes=[pltpu.VMEM((s, dh), BF16)] * 4
        + [pltpu.VMEM((len(NA_CLASS_BASES), NA_QROWS * GRID_W, NA_SPAN * GRID_W), F32)],
        compiler_params=_cparams("arbitrary", "arbitrary"),
        name="neighbourhood_attention",
    )(p3, p3, p3, kvc3, kvc3, bias_pairs, cos, sin, _rotate_half_matrix())


def _ctx_attn_body(q_ref, k_ref, v_ref, o_ref, *, scale):
    nt = (((1,), (1,)), ((), ()))
    s = lax.dot_general(q_ref[...].astype(BF16), k_ref[...].astype(BF16), nt, preferred_element_type=F32) * scale
    m = jnp.max(s, axis=-1, keepdims=True)
    e = jnp.exp(s - m)
    den = jnp.sum(e, axis=-1, keepdims=True)
    o = jnp.dot(e.astype(BF16), v_ref[...].astype(BF16), preferred_element_type=F32)
    o_ref[...] = (o / den).astype(o_ref.dtype)


def _context_attention(pc3, *, off_q, off_k, off_v, heads):
    bsz, lc, _ = pc3.shape
    dh = NA_HEAD_DIM

    def col(off):
        return pl.BlockSpec((None, lc, dh), lambda b, h: (b, 0, off + h))

    return pl.pallas_call(
        functools.partial(_ctx_attn_body, scale=dh ** -0.5),
        grid=(bsz, heads),
        in_specs=[col(off_q), col(off_k), col(off_v)],
        out_specs=pl.BlockSpec((None, lc, dh), lambda b, h: (b, 0, h)),
        out_shape=jax.ShapeDtypeStruct((bsz, lc, heads * dh), BF16),
        compiler_params=_cparams("parallel", "parallel"),
        name="context_attention",
    )(pc3, pc3, pc3)


def _conformer_body(a_ref, g_ref, pa_ref, pg_ref, na_ref, ng_ref, w_ref, wb_ref, lg_ref, lb_ref, o_ref, u_s, sh_s,
                    *, ts, taps):
    i = pl.program_id(1)
    halo = CONV_HALO_ROWS
    a = a_ref[...]
    g = g_ref[...]
    u_s[halo:halo + ts, :] = a * _sigmoid(g)
    prev = pa_ref[...] * _sigmoid(pg_ref[...])
    u_s[0:halo, :] = jnp.where(i > 0, prev, 0.0)
    nxt = na_ref[...] * _sigmoid(ng_ref[...])
    u_s[halo + ts:, :] = jnp.where(i < pl.num_programs(1) - 1, nxt, 0.0)
    span = ts + 2 * halo - SUBLANES
    for r in range(SUBLANES):
        sh_s[r] = u_s[r:r + span, :]
    base = halo - taps // 2
    z = wb_ref[...]
    for t in range(taps):
        a8, r = divmod(base + t, SUBLANES)
        z += w_ref[t:t + 1, :] * sh_s[r, a8 * SUBLANES:a8 * SUBLANES + ts, :]
    mu = jnp.mean(z, axis=-1, keepdims=True)
    zc = z - mu
    var = jnp.mean(zc * zc, axis=-1, keepdims=True)
    zn = zc * lax.rsqrt(var + LN_EPS) * lg_ref[...] + lb_ref[...]
    o_ref[...] = (zn * _sigmoid(zn)).astype(o_ref.dtype)


def _conformer(p3, dw_w, dw_b, ln_g, ln_b, *, off):
    bsz, l, _ = p3.shape
    taps, c = dw_w.shape
    halo = CONV_HALO_ROWS
    assert taps // 2 < halo and l % halo == 0
    ts = _tile(l, 256)
    hb = ts // halo
    nh = l // halo

    def cur(o):
        return pl.BlockSpec((None, ts, c), lambda b, i: (b, i, o))

    def prev(o):
        return pl.BlockSpec((None, halo, c), lambda b, i: (b, jnp.maximum(i * hb - 1, 0), o))

    def nxt(o):
        return pl.BlockSpec((None, halo, c), lambda b, i: (b, jnp.minimum((i + 1) * hb, nh - 1), o))

    par = pl.BlockSpec((1, c), lambda b, i: (0, 0))
    return pl.pallas_call(
        functools.partial(_conformer_body, ts=ts, taps=taps),
        grid=(bsz, l // ts),
        in_specs=[cur(off), cur(off + 1), prev(off), prev(off + 1), nxt(off), nxt(off + 1),
                  pl.BlockSpec((taps, c), lambda b, i: (0, 0)), par, par, par],
        out_specs=pl.BlockSpec((None, ts, c), lambda b, i: (b, i, 0)),
        out_shape=jax.ShapeDtypeStruct((bsz, l, c), BF16),
        scratch_shapes=[pltpu.VMEM((ts + 2 * halo, c), F32),
                        pltpu.VMEM((SUBLANES, ts + 2 * halo - SUBLANES, c), F32)],
        compiler_params=_cparams("parallel", "arbitrary"),
        name="conformer",
    )(p3, p3, p3, p3, p3, p3, dw_w, dw_b.reshape(1, c), ln_g.reshape(1, c), ln_b.reshape(1, c))


def _hy_short_body(cur_ref, prev_ref, next_ref, w_ref, b_ref, v_ref, x1_ref, x2_ref, u_s, *, ts, c):
    i = pl.program_id(1)
    halo = CONV_HALO_ROWS
    u_s[halo:halo + ts, :] = cur_ref[...]
    u_s[0:halo, :] = jnp.where(i > 0, prev_ref[...], 0.0)
    u_s[halo + ts:, :] = jnp.where(i < pl.num_programs(1) - 1, next_ref[...], 0.0)
    z = (b_ref[...] + w_ref[0:1, :] * u_s[halo - 1:halo - 1 + ts, :] + w_ref[1:2, :] * u_s[halo:halo + ts, :]
         + w_ref[2:3, :] * u_s[halo + 1:halo + 1 + ts, :])
    v_ref[...] = z[:, 0:c].astype(v_ref.dtype)
    x1_ref[...] = z[:, c:2 * c]
    x2_ref[...] = z[:, 2 * c:3 * c]


def _hyena_short_conv(p3, short_w, short_b, c):
    bsz, l, _ = p3.shape
    halo = CONV_HALO_ROWS
    ts = _tile(l, 256)
    hb = ts // halo
    nh = l // halo
    w3 = 3 * c
    out = pl.BlockSpec((ts, c), lambda b, i: (i, b))
    return pl.pallas_call(
        functools.partial(_hy_short_body, ts=ts, c=c),
        grid=(bsz, l // ts),
        in_specs=[
            pl.BlockSpec((None, ts, w3), lambda b, i: (b, i, 0)),
            pl.BlockSpec((None, halo, w3), lambda b, i: (b, jnp.maximum(i * hb - 1, 0), 0)),
            pl.BlockSpec((None, halo, w3), lambda b, i: (b, jnp.minimum((i + 1) * hb, nh - 1), 0)),
            pl.BlockSpec((3, w3), lambda b, i: (0, 0)),
            pl.BlockSpec((1, w3), lambda b, i: (0, 0)),
        ],
        out_specs=[out, out, out],
        out_shape=[jax.ShapeDtypeStruct((l, bsz * c), BF16), jax.ShapeDtypeStruct((l, bsz * c), F32),
                   jax.ShapeDtypeStruct((l, bsz * c), F32)],
        scratch_shapes=[pltpu.VMEM((ts + 2 * halo, w3), F32)],
        compiler_params=_cparams("parallel", "arbitrary"),
        name="hyena_short_conv",
    )(p3, p3, p3, short_w, short_b.reshape(1, w3))


def _hyena_embedding(l):
    t = jnp.linspace(0.0, 1.0, l, dtype=F32)[:, None]
    bands = (HYENA_EMB_DIM - 1) // 2
    omega = (2.0 * math.pi / l) * jnp.arange(l, dtype=F32)[:, None]
    f = jnp.linspace(1e-4, bands - 1, bands, dtype=F32)[None, :]
    return jnp.concatenate([t, jnp.cos(f * omega), -jnp.sin(f * omega)], axis=-1)


def _hyena_embedding_pair(l):
    emb = _hyena_embedding(l)
    return jnp.stack([jnp.concatenate([emb[0:1], jnp.flip(emb[1:], axis=0)], axis=0), emb])


def _hy_filter_body(emb_ref, embt_ref, w1_ref, b1_ref, w2_ref, b2_ref, w3_ref, b3_ref, freq_ref, w4_ref, delta_ref,
                    g_ref, hb0_ref, hid_s):
    dirn = pl.program_id(2)
    first = jnp.logical_and(jnp.logical_and(pl.program_id(0) == 0, pl.program_id(1) == 0), dirn == 0)
    tn_dims = (((0,), (0,)), ((), ()))

    @pl.when(first)
    def _():
        freq = freq_ref[...]
        for d in range(2):
            hid = lax.dot_general(w1_ref[...], embt_ref[d], tn_dims, preferred_element_type=F32)
            hid = jnp.sin(freq * (hid + b1_ref[...]))
            hid = jnp.sin(freq * (lax.dot_general(w2_ref[...], hid, tn_dims, preferred_element_type=F32) + b2_ref[...]))
            hid_s[d] = jnp.sin(freq * (lax.dot_general(w3_ref[...], hid, tn_dims, preferred_element_type=F32)
                                       + b3_ref[...]))

    t = emb_ref[dirn][:, 0:1]
    h = lax.dot_general(hid_s[dirn], w4_ref[...], tn_dims, preferred_element_type=F32)
    h = h * jnp.exp(-t * delta_ref[...])
    h = h / (jnp.sum(jnp.abs(h), axis=0, keepdims=True) + 1e-6)
    is_bwd = dirn == 0

    @pl.when(is_bwd)
    def _():
        hb0_ref[...] = h[0:1, :].astype(hb0_ref.dtype)

    row = lax.broadcasted_iota(jnp.int32, (h.shape[0], 1), 0)
    g_ref[...] = jnp.where(jnp.logical_and(row == 0, is_bwd), 0.0, h).astype(g_ref.dtype)


def _hyena_filters(emb2, w1, b1, w2, b2, w3, b3, w4, freq, deltas, order):
    _, l, e = emb2.shape
    hf = w1.shape[1]
    c = w4.shape[1] // (2 * order)
    tn = _tile(c, 512)
    cb = c // tn

    def full(shape):
        return pl.BlockSpec(shape, lambda o, ct, dn: (0,) * len(shape))

    return pl.pallas_call(
        _hy_filter_body,
        grid=(order, cb, 2),
        in_specs=[full((2, l, e)), full((2, e, l)), full((e, hf)), full((hf, 1)), full((hf, hf)), full((hf, 1)),
                  full((hf, hf)), full((hf, 1)), full((hf, 1)),
                  pl.BlockSpec((hf, tn), lambda o, ct, dn: (0, (2 * o + 1 - dn) * cb + ct)),
                  pl.BlockSpec((1, tn), lambda o, ct, dn: (0, ct))],
        out_specs=[pl.BlockSpec((None, l, tn), lambda o, ct, dn: (o, dn, ct)),
                   pl.BlockSpec((None, 1, tn), lambda o, ct, dn: (o, 0, ct))],
        out_shape=[jax.ShapeDtypeStruct((order, 2 * l, c), BF16), jax.ShapeDtypeStruct((order, 1, c), BF16)],
        scratch_shapes=[pltpu.VMEM((2, hf, l), F32)],
        compiler_params=_cparams("arbitrary", "arbitrary", "arbitrary"),
        name="hyena_filters",
    )(emb2, jnp.swapaxes(emb2, 1, 2), w1, b1.reshape(hf, 1), w2, b2.reshape(hf, 1), w3, b3.reshape(hf, 1),
      freq.reshape(hf, 1), w4, deltas)


def _dft_table(l, tm):
    n = 2 * l
    half = tm // 2
    r = np.arange(n)
    f_np = (r // tm) * half + (r % half)
    is_im_np = (r % tm) >= half
    sq = 1 << (int(math.log2(l)) // 2)
    hi = l // sq
    f = jnp.asarray(f_np, jnp.int32)[:, None]
    w0 = 2.0 * math.pi / n
    a_hi = ((f * (jnp.arange(hi, dtype=jnp.int32)[None, :] * sq)) % n).astype(F32) * w0
    a_lo = ((f * jnp.arange(sq, dtype=jnp.int32)[None, :]) % n).astype(F32) * w0
    ch, sh = jnp.cos(a_hi)[:, :, None], jnp.sin(a_hi)[:, :, None]
    cl, sl = jnp.cos(a_lo)[:, None, :], jnp.sin(a_lo)[:, None, :]
    cosv = (ch * cl - sh * sl).reshape(n, l)
    sinv = (sh * cl + ch * sl).reshape(n, l)
    nyq = jnp.asarray(np.where(np.arange(l) % 2 == 0, 1.0, -1.0), F32)[None, :]
    is_im = jnp.asarray(is_im_np)[:, None]
    table = jnp.where(is_im, jnp.where(f == 0, nyq, -sinv), cosv)
    return table.astype(BF16)


def _spec_body(a_ref, blk_ref, skip_ref, hb0_ref, gr_ref, gi_ref, prev_s, c_s, *, n, lag0):
    i = pl.program_id(0)
    b = pl.program_id(3)
    half = a_ref.shape[0] // 2
    blk = blk_ref[...]
    cur = jnp.dot(a_ref[...], blk, preferred_element_type=F32)

    @pl.when(b > 0)
    def _():
        sp_ = prev_s[...]
        c_prev = c_s[...]
        row = lax.broadcasted_iota(jnp.int32, (half, 1), 0)
        is_dc = jnp.logical_and(row == 0, i == 0)
        sigma = (1 - 2 * (row % 2)).astype(F32)
        gr = cur[:half] + sigma * (sp_[:half] - c_prev)
        gi = jnp.where(is_dc, cur[half:] + sp_[half:] - c_prev, cur[half:] + sigma * sp_[half:])
        add = jnp.where(b - 1 == lag0, skip_ref[...] + hb0_ref[...].astype(F32), 0.0)
        gr = gr + add
        gi = jnp.where(is_dc, gi + add, gi)
        w = jnp.where(is_dc, 1.0 / n, 2.0 / n)
        gr_ref[...] = gr * w
        gi_ref[...] = gi * w

    prev_s[...] = cur
    c_s[...] = blk[0:1, :].astype(F32)


def _hyena_filter_spectrum(table, gseq, skip, hb0, tm):
    n, t = table.shape
    order, l2, c = gseq.shape
    nblk = l2 // t
    nlag = nblk - 1
    tn = _tile(c, 512)
    half = tm // 2
    out = pl.BlockSpec((None, None, half, tn), lambda i, o, j, b: (o, jnp.maximum(b - 1, 0), i, j))
    vec = pl.BlockSpec((None, 1, tn), lambda i, o, j, b: (o, 0, j))
    return pl.pallas_call(
        functools.partial(_spec_body, n=n, lag0=nblk // 2 - 1),
        grid=(n // tm, order, c // tn, nblk),
        in_specs=[
            pl.BlockSpec((tm, t), lambda i, o, j, b: (i, 0)),
            pl.BlockSpec((None, t, tn), lambda i, o, j, b: (o, b, j)),
            vec, vec,
        ],
        out_specs=[out, out],
        out_shape=[jax.ShapeDtypeStruct((order, nlag, t, c), F32)] * 2,
        scratch_shapes=[pltpu.VMEM((tm, tn), F32), pltpu.VMEM((1, tn), F32)],
        compiler_params=_cparams("arbitrary", "arbitrary", "arbitrary", "arbitrary"),
        name="hyena_filter_spectrum",
    )(table, gseq, skip.reshape(order, 1, c), hb0)


def _fwd_body(a_ref, u_ref, o_ref):
    o_ref[...] = jnp.dot(a_ref[...], u_ref[...], preferred_element_type=F32)


def _hyena_fwd(table, u, tm):
    n, t = table.shape
    l, ncol = u.shape
    tn = _tile(ncol, 1024)
    return pl.pallas_call(
        _fwd_body,
        grid=(n // tm, l // t, ncol // tn),
        in_specs=[pl.BlockSpec((tm, t), lambda i, jb, cc: (i, 0)),
                  pl.BlockSpec((t, tn), lambda i, jb, cc: (jb, cc))],
        out_specs=pl.BlockSpec((None, tm, tn), lambda i, jb, cc: (jb, i, cc)),
        out_shape=jax.ShapeDtypeStruct((l // t, n, ncol), F32),
        compiler_params=_cparams("parallel", "arbitrary", "arbitrary"),
        name="hyena_fwd_dft",
    )(table, u)


def _mix_body(u_ref, gr_ref, gi_ref, o_ref, *, nblk, half, chunk):
    i = pl.program_id(0)

    def rows(r0, size):
        re = pl.ds(r0, size)
        im = pl.ds(half + r0, size)
        for io in range(nblk):
            acc_r = None
            acc_i = None
            for jj in range(nblk):
                lag = io - jj + nblk - 1
                ur, ui = u_ref[jj, re, :], u_ref[jj, im, :]
                gr, gi = gr_ref[lag, pl.ds(r0, size), :], gi_ref[lag, pl.ds(r0, size), :]
                pr = ur * gr - ui * gi
                pi = ur * gi + ui * gr
                acc_r = pr if acc_r is None else acc_r + pr
                acc_i = pi if acc_i is None else acc_i + pi
            o_ref[io, re, :] = acc_r.astype(o_ref.dtype)
            o_ref[io, im, :] = acc_i.astype(o_ref.dtype)

    def body(rc, carry):
        rows(pl.multiple_of(rc * chunk, chunk), chunk)
        return carry

    lax.fori_loop(0, half // chunk, body, 0)

    @pl.when(i == 0)
    def _():
        for io in range(nblk):
            dc = None
            ny = None
            for jj in range(nblk):
                lag = io - jj + nblk - 1
                pr = u_ref[jj, 0:1, :] * gr_ref[lag, 0:1, :]
                pi = u_ref[jj, half:half + 1, :] * gi_ref[lag, 0:1, :]
                dc = pr if dc is None else dc + pr
                ny = pi if ny is None else ny + pi
            o_ref[io, 0:1, :] = dc.astype(o_ref.dtype)
            o_ref[io, half:half + 1, :] = ny.astype(o_ref.dtype)


def _hyena_mix(u, gr, gi, order, tm):
    nblk, n, ncol = u.shape
    c = gr.shape[3]
    tc = _tile(c, 256)
    cb = c // tc
    half = tm // 2
    chunk = min(64, half)
    gspec = pl.BlockSpec((None, 2 * nblk - 1, half, tc), lambda i, cc: (order, 0, i, cc % cb))
    return pl.pallas_call(
        functools.partial(_mix_body, nblk=nblk, half=half, chunk=chunk),
        grid=(n // tm, ncol // tc),
        in_specs=[pl.BlockSpec((nblk, tm, tc), lambda i, cc: (0, i, cc)), gspec, gspec],
        out_specs=pl.BlockSpec((nblk, tm, tc), lambda i, cc: (0, i, cc)),
        out_shape=jax.ShapeDtypeStruct((nblk, n, ncol), BF16),
        compiler_params=_cparams("parallel", "arbitrary"),
        name="hyena_block_mix",
    )(u, gr, gi)


def _inv_body(a_ref, y_ref, x_ref, o_ref, acc_ref):
    k_axis = 3
    kk = pl.program_id(k_axis)
    tn_dims = (((0,), (0,)), ((), ()))
    prod = lax.dot_general(a_ref[...], y_ref[...], tn_dims, preferred_element_type=F32)

    @pl.when(kk == 0)
    def _():
        acc_ref[...] = prod

    @pl.when(kk > 0)
    def _():
        acc_ref[...] += prod

    @pl.when(kk == pl.num_programs(k_axis) - 1)
    def _():
        o_ref[...] = (x_ref[...] * acc_ref[...]).astype(o_ref.dtype)


def _hyena_inv_gate(table, y, xg, c, *, token_major):
    n, t = table.shape
    nblk, _, ncol = y.shape
    l = nblk * t
    tmt = _tile(t, 1024)
    tb = t // tmt
    tk = _tile(n, 2048)
    tn = c if token_major else _tile(c, 1024)
    if token_major:
        out_spec = pl.BlockSpec((None, tmt, c), lambda io, ti, j, kk: (j, io * tb + ti, 0))
        out_shape = jax.ShapeDtypeStruct((ncol // c, l, c), BF16)
    else:
        out_spec = pl.BlockSpec((tmt, tn), lambda io, ti, j, kk: (io * tb + ti, j))
        out_shape = jax.ShapeDtypeStruct((l, ncol), BF16)
    return pl.pallas_call(
        _inv_body,
        grid=(nblk, tb, ncol // tn, n // tk),
        in_specs=[pl.BlockSpec((tk, tmt), lambda io, ti, j, kk: (kk, ti)),
                  pl.BlockSpec((None, tk, tn), lambda io, ti, j, kk: (io, kk, j)),
                  pl.BlockSpec((tmt, tn), lambda io, ti, j, kk: (io * tb + ti, j))],
        out_specs=out_spec,
        out_shape=out_shape,
        scratch_shapes=[pltpu.VMEM((tmt, tn), F32)],
        compiler_params=_cparams("parallel", "parallel", "parallel", "arbitrary"),
        name="hyena_inv_dft",
    )(table, y, xg)


def _hyena_group(p3, table, tm, emb2, filt_w, short_w, short_b, skip, deltas):
    order, c = skip.shape
    gseq, hb0 = _hyena_filters(emb2, *filt_w, deltas, order)
    gr, gi = _hyena_filter_spectrum(table, gseq, skip, hb0, tm)
    v, x1, x2 = _hyena_short_conv(p3, short_w, short_b, c)
    y1 = _hyena_mix(_hyena_fwd(table, v, tm), gr, gi, 0, tm)
    z1 = _hyena_inv_gate(table, y1, x1, c, token_major=False)
    y2 = _hyena_mix(_hyena_fwd(table, z1, tm), gr, gi, 1, tm)
    return _hyena_inv_gate(table, y2, x2, c, token_major=True)


def kernel(x, c, ctx, c_ctx, w_mod_down, w_mod_up, b_mod, w_in, w_out, hy_short_w, hy_short_b, hy_filt_w1, hy_filt_b1, hy_filt_w2, hy_filt_b2, hy_filt_w3, hy_filt_b3, hy_filt_w4, hy_filt_freq, hy_skip, na_rpb, cf_dw_w, cf_dw_b, cf_norm_g, cf_norm_b, w_ffn_gate, w_ffn_up, w_ffn_down, ln_mix_g, ln_mix_b, ln_ffn_g, ln_ffn_b):
    bsz, s, d = x.shape
    lc = ctx.shape[1]
    depth = w_in.shape[0]
    in_cols = w_in.shape[2]
    w_hy = hy_skip.shape[2]
    w_cf = cf_dw_w.shape[2]
    w_na = (in_cols - 3 * w_hy - 2 * w_cf) // 3
    heads = w_na // NA_HEAD_DIM
    off_na = 3 * w_hy
    off_kv = off_na + w_na
    off_conf = off_na + 3 * w_na
    alpha = (2 * depth) ** 0.25
    dh = NA_HEAD_DIM
    assert bsz + 1 <= 8 and off_conf % w_cf == 0 and off_na % dh == 0

    cos, sin = _rope_tables(s)
    t_lat = min(HYENA_BLOCK, s)
    t_ctx = min(HYENA_BLOCK, lc)
    assert s % t_lat == 0 and lc % t_ctx == 0
    tm_lat = min(1024, 2 * t_lat)
    tm_ctx = min(1024, 2 * t_ctx)
    table_lat = _dft_table(t_lat, tm_lat)
    table_ctx = _dft_table(t_ctx, tm_ctx)
    emb_lat = _hyena_embedding_pair(s)
    emb_ctx = _hyena_embedding_pair(lc)
    deltas = jnp.abs(jnp.linspace(HYENA_MIN_DECAY, HYENA_MAX_DECAY, w_hy, dtype=F32))
    deltas = deltas[None, :]

    cond8 = jnp.zeros((8, d), F32).at[:bsz].set(c).at[bsz].set(c_ctx)
    mods = _modulation_all(cond8, w_mod_down, w_mod_up, b_mod).reshape(depth, 8, N_MOD, d)

    def mod_lat(l, i):
        return mods[l, :bsz, i][:, None, :]

    def mod_ctx(l, i):
        return jnp.broadcast_to(mods[l, bsz, i][None, None, :], (bsz, 1, d))

    h = _modulate(x, mod_lat(0, 0), mod_lat(0, 1))
    hc = _modulate(ctx, mod_ctx(0, 0), mod_ctx(0, 1))

    for l in range(depth):
        last = l == depth - 1
        filt_w = (hy_filt_w1[l], hy_filt_b1[l], hy_filt_w2[l], hy_filt_b2[l], hy_filt_w3[l], hy_filt_b3[l],
                  hy_filt_w4[l], hy_filt_freq[l])
        bias_cls = _na_bias_pairs(na_rpb[l])

        hc2 = hc.reshape(bsz * lc, d)
        if last:
            kvc = _matmul(hc2, w_in, l, col0=off_kv, ncols=2 * w_na).reshape(bsz, lc, 2 * w_na)
            off_kc, off_vc = 0, heads
        else:
            pc = _matmul(hc2, w_in, l).reshape(bsz, lc, in_cols)
            kvc = pc
            off_kc, off_vc = off_kv // dh, (off_kv + w_na) // dh

        p = _matmul(h.reshape(bsz * s, d), w_in, l).reshape(bsz, s, in_cols)
        y_hy = _hyena_group(p, table_lat, tm_lat, emb_lat, filt_w, hy_short_w[l], hy_short_b[l], hy_skip[l], deltas)
        y_na = _neighbourhood_attention(p, kvc, bias_cls, cos, sin, off_q=off_na // dh, off_k=off_kv // dh,
                                        off_v=(off_kv + w_na) // dh, off_kc=off_kc, off_vc=off_vc, heads=heads)
        y_cf = _conformer(p, cf_dw_w[l], cf_dw_b[l], cf_norm_g[l], cf_norm_b[l], off=off_conf // w_cf)
        z = _matmul_concat3_residual(y_hy, y_na, y_cf, w_out, l, x, mod_lat(l, 2), alpha)
        x, h = _layer_norm(z, ln_mix_g[l], ln_mix_b[l], mod_lat(l, 3), mod_lat(l, 4))

        if not last:
            yc_hy = _hyena_group(pc, table_ctx, tm_ctx, emb_ctx, filt_w, hy_short_w[l], hy_short_b[l], hy_skip[l],
                                 deltas)
            yc_na = _context_attention(pc, off_q=off_na // dh, off_k=off_kv // dh, off_v=(off_kv + w_na) // dh,
                                       heads=heads)
            yc_cf = _conformer(pc, cf_dw_w[l], cf_dw_b[l], cf_norm_g[l], cf_norm_b[l], off=off_conf // w_cf)
            zc = _matmul_concat3_residual(yc_hy, yc_na, yc_cf, w_out, l, ctx, mod_ctx(l, 2), alpha)
            ctx, hc = _layer_norm(zc, ln_mix_g[l], ln_mix_b[l], mod_ctx(l, 3), mod_ctx(l, 4))
            ac = _matmul_swiglu_in(hc.reshape(bsz * lc, d), w_ffn_gate, w_ffn_up, l).reshape(bsz, lc, -1)
            zc = _matmul_down_residual(ac, w_ffn_down, l, ctx, mod_ctx(l, 5), alpha)
            ctx, hc = _layer_norm(zc, ln_ffn_g[l], ln_ffn_b[l], mod_ctx(l + 1, 0), mod_ctx(l + 1, 1))

        a = _matmul_swiglu_in(h.reshape(bsz * s, d), w_ffn_gate, w_ffn_up, l).reshape(bsz, s, -1)
        z = _matmul_down_residual(a, w_ffn_down, l, x, mod_lat(l, 5), alpha)
        if last:
            x, _ = _layer_norm(z, ln_ffn_g[l], ln_ffn_b[l])
        else:
            x, h = _layer_norm(z, ln_ffn_g[l], ln_ffn_b[l], mod_lat(l + 1, 0), mod_lat(l + 1, 1))
    return x
```

```python
import functools
import math

import numpy as np
import jax
import jax.numpy as jnp
from jax import lax
from jax.experimental import pallas as pl
from jax.experimental.pallas import tpu as pltpu

F32 = jnp.float32
BF16 = jnp.bfloat16

GRID_W = 64
NA_HEAD_DIM = 128
NA_KH = 8
NA_KW = 16
NA_QROWS = 4
ROPE_THETA = 10000.0
HYENA_EMB_DIM = 33
HYENA_BLOCK = 1024
HYENA_TARGET = 1e-2
HYENA_MIN_DECAY = math.log(HYENA_TARGET) / 1.5
HYENA_MAX_DECAY = math.log(HYENA_TARGET) / 0.3
N_MOD = 6
LN_EPS = 1e-5
MASK_VALUE = -1e30

V7X_VMEM_LIMIT_BYTES = 56 * 1024 * 1024
V7X_VMEM_LIMIT_DOWN_BYTES = 60 * 1024 * 1024
SUBLANES = 8
CONV_HALO_ROWS = 16


def _cparams(*sem, vmem=V7X_VMEM_LIMIT_BYTES):
    return pltpu.CompilerParams(dimension_semantics=sem, vmem_limit_bytes=vmem)


def _tile(dim, pref, align=128):
    t = (min(dim, pref) // align) * align
    while t >= align:
        if dim % t == 0:
            return t
        t -= align
    return dim


def _sigmoid(z):
    return jax.nn.sigmoid(z)


def _mod_body(cond_ref, wd_ref, wu_ref, b_ref, o_ref, t_ref):
    @pl.when(pl.program_id(1) == 0)
    def _():
        cnd = cond_ref[...]
        t_ref[...] = jnp.dot(cnd * _sigmoid(cnd), wd_ref[...], preferred_element_type=F32)

    o_ref[...] = jnp.dot(t_ref[...], wu_ref[...], preferred_element_type=F32) + b_ref[...]


def _modulation_all(cond8, w_down, w_up, b_mod):
    depth, d, rank = w_down.shape
    n_out = w_up.shape[2]
    tn = _tile(n_out, 2048)
    return pl.pallas_call(
        _mod_body,
        grid=(depth, n_out // tn),
        in_specs=[
            pl.BlockSpec((8, d), lambda l, j: (0, 0)),
            pl.BlockSpec((None, d, rank), lambda l, j: (l, 0, 0)),
            pl.BlockSpec((None, rank, tn), lambda l, j: (l, 0, j)),
            pl.BlockSpec((None, 1, tn), lambda l, j: (l, 0, j)),
        ],
        out_specs=pl.BlockSpec((None, 8, tn), lambda l, j: (l, 0, j)),
        out_shape=jax.ShapeDtypeStruct((depth, 8, n_out), F32),
        scratch_shapes=[pltpu.VMEM((8, rank), F32)],
        compiler_params=_cparams("arbitrary", "arbitrary"),
        name="modulation",
    )(cond8, w_down, w_up, b_mod.reshape(depth, 1, n_out))


def _modulate_body(x_ref, shift_ref, scale_ref, o_ref):
    o_ref[...] = (x_ref[...] * (1.0 + scale_ref[...]) + shift_ref[...]).astype(o_ref.dtype)


def _modulate(x3, shift, scale):
    b, s, d = x3.shape
    ts = _tile(s, 256)
    tok = pl.BlockSpec((None, ts, d), lambda bi, i: (bi, i, 0))
    vec = pl.BlockSpec((None, 1, d), lambda bi, i: (bi, 0, 0))
    return pl.pallas_call(
        _modulate_body,
        grid=(b, s // ts),
        in_specs=[tok, vec, vec],
        out_specs=tok,
        out_shape=jax.ShapeDtypeStruct((b, s, d), BF16),
        compiler_params=_cparams("parallel", "parallel"),
        name="modulate",
    )(x3, shift, scale)


def _ln_body(*refs, alpha, with_h):
    if with_h:
        x_ref, y_ref, gate_ref, g_ref, b_ref, shift_ref, scale_ref, xo_ref, ho_ref = refs
    else:
        x_ref, y_ref, gate_ref, g_ref, b_ref, xo_ref = refs
    z = alpha * x_ref[...] + gate_ref[...] * y_ref[...].astype(F32)
    mu = jnp.mean(z, axis=-1, keepdims=True)
    zc = z - mu
    var = jnp.mean(zc * zc, axis=-1, keepdims=True)
    xn = zc * lax.rsqrt(var + LN_EPS) * g_ref[...] + b_ref[...]
    xo_ref[...] = xn
    if with_h:
        ho_ref[...] = (xn * (1.0 + scale_ref[...]) + shift_ref[...]).astype(ho_ref.dtype)


def _ln_residual(x3, y3, gate, g, b, alpha, shift=None, scale=None):
    bsz, s, d = x3.shape
    ts = _tile(s, 256)
    with_h = shift is not None
    tok = pl.BlockSpec((None, ts, d), lambda bi, i: (bi, i, 0))
    vec = pl.BlockSpec((None, 1, d), lambda bi, i: (bi, 0, 0))
    par = pl.BlockSpec((1, d), lambda bi, i: (0, 0))
    in_specs = [tok, tok, vec, par, par]
    args = [x3, y3, gate, g.reshape(1, d), b.reshape(1, d)]
    out_specs = [tok]
    out_shape = [jax.ShapeDtypeStruct((bsz, s, d), F32)]
    if with_h:
        in_specs += [vec, vec]
        args += [shift, scale]
        out_specs.append(tok)
        out_shape.append(jax.ShapeDtypeStruct((bsz, s, d), BF16))
    res = pl.pallas_call(
        functools.partial(_ln_body, alpha=alpha, with_h=with_h),
        grid=(bsz, s // ts),
        in_specs=in_specs,
        out_specs=out_specs,
        out_shape=out_shape,
        compiler_params=_cparams("parallel", "parallel"),
        name="ln_residual",
    )(*args)
    return (res[0], res[1]) if with_h else (res[0], None)


def _row_spec(tm, k, single_buffer):
    mode = dict(pipeline_mode=pl.Buffered(1)) if single_buffer else {}
    return pl.BlockSpec((tm, k), lambda i, j: (i, 0), **mode)


def _mm_tiles(m, ncols, tn_big=256):
    if m >= 2048:
        return _tile(m, 2048), _tile(ncols, tn_big)
    return m, _tile(ncols, 1024)


def _mm_body(a_ref, b_ref, o_ref):
    o_ref[...] = jnp.dot(a_ref[...], b_ref[...].astype(BF16), preferred_element_type=F32).astype(o_ref.dtype)


def _matmul(a, w, layer, *, col0=0, ncols=None, out_dtype=F32):
    m, k = a.shape
    ncols = w.shape[2] - col0 if ncols is None else ncols
    tm, tn = _mm_tiles(m, math.gcd(ncols, col0), tn_big=512)
    jb = col0 // tn
    return pl.pallas_call(
        _mm_body,
        grid=(m // tm, ncols // tn),
        in_specs=[
            _row_spec(tm, k, m > tm),
            pl.BlockSpec((None, k, tn), lambda i, j: (layer, 0, j + jb)),
        ],
        out_specs=pl.BlockSpec((tm, tn), lambda i, j: (i, j)),
        out_shape=jax.ShapeDtypeStruct((m, ncols), out_dtype),
        compiler_params=_cparams("parallel", "arbitrary"),
        name="matmul",
    )(a, w)


def _mm3_body(a1_ref, a2_ref, a3_ref, w_ref, o_ref):
    c1 = a1_ref.shape[1]
    c2 = a2_ref.shape[1]
    acc = jnp.dot(a1_ref[...], w_ref[0:c1, :].astype(BF16), preferred_element_type=F32)
    acc += jnp.dot(a2_ref[...], w_ref[c1:c1 + c2, :].astype(BF16), preferred_element_type=F32)
    acc += jnp.dot(a3_ref[...], w_ref[c1 + c2:, :].astype(BF16), preferred_element_type=F32)
    o_ref[...] = acc.astype(o_ref.dtype)


def _matmul_concat3(a1, a2, a3, w, layer):
    m = a1.shape[0]
    k, n = w.shape[1], w.shape[2]
    assert a1.shape[1] + a2.shape[1] + a3.shape[1] == k
    tm, tn = _mm_tiles(m, n)
    return pl.pallas_call(
        _mm3_body,
        grid=(m // tm, n // tn),
        in_specs=[
            _row_spec(tm, a1.shape[1], m > tm),
            _row_spec(tm, a2.shape[1], m > tm),
            _row_spec(tm, a3.shape[1], m > tm),
            pl.BlockSpec((None, k, tn), lambda i, j: (layer, 0, j)),
        ],
        out_specs=pl.BlockSpec((tm, tn), lambda i, j: (i, j)),
        out_shape=jax.ShapeDtypeStruct((m, n), BF16),
        compiler_params=_cparams("parallel", "arbitrary"),
        name="matmul_out",
    )(a1, a2, a3, w)


def _gu_body(a_ref, wg_ref, wu_ref, o_ref):
    a = a_ref[...]
    g = jnp.dot(a, wg_ref[...].astype(BF16), preferred_element_type=F32)
    u = jnp.dot(a, wu_ref[...].astype(BF16), preferred_element_type=F32)
    o_ref[...] = (g * _sigmoid(g) * u).astype(o_ref.dtype)


def _matmul_swiglu_in(a, wg, wu, layer):
    m, k = a.shape
    n = wg.shape[2]
    tm, tn = _mm_tiles(m, n)
    tn = min(tn, _tile(n, 512))
    wspec = pl.BlockSpec((None, k, tn), lambda i, j: (layer, 0, j))
    return pl.pallas_call(
        _gu_body,
        grid=(m // tm, n // tn),
        in_specs=[_row_spec(tm, k, m > tm), wspec, wspec],
        out_specs=pl.BlockSpec((tm, tn), lambda i, j: (i, j)),
        out_shape=jax.ShapeDtypeStruct((m, n), BF16),
        compiler_params=_cparams("parallel", "arbitrary"),
        name="matmul_swiglu_in",
    )(a, wg, wu)


def _matmul_down(a, w, layer):
    m, k = a.shape
    n = w.shape[2]
    tm = _tile(m, 1024)
    tn = _tile(n, 256)
    return pl.pallas_call(
        _mm_body,
        grid=(m // tm, n // tn),
        in_specs=[_row_spec(tm, k, m > tm), pl.BlockSpec((None, k, tn), lambda i, j: (layer, 0, j))],
        out_specs=pl.BlockSpec((tm, tn), lambda i, j: (i, j)),
        out_shape=jax.ShapeDtypeStruct((m, n), BF16),
        compiler_params=_cparams("parallel", "arbitrary", vmem=V7X_VMEM_LIMIT_DOWN_BYTES),
        name="matmul_down",
    )(a, w)


def _rope_tables(s):
    t = jnp.arange(s)
    row = (t // GRID_W).astype(F32)[:, None]
    col = (t % GRID_W).astype(F32)[:, None]
    axis_dim = NA_HEAD_DIM // 2
    inv_freq = ROPE_THETA ** (-jnp.arange(0, axis_dim, 2, dtype=F32) / axis_dim)
    ang = jnp.concatenate([row * inv_freq, row * inv_freq, col * inv_freq, col * inv_freq], axis=-1)
    return jnp.cos(ang), jnp.sin(ang)


def _rotate_half_matrix():
    quarter = NA_HEAD_DIM // 4
    p = np.zeros((NA_HEAD_DIM, NA_HEAD_DIM), np.float32)
    for i in range(NA_HEAD_DIM):
        if (i % (2 * quarter)) < quarter:
            p[i + quarter, i] = -1.0
        else:
            p[i - quarter, i] = 1.0
    return jnp.asarray(p, BF16)


NA_SPAN = NA_QROWS + NA_KH - 1
NA_DROWS = 2 * NA_KH
NA_CLASS_BASES = (0, -(NA_KH // 2), -(NA_KH - 1))


def _bias_table_body(rpb_ref, onehot_ref, mask_ref, o_ref):
    o_ref[...] = jnp.dot(rpb_ref[...], onehot_ref[...], preferred_element_type=F32,
                         precision=lax.Precision.HIGHEST) + mask_ref[...]


def _na_bias_pairs(rpb):
    h, nr, nc = rpb.shape
    w = GRID_W
    cq = np.arange(w)[:, None]
    ck = np.arange(w)[None, :]
    d_col = np.clip(ck - cq, -(NA_KW - 1), NA_KW - 1) + (NA_KW - 1)
    col_start = np.clip(cq - NA_KW // 2, 0, w - NA_KW)
    col_ok = (ck >= col_start) & (ck < col_start + NA_KW)
    ncp = -(-nc // 8) * 8
    onehot = np.zeros((2 * ncp, w, 2, w), np.float32)
    for half in range(2):
        onehot[half * ncp + d_col, cq, half, ck] = 1.0
    maskadd = np.where(col_ok, 0.0, MASK_VALUE).astype(np.float32)
    maskadd = np.broadcast_to(maskadd[:, None, :], (w, 2, w)).reshape(1, 2 * w * w)
    rp = jnp.pad(rpb, ((0, 0), (1, NA_DROWS - nr), (0, ncp - nc)))
    rpb_pairs = jnp.concatenate([rp[:, :NA_DROWS], rp[:, 1:NA_DROWS + 1]], axis=-1).reshape(h * NA_DROWS, 2 * ncp)
    out = pl.pallas_call(
        _bias_table_body,
        out_shape=jax.ShapeDtypeStruct((h * NA_DROWS, 2 * w * w), F32),
        compiler_params=_cparams(),
        name="na_bias_table",
    )(rpb_pairs, jnp.asarray(onehot.reshape(2 * ncp, 2 * w * w)), jnp.asarray(maskadd))
    return out.reshape(h, NA_DROWS, w, 2 * w)


def _na_window_valid(cls, a, j):
    if j >= NA_SPAN:
        return False
    if cls == 0:
        return j < NA_KH
    if cls == 2:
        return j >= NA_SPAN - NA_KH
    return 0 <= NA_CLASS_BASES[1] + j - a + NA_KH // 2 < NA_KH


def _na_build_bias(pair_ref, bias_s):
    w = GRID_W
    left = lax.broadcasted_iota(jnp.int32, (w, 2 * w), 1) < w
    for cls, base in enumerate(NA_CLASS_BASES):
        for a in range(NA_QROWS):
            for pj in range(-(-NA_SPAN // 2)):
                j0 = 2 * pj
                v0, v1 = _na_window_valid(cls, a, j0), _na_window_valid(cls, a, j0 + 1)
                width = 2 * w if j0 + 1 < NA_SPAN else w
                if v0 or v1:
                    tile = pair_ref[base + j0 - a + NA_KH]
                    if not v1:
                        tile = jnp.where(left, tile, MASK_VALUE)
                    elif not v0:
                        tile = jnp.where(left, MASK_VALUE, tile)
                else:
                    tile = jnp.full((w, 2 * w), MASK_VALUE, F32)
                bias_s[cls, a * w:(a + 1) * w, j0 * w:j0 * w + width] = tile[:, :width]


def _na_body(q_ref, k_ref, v_ref, kc_ref, vc_ref, pair_ref, cos_ref, sin_ref, rot_ref, o_ref,
             qr_s, kr_s, qb_s, vb_s, bias_s, *, rows, scale):
    @pl.when(pl.program_id(1) == 0)
    def _():
        _na_build_bias(pair_ref, bias_s)

    cos = cos_ref[...]
    sin = sin_ref[...]

    def rope(z, zb):
        return z * cos + jnp.dot(zb, rot_ref[...], preferred_element_type=F32) * sin

    q = q_ref[...] * scale
    qb = q.astype(BF16)
    qb_s[...] = qb
    qr_s[...] = rope(q, qb).astype(BF16)
    k = k_ref[...]
    kr_s[...] = rope(k, k.astype(BF16)).astype(BF16)
    vb_s[...] = v_ref[...].astype(BF16)
    kc = kc_ref[...].astype(BF16)
    vc = vc_ref[...].astype(BF16)
    nt = (((1,), (1,)), ((), ()))
    groups = rows // NA_QROWS
    qn = NA_QROWS * GRID_W
    kn = NA_SPAN * GRID_W

    def body(g, carry):
        ks = jnp.clip(g * NA_QROWS + NA_CLASS_BASES[1], 0, rows - NA_SPAN)
        cls = jnp.where(g == 0, 0, jnp.where(g == groups - 1, 2, 1))
        q0 = pl.multiple_of(g * qn, qn)
        k0 = pl.multiple_of(ks * GRID_W, GRID_W)
        s_win = lax.dot_general(qr_s[pl.ds(q0, qn), :], kr_s[pl.ds(k0, kn), :], nt,
                                preferred_element_type=F32) + bias_s[cls]
        s_ctx = lax.dot_general(qb_s[pl.ds(q0, qn), :], kc, nt, preferred_element_type=F32)
        m = jnp.maximum(jnp.max(s_win, axis=-1, keepdims=True), jnp.max(s_ctx, axis=-1, keepdims=True))
        e_win = jnp.exp(s_win - m)
        e_ctx = jnp.exp(s_ctx - m)
        den = jnp.sum(e_win, axis=-1, keepdims=True) + jnp.sum(e_ctx, axis=-1, keepdims=True)
        o = jnp.dot(e_win.astype(BF16), vb_s[pl.ds(k0, kn), :], preferred_element_type=F32)
        o += jnp.dot(e_ctx.astype(BF16), vc, preferred_element_type=F32)
        o_ref[pl.ds(q0, qn), :] = (o / den).astype(o_ref.dtype)
        return carry

    lax.fori_loop(0, groups, body, 0, unroll=4)


def _neighbourhood_attention(p3, kvc3, bias_pairs, cos, sin, *, off_q, off_k, off_v, off_kc, off_vc, heads):
    bsz, s, _ = p3.shape
    lc = kvc3.shape[1]
    rows = s // GRID_W
    assert s % GRID_W == 0 and rows % NA_QROWS == 0 and rows // NA_QROWS >= 3
    dh = NA_HEAD_DIM

    def col(off):
        return pl.BlockSpec((None, s, dh), lambda h, b: (b, 0, off + h))

    def colc(off):
        return pl.BlockSpec((None, lc, dh), lambda h, b: (b, 0, off + h))

    tab = pl.BlockSpec((s, dh), lambda h, b: (0, 0))
    return pl.pallas_call(
        functools.partial(_na_body, rows=rows, scale=dh ** -0.5),
        grid=(heads, bsz),
        in_specs=[col(off_q), col(off_k), col(off_v), colc(off_kc), colc(off_vc),
                  pl.BlockSpec((None, NA_DROWS, GRID_W, 2 * GRID_W), lambda h, b: (h, 0, 0, 0)), tab, tab,
                  pl.BlockSpec((dh, dh), lambda h, b: (0, 0))],
        out_specs=pl.BlockSpec((None, s, dh), lambda h, b: (b, 0, h)),
        out_shape=jax.ShapeDtypeStruct((bsz, s, heads * dh), BF16),
        scratch_shapes=[pltpu.VMEM((s, dh), BF16)] * 4
        + [pltpu.VMEM((len(NA_CLASS_BASES), NA_QROWS * GRID_W, NA_SPAN * GRID_W), F32)],
        compiler_params=_cparams("arbitrary", "arbitrary"),
        name="neighbourhood_attention",
    )(p3, p3, p3, kvc3, kvc3, bias_pairs, cos, sin, _rotate_half_matrix())


def _ctx_attn_body(q_ref, k_ref, v_ref, o_ref, *, scale):
    nt = (((1,), (1,)), ((), ()))
    s = lax.dot_general(q_ref[...].astype(BF16), k_ref[...].astype(BF16), nt, preferred_element_type=F32) * scale
    m = jnp.max(s, axis=-1, keepdims=True)
    e = jnp.exp(s - m)
    den = jnp.sum(e, axis=-1, keepdims=True)
    o = jnp.dot(e.astype(BF16), v_ref[...].astype(BF16), preferred_element_type=F32)
    o_ref[...] = (o / den).astype(o_ref.dtype)


def _context_attention(pc3, *, off_q, off_k, off_v, heads):
    bsz, lc, _ = pc3.shape
    dh = NA_HEAD_DIM

    def col(off):
        return pl.BlockSpec((None, lc, dh), lambda b, h: (b, 0, off + h))

    return pl.pallas_call(
        functools.partial(_ctx_attn_body, scale=dh ** -0.5),
        grid=(bsz, heads),
        in_specs=[col(off_q), col(off_k), col(off_v)],
        out_specs=pl.BlockSpec((None, lc, dh), lambda b, h: (b, 0, h)),
        out_shape=jax.ShapeDtypeStruct((bsz, lc, heads * dh), BF16),
        compiler_params=_cparams("parallel", "parallel"),
        name="context_attention",
    )(pc3, pc3, pc3)


def _conformer_body(a_ref, g_ref, pa_ref, pg_ref, na_ref, ng_ref, w_ref, wb_ref, lg_ref, lb_ref, o_ref, u_s, sh_s,
                    *, ts, taps):
    i = pl.program_id(1)
    halo = CONV_HALO_ROWS
    a = a_ref[...]
    g = g_ref[...]
    u_s[halo:halo + ts, :] = a * _sigmoid(g)
    prev = pa_ref[...] * _sigmoid(pg_ref[...])
    u_s[0:halo, :] = jnp.where(i > 0, prev, 0.0)
    nxt = na_ref[...] * _sigmoid(ng_ref[...])
    u_s[halo + ts:, :] = jnp.where(i < pl.num_programs(1) - 1, nxt, 0.0)
    span = ts + 2 * halo - SUBLANES
    for r in range(SUBLANES):
        sh_s[r] = u_s[r:r + span, :]
    base = halo - taps // 2
    z = wb_ref[...]
    for t in range(taps):
        a8, r = divmod(base + t, SUBLANES)
        z += w_ref[t:t + 1, :] * sh_s[r, a8 * SUBLANES:a8 * SUBLANES + ts, :]
    mu = jnp.mean(z, axis=-1, keepdims=True)
    zc = z - mu
    var = jnp.mean(zc * zc, axis=-1, keepdims=True)
    zn = zc * lax.rsqrt(var + LN_EPS) * lg_ref[...] + lb_ref[...]
    o_ref[...] = (zn * _sigmoid(zn)).astype(o_ref.dtype)


def _conformer(p3, dw_w, dw_b, ln_g, ln_b, *, off):
    bsz, l, _ = p3.shape
    taps, c = dw_w.shape
    halo = CONV_HALO_ROWS
    assert taps // 2 < halo and l % halo == 0
    ts = _tile(l, 256)
    hb = ts // halo
    nh = l // halo

    def cur(o):
        return pl.BlockSpec((None, ts, c), lambda b, i: (b, i, o))

    def prev(o):
        return pl.BlockSpec((None, halo, c), lambda b, i: (b, jnp.maximum(i * hb - 1, 0), o))

    def nxt(o):
        return pl.BlockSpec((None, halo, c), lambda b, i: (b, jnp.minimum((i + 1) * hb, nh - 1), o))

    par = pl.BlockSpec((1, c), lambda b, i: (0, 0))
    return pl.pallas_call(
        functools.partial(_conformer_body, ts=ts, taps=taps),
        grid=(bsz, l // ts),
        in_specs=[cur(off), cur(off + 1), prev(off), prev(off + 1), nxt(off), nxt(off + 1),
                  pl.BlockSpec((taps, c), lambda b, i: (0, 0)), par, par, par],
        out_specs=pl.BlockSpec((None, ts, c), lambda b, i: (b, i, 0)),
        out_shape=jax.ShapeDtypeStruct((bsz, l, c), BF16),
        scratch_shapes=[pltpu.VMEM((ts + 2 * halo, c), F32),
                        pltpu.VMEM((SUBLANES, ts + 2 * halo - SUBLANES, c), F32)],
        compiler_params=_cparams("parallel", "arbitrary"),
        name="conformer",
    )(p3, p3, p3, p3, p3, p3, dw_w, dw_b.reshape(1, c), ln_g.reshape(1, c), ln_b.reshape(1, c))


def _hy_short_body(cur_ref, prev_ref, next_ref, w_ref, b_ref, v_ref, x1_ref, x2_ref, u_s, *, ts, c):
    i = pl.program_id(1)
    halo = CONV_HALO_ROWS
    u_s[halo:halo + ts, :] = cur_ref[...]
    u_s[0:halo, :] = jnp.where(i > 0, prev_ref[...], 0.0)
    u_s[halo + ts:, :] = jnp.where(i < pl.num_programs(1) - 1, next_ref[...], 0.0)
    z = (b_ref[...] + w_ref[0:1, :] * u_s[halo - 1:halo - 1 + ts, :] + w_ref[1:2, :] * u_s[halo:halo + ts, :]
         + w_ref[2:3, :] * u_s[halo + 1:halo + 1 + ts, :])
    v_ref[...] = z[:, 0:c].astype(v_ref.dtype)
    x1_ref[...] = z[:, c:2 * c]
    x2_ref[...] = z[:, 2 * c:3 * c]


def _hyena_short_conv(p3, short_w, short_b, c):
    bsz, l, _ = p3.shape
    halo = CONV_HALO_ROWS
    ts = _tile(l, 256)
    hb = ts // halo
    nh = l // halo
    w3 = 3 * c
    out = pl.BlockSpec((ts, c), lambda b, i: (i, b))
    return pl.pallas_call(
        functools.partial(_hy_short_body, ts=ts, c=c),
        grid=(bsz, l // ts),
        in_specs=[
            pl.BlockSpec((None, ts, w3), lambda b, i: (b, i, 0)),
            pl.BlockSpec((None, halo, w3), lambda b, i: (b, jnp.maximum(i * hb - 1, 0), 0)),
            pl.BlockSpec((None, halo, w3), lambda b, i: (b, jnp.minimum((i + 1) * hb, nh - 1), 0)),
            pl.BlockSpec((3, w3), lambda b, i: (0, 0)),
            pl.BlockSpec((1, w3), lambda b, i: (0, 0)),
        ],
        out_specs=[out, out, out],
        out_shape=[jax.ShapeDtypeStruct((l, bsz * c), BF16), jax.ShapeDtypeStruct((l, bsz * c), F32),
                   jax.ShapeDtypeStruct((l, bsz * c), F32)],
        scratch_shapes=[pltpu.VMEM((ts + 2 * halo, w3), F32)],
        compiler_params=_cparams("parallel", "arbitrary"),
        name="hyena_short_conv",
    )(p3, p3, p3, short_w, short_b.reshape(1, w3))


def _hyena_embedding(l):
    t = jnp.linspace(0.0, 1.0, l, dtype=F32)[:, None]
    bands = (HYENA_EMB_DIM - 1) // 2
    omega = (2.0 * math.pi / l) * jnp.arange(l, dtype=F32)[:, None]
    f = jnp.linspace(1e-4, bands - 1, bands, dtype=F32)[None, :]
    return jnp.concatenate([t, jnp.cos(f * omega), -jnp.sin(f * omega)], axis=-1)


def _hyena_embedding_pair(l):
    emb = _hyena_embedding(l)
    return jnp.stack([jnp.concatenate([emb[0:1], jnp.flip(emb[1:], axis=0)], axis=0), emb])


def _hy_filter_body(emb_ref, embt_ref, w1_ref, b1_ref, w2_ref, b2_ref, w3_ref, b3_ref, freq_ref, w4_ref, delta_ref,
                    g_ref, hb0_ref, hid_s):
    dirn = pl.program_id(2)
    first = jnp.logical_and(jnp.logical_and(pl.program_id(0) == 0, pl.program_id(1) == 0), dirn == 0)
    tn_dims = (((0,), (0,)), ((), ()))

    @pl.when(first)
    def _():
        freq = freq_ref[...]
        for d in range(2):
            hid = lax.dot_general(w1_ref[...], embt_ref[d], tn_dims, preferred_element_type=F32)
            hid = jnp.sin(freq * (hid + b1_ref[...]))
            hid = jnp.sin(freq * (lax.dot_general(w2_ref[...], hid, tn_dims, preferred_element_type=F32) + b2_ref[...]))
            hid_s[d] = jnp.sin(freq * (lax.dot_general(w3_ref[...], hid, tn_dims, preferred_element_type=F32)
                                       + b3_ref[...]))

    t = emb_ref[dirn][:, 0:1]
    h = lax.dot_general(hid_s[dirn], w4_ref[...], tn_dims, preferred_element_type=F32)
    h = h * jnp.exp(-t * delta_ref[...])
    h = h / (jnp.sum(jnp.abs(h), axis=0, keepdims=True) + 1e-6)
    is_bwd = dirn == 0

    @pl.when(is_bwd)
    def _():
        hb0_ref[...] = h[0:1, :].astype(hb0_ref.dtype)

    row = lax.broadcasted_iota(jnp.int32, (h.shape[0], 1), 0)
    g_ref[...] = jnp.where(jnp.logical_and(row == 0, is_bwd), 0.0, h).astype(g_ref.dtype)


def _hyena_filters(emb2, w1, b1, w2, b2, w3, b3, w4, freq, deltas, order):
    _, l, e = emb2.shape
    hf = w1.shape[1]
    c = w4.shape[1] // (2 * order)
    tn = _tile(c, 512)
    cb = c // tn

    def full(shape):
        return pl.BlockSpec(shape, lambda o, ct, dn: (0,) * len(shape))

    return pl.pallas_call(
        _hy_filter_body,
        grid=(order, cb, 2),
        in_specs=[full((2, l, e)), full((2, e, l)), full((e, hf)), full((hf, 1)), full((hf, hf)), full((hf, 1)),
                  full((hf, hf)), full((hf, 1)), full((hf, 1)),
                  pl.BlockSpec((hf, tn), lambda o, ct, dn: (0, (2 * o + 1 - dn) * cb + ct)),
                  pl.BlockSpec((1, tn), lambda o, ct, dn: (0, ct))],
        out_specs=[pl.BlockSpec((None, l, tn), lambda o, ct, dn: (o, dn, ct)),
                   pl.BlockSpec((None, 1, tn), lambda o, ct, dn: (o, 0, ct))],
        out_shape=[jax.ShapeDtypeStruct((order, 2 * l, c), BF16), jax.ShapeDtypeStruct((order, 1, c), BF16)],
        scratch_shapes=[pltpu.VMEM((2, hf, l), F32)],
        compiler_params=_cparams("arbitrary", "arbitrary", "arbitrary"),
        name="hyena_filters",
    )(emb2, jnp.swapaxes(emb2, 1, 2), w1, b1.reshape(hf, 1), w2, b2.reshape(hf, 1), w3, b3.reshape(hf, 1),
      freq.reshape(hf, 1), w4, deltas)


def _dft_table(l, tm):
    n = 2 * l
    half = tm // 2
    r = np.arange(n)
    f_np = (r // tm) * half + (r % half)
    is_im_np = (r % tm) >= half
    sq = 1 << (int(math.log2(l)) // 2)
    hi = l // sq
    f = jnp.asarray(f_np, jnp.int32)[:, None]
    w0 = 2.0 * math.pi / n
    a_hi = ((f * (jnp.arange(hi, dtype=jnp.int32)[None, :] * sq)) % n).astype(F32) * w0
    a_lo = ((f * jnp.arange(sq, dtype=jnp.int32)[None, :]) % n).astype(F32) * w0
    ch, sh = jnp.cos(a_hi)[:, :, None], jnp.sin(a_hi)[:, :, None]
    cl, sl = jnp.cos(a_lo)[:, None, :], jnp.sin(a_lo)[:, None, :]
    cosv = (ch * cl - sh * sl).reshape(n, l)
    sinv = (sh * cl + ch * sl).reshape(n, l)
    nyq = jnp.asarray(np.where(np.arange(l) % 2 == 0, 1.0, -1.0), F32)[None, :]
    is_im = jnp.asarray(is_im_np)[:, None]
    table = jnp.where(is_im, jnp.where(f == 0, nyq, -sinv), cosv)
    return table.astype(BF16)


def _spec_body(a_ref, blk_ref, skip_ref, hb0_ref, gr_ref, gi_ref, prev_s, c_s, *, n, lag0):
    i = pl.program_id(0)
    b = pl.program_id(3)
    half = a_ref.shape[0] // 2
    blk = blk_ref[...]
    cur = jnp.dot(a_ref[...], blk, preferred_element_type=F32)

    @pl.when(b > 0)
    def _():
        sp_ = prev_s[...]
        c_prev = c_s[...]
        row = lax.broadcasted_iota(jnp.int32, (half, 1), 0)
        is_dc = jnp.logical_and(row == 0, i == 0)
        sigma = (1 - 2 * (row % 2)).astype(F32)
        gr = cur[:half] + sigma * (sp_[:half] - c_prev)
        gi = jnp.where(is_dc, cur[half:] + sp_[half:] - c_prev, cur[half:] + sigma * sp_[half:])
        add = jnp.where(b - 1 == lag0, skip_ref[...] + hb0_ref[...].astype(F32), 0.0)
        gr = gr + add
        gi = jnp.where(is_dc, gi + add, gi)
        w = jnp.where(is_dc, 1.0 / n, 2.0 / n)
        gr_ref[...] = gr * w
        gi_ref[...] = gi * w

    prev_s[...] = cur
    c_s[...] = blk[0:1, :].astype(F32)


def _hyena_filter_spectrum(table, gseq, skip, hb0, tm):
    n, t = table.shape
    order, l2, c = gseq.shape
    nblk = l2 // t
    nlag = nblk - 1
    tn = _tile(c, 1024)
    half = tm // 2
    out = pl.BlockSpec((None, None, half, tn), lambda i, o, j, b: (o, jnp.maximum(b - 1, 0), i, j))
    vec = pl.BlockSpec((None, 1, tn), lambda i, o, j, b: (o, 0, j))
    return pl.pallas_call(
        functools.partial(_spec_body, n=n, lag0=nblk // 2 - 1),
        grid=(n // tm, order, c // tn, nblk),
        in_specs=[
            pl.BlockSpec((tm, t), lambda i, o, j, b: (i, 0)),
            pl.BlockSpec((None, t, tn), lambda i, o, j, b: (o, b, j)),
            vec, vec,
        ],
        out_specs=[out, out],
        out_shape=[jax.ShapeDtypeStruct((order, nlag, t, c), F32)] * 2,
        scratch_shapes=[pltpu.VMEM((tm, tn), F32), pltpu.VMEM((1, tn), F32)],
        compiler_params=_cparams("arbitrary", "arbitrary", "arbitrary", "arbitrary"),
        name="hyena_filter_spectrum",
    )(table, gseq, skip.reshape(order, 1, c), hb0)


def _fwd_body(a_ref, u_ref, o_ref):
    o_ref[...] = jnp.dot(a_ref[...], u_ref[...], preferred_element_type=F32)


def _hyena_fwd(table, u, tm):
    n, t = table.shape
    l, ncol = u.shape
    tn = _tile(ncol, 1024)
    return pl.pallas_call(
        _fwd_body,
        grid=(n // tm, l // t, ncol // tn),
        in_specs=[pl.BlockSpec((tm, t), lambda i, jb, cc: (i, 0)),
                  pl.BlockSpec((t, tn), lambda i, jb, cc: (jb, cc))],
        out_specs=pl.BlockSpec((None, tm, tn), lambda i, jb, cc: (jb, i, cc)),
        out_shape=jax.ShapeDtypeStruct((l // t, n, ncol), F32),
        compiler_params=_cparams("parallel", "arbitrary", "arbitrary"),
        name="hyena_fwd_dft",
    )(table, u)


def _mix_body(u_ref, gr_ref, gi_ref, o_ref, *, nblk, half, chunk):
    i = pl.program_id(0)

    def rows(r0, size):
        re = pl.ds(r0, size)
        im = pl.ds(half + r0, size)
        for io in range(nblk):
            acc_r = None
            acc_i = None
            for jj in range(nblk):
                lag = io - jj + nblk - 1
                ur, ui = u_ref[jj, re, :], u_ref[jj, im, :]
                gr, gi = gr_ref[lag, pl.ds(r0, size), :], gi_ref[lag, pl.ds(r0, size), :]
                pr = ur * gr - ui * gi
                pi = ur * gi + ui * gr
                acc_r = pr if acc_r is None else acc_r + pr
                acc_i = pi if acc_i is None else acc_i + pi
            o_ref[io, re, :] = acc_r.astype(o_ref.dtype)
            o_ref[io, im, :] = acc_i.astype(o_ref.dtype)

    def body(rc, carry):
        rows(pl.multiple_of(rc * chunk, chunk), chunk)
        return carry

    lax.fori_loop(0, half // chunk, body, 0)

    @pl.when(i == 0)
    def _():
        for io in range(nblk):
            dc = None
            ny = None
            for jj in range(nblk):
                lag = io - jj + nblk - 1
                pr = u_ref[jj, 0:1, :] * gr_ref[lag, 0:1, :]
                pi = u_ref[jj, half:half + 1, :] * gi_ref[lag, 0:1, :]
                dc = pr if dc is None else dc + pr
                ny = pi if ny is None else ny + pi
            o_ref[io, 0:1, :] = dc.astype(o_ref.dtype)
            o_ref[io, half:half + 1, :] = ny.astype(o_ref.dtype)


def _hyena_mix(u, gr, gi, order, tm):
    nblk, n, ncol = u.shape
    c = gr.shape[3]
    tc = _tile(c, 256)
    cb = c // tc
    half = tm // 2
    chunk = min(64, half)
    gspec = pl.BlockSpec((None, 2 * nblk - 1, half, tc), lambda i, cc: (order, 0, i, cc % cb))
    return pl.pallas_call(
        functools.partial(_mix_body, nblk=nblk, half=half, chunk=chunk),
        grid=(n // tm, ncol // tc),
        in_specs=[pl.BlockSpec((nblk, tm, tc), lambda i, cc: (0, i, cc)), gspec, gspec],
        out_specs=pl.BlockSpec((nblk, tm, tc), lambda i, cc: (0, i, cc)),
        out_shape=jax.ShapeDtypeStruct((nblk, n, ncol), BF16),
        compiler_params=_cparams("parallel", "arbitrary"),
        name="hyena_block_mix",
    )(u, gr, gi)


def _inv_body(a_ref, y_ref, x_ref, o_ref, acc_ref):
    k_axis = 3
    kk = pl.program_id(k_axis)
    tn_dims = (((0,), (0,)), ((), ()))
    prod = lax.dot_general(a_ref[...], y_ref[...], tn_dims, preferred_element_type=F32)

    @pl.when(kk == 0)
    def _():
        acc_ref[...] = prod

    @pl.when(kk > 0)
    def _():
        acc_ref[...] += prod

    @pl.when(kk == pl.num_programs(k_axis) - 1)
    def _():
        o_ref[...] = (x_ref[...] * acc_ref[...]).astype(o_ref.dtype)


def _hyena_inv_gate(table, y, xg, c, *, token_major):
    n, t = table.shape
    nblk, _, ncol = y.shape
    l = nblk * t
    tmt = _tile(t, 1024)
    tb = t // tmt
    tk = _tile(n, 2048)
    tn = c if token_major else _tile(c, 1024)
    if token_major:
        out_spec = pl.BlockSpec((None, tmt, c), lambda io, ti, j, kk: (j, io * tb + ti, 0))
        out_shape = jax.ShapeDtypeStruct((ncol // c, l, c), BF16)
    else:
        out_spec = pl.BlockSpec((tmt, tn), lambda io, ti, j, kk: (io * tb + ti, j))
        out_shape = jax.ShapeDtypeStruct((l, ncol), BF16)
    return pl.pallas_call(
        _inv_body,
        grid=(nblk, tb, ncol // tn, n // tk),
        in_specs=[pl.BlockSpec((tk, tmt), lambda io, ti, j, kk: (kk, ti)),
                  pl.BlockSpec((None, tk, tn), lambda io, ti, j, kk: (io, kk, j)),
                  pl.BlockSpec((tmt, tn), lambda io, ti, j, kk: (io * tb + ti, j))],
        out_specs=out_spec,
        out_shape=out_shape,
        scratch_shapes=[pltpu.VMEM((tmt, tn), F32)],
        compiler_params=_cparams("parallel", "parallel", "parallel", "arbitrary"),
        name="hyena_inv_dft",
    )(table, y, xg)


def _hyena_group(p3, table, tm, emb2, filt_w, short_w, short_b, skip, deltas):
    order, c = skip.shape
    gseq, hb0 = _hyena_filters(emb2, *filt_w, deltas, order)
    gr, gi = _hyena_filter_spectrum(table, gseq, skip, hb0, tm)
    v, x1, x2 = _hyena_short_conv(p3, short_w, short_b, c)
    y1 = _hyena_mix(_hyena_fwd(table, v, tm), gr, gi, 0, tm)
    z1 = _hyena_inv_gate(table, y1, x1, c, token_major=False)
    y2 = _hyena_mix(_hyena_fwd(table, z1, tm), gr, gi, 1, tm)
    return _hyena_inv_gate(table, y2, x2, c, token_major=True)


def kernel(x, c, ctx, c_ctx, w_mod_down, w_mod_up, b_mod, w_in, w_out, hy_short_w, hy_short_b, hy_filt_w1, hy_filt_b1, hy_filt_w2, hy_filt_b2, hy_filt_w3, hy_filt_b3, hy_filt_w4, hy_filt_freq, hy_skip, na_rpb, cf_dw_w, cf_dw_b, cf_norm_g, cf_norm_b, w_ffn_gate, w_ffn_up, w_ffn_down, ln_mix_g, ln_mix_b, ln_ffn_g, ln_ffn_b):
    bsz, s, d = x.shape
    lc = ctx.shape[1]
    depth = w_in.shape[0]
    in_cols = w_in.shape[2]
    w_hy = hy_skip.shape[2]
    w_cf = cf_dw_w.shape[2]
    w_na = (in_cols - 3 * w_hy - 2 * w_cf) // 3
    heads = w_na // NA_HEAD_DIM
    off_na = 3 * w_hy
    off_kv = off_na + w_na
    off_conf = off_na + 3 * w_na
    alpha = (2 * depth) ** 0.25
    dh = NA_HEAD_DIM
    assert bsz + 1 <= 8 and off_conf % w_cf == 0 and off_na % dh == 0

    cos, sin = _rope_tables(s)
    t_lat = min(HYENA_BLOCK, s)
    t_ctx = min(HYENA_BLOCK, lc)
    assert s % t_lat == 0 and lc % t_ctx == 0
    tm_lat = min(1024, 2 * t_lat)
    tm_ctx = min(1024, 2 * t_ctx)
    table_lat = _dft_table(t_lat, tm_lat)
    table_ctx = _dft_table(t_ctx, tm_ctx)
    emb_lat = _hyena_embedding_pair(s)
    emb_ctx = _hyena_embedding_pair(lc)
    deltas = jnp.abs(jnp.linspace(HYENA_MIN_DECAY, HYENA_MAX_DECAY, w_hy, dtype=F32))
    deltas = deltas[None, :]

    cond8 = jnp.zeros((8, d), F32).at[:bsz].set(c).at[bsz].set(c_ctx)
    mods = _modulation_all(cond8, w_mod_down, w_mod_up, b_mod).reshape(depth, 8, N_MOD, d)

    def mod_lat(l, i):
        return mods[l, :bsz, i][:, None, :]

    def mod_ctx(l, i):
        return jnp.broadcast_to(mods[l, bsz, i][None, None, :], (bsz, 1, d))

    h = _modulate(x, mod_lat(0, 0), mod_lat(0, 1))
    hc = _modulate(ctx, mod_ctx(0, 0), mod_ctx(0, 1))

    for l in range(depth):
        last = l == depth - 1
        filt_w = (hy_filt_w1[l], hy_filt_b1[l], hy_filt_w2[l], hy_filt_b2[l], hy_filt_w3[l], hy_filt_b3[l],
                  hy_filt_w4[l], hy_filt_freq[l])
        bias_cls = _na_bias_pairs(na_rpb[l])

        hc2 = hc.reshape(bsz * lc, d)
        if last:
            kvc = _matmul(hc2, w_in, l, col0=off_kv, ncols=2 * w_na).reshape(bsz, lc, 2 * w_na)
            off_kc, off_vc = 0, heads
        else:
            pc = _matmul(hc2, w_in, l).reshape(bsz, lc, in_cols)
            kvc = pc
            off_kc, off_vc = off_kv // dh, (off_kv + w_na) // dh

        p = _matmul(h.reshape(bsz * s, d), w_in, l).reshape(bsz, s, in_cols)
        y_hy = _hyena_group(p, table_lat, tm_lat, emb_lat, filt_w, hy_short_w[l], hy_short_b[l], hy_skip[l], deltas)
        y_na = _neighbourhood_attention(p, kvc, bias_cls, cos, sin, off_q=off_na // dh, off_k=off_kv // dh,
                                        off_v=(off_kv + w_na) // dh, off_kc=off_kc, off_vc=off_vc, heads=heads)
        y_cf = _conformer(p, cf_dw_w[l], cf_dw_b[l], cf_norm_g[l], cf_norm_b[l], off=off_conf // w_cf)
        y = _matmul_concat3(y_hy.reshape(bsz * s, w_hy), y_na.reshape(bsz * s, w_na), y_cf.reshape(bsz * s, w_cf),
                            w_out, l).reshape(bsz, s, d)
        x, h = _ln_residual(x, y, mod_lat(l, 2), ln_mix_g[l], ln_mix_b[l], alpha, mod_lat(l, 3), mod_lat(l, 4))

        if not last:
            yc_hy = _hyena_group(pc, table_ctx, tm_ctx, emb_ctx, filt_w, hy_short_w[l], hy_short_b[l], hy_skip[l],
                                 deltas)
            yc_na = _context_attention(pc, off_q=off_na // dh, off_k=off_kv // dh, off_v=(off_kv + w_na) // dh,
                                       heads=heads)
            yc_cf = _conformer(pc, cf_dw_w[l], cf_dw_b[l], cf_norm_g[l], cf_norm_b[l], off=off_conf // w_cf)
            yc = _matmul_concat3(yc_hy.reshape(bsz * lc, w_hy), yc_na.reshape(bsz * lc, w_na),
                                 yc_cf.reshape(bsz * lc, w_cf), w_out, l).reshape(bsz, lc, d)
            ctx, hc = _ln_residual(ctx, yc, mod_ctx(l, 2), ln_mix_g[l], ln_mix_b[l], alpha, mod_ctx(l, 3),
                                   mod_ctx(l, 4))
            ac = _matmul_swiglu_in(hc.reshape(bsz * lc, d), w_ffn_gate, w_ffn_up, l)
            fc = _matmul_down(ac, w_ffn_down, l).reshape(bsz, lc, d)
            ctx, hc = _ln_residual(ctx, fc, mod_ctx(l, 5), ln_ffn_g[l], ln_ffn_b[l], alpha, mod_ctx(l + 1, 0),
                                   mod_ctx(l + 1, 1))

        a = _matmul_swiglu_in(h.reshape(bsz * s, d), w_ffn_gate, w_ffn_up, l)
        f = _matmul_down(a, w_ffn_down, l).reshape(bsz, s, d)
        if last:
            x, _ = _ln_residual(x, f, mod_lat(l, 5), ln_ffn_g[l], ln_ffn_b[l], alpha)
        else:
            x, h = _ln_residual(x, f, mod_lat(l, 5), ln_ffn_g[l], ln_ffn_b[l], alpha, mod_lat(l + 1, 0),
                                mod_lat(l + 1, 1))
    return x
```

```python
import functools
import math

import numpy as np
import jax
import jax.numpy as jnp
from jax import lax
from jax.experimental import pallas as pl
from jax.experimental.pallas import tpu as pltpu

F32 = jnp.float32
BF16 = jnp.bfloat16

GRID_W = 64
NA_HEAD_DIM = 128
NA_KH = 8
NA_KW = 16
NA_QROWS = 4
ROPE_THETA = 10000.0
HYENA_EMB_DIM = 33
HYENA_BLOCK = 1024
HYENA_TARGET = 1e-2
HYENA_MIN_DECAY = math.log(HYENA_TARGET) / 1.5
HYENA_MAX_DECAY = math.log(HYENA_TARGET) / 0.3
N_MOD = 6
LN_EPS = 1e-5
MASK_VALUE = -1e30

V7X_VMEM_LIMIT_BYTES = 56 * 1024 * 1024
V7X_VMEM_LIMIT_DOWN_BYTES = 60 * 1024 * 1024
SUBLANES = 8
CONV_HALO_ROWS = 16


def _cparams(*sem, vmem=V7X_VMEM_LIMIT_BYTES):
    return pltpu.CompilerParams(dimension_semantics=sem, vmem_limit_bytes=vmem)


def _tile(dim, pref, align=128):
    t = (min(dim, pref) // align) * align
    while t >= align:
        if dim % t == 0:
            return t
        t -= align
    return dim


def _sigmoid(z):
    return jax.nn.sigmoid(z)


def _mod_body(cond_ref, wd_ref, wu_ref, b_ref, o_ref, t_ref):
    @pl.when(pl.program_id(1) == 0)
    def _():
        cnd = cond_ref[...]
        t_ref[...] = jnp.dot(cnd * _sigmoid(cnd), wd_ref[...], preferred_element_type=F32)

    o_ref[...] = jnp.dot(t_ref[...], wu_ref[...], preferred_element_type=F32) + b_ref[...]


def _modulation_all(cond8, w_down, w_up, b_mod):
    depth, d, rank = w_down.shape
    n_out = w_up.shape[2]
    tn = _tile(n_out, 2048)
    return pl.pallas_call(
        _mod_body,
        grid=(depth, n_out // tn),
        in_specs=[
            pl.BlockSpec((8, d), lambda l, j: (0, 0)),
            pl.BlockSpec((None, d, rank), lambda l, j: (l, 0, 0)),
            pl.BlockSpec((None, rank, tn), lambda l, j: (l, 0, j)),
            pl.BlockSpec((None, 1, tn), lambda l, j: (l, 0, j)),
        ],
        out_specs=pl.BlockSpec((None, 8, tn), lambda l, j: (l, 0, j)),
        out_shape=jax.ShapeDtypeStruct((depth, 8, n_out), F32),
        scratch_shapes=[pltpu.VMEM((8, rank), F32)],
        compiler_params=_cparams("arbitrary", "arbitrary"),
        name="modulation",
    )(cond8, w_down, w_up, b_mod.reshape(depth, 1, n_out))


def _modulate_body(x_ref, shift_ref, scale_ref, o_ref):
    o_ref[...] = (x_ref[...] * (1.0 + scale_ref[...]) + shift_ref[...]).astype(o_ref.dtype)


def _modulate(x3, shift, scale):
    b, s, d = x3.shape
    ts = _tile(s, 256)
    tok = pl.BlockSpec((None, ts, d), lambda bi, i: (bi, i, 0))
    vec = pl.BlockSpec((None, 1, d), lambda bi, i: (bi, 0, 0))
    return pl.pallas_call(
        _modulate_body,
        grid=(b, s // ts),
        in_specs=[tok, vec, vec],
        out_specs=tok,
        out_shape=jax.ShapeDtypeStruct((b, s, d), BF16),
        compiler_params=_cparams("parallel", "parallel"),
        name="modulate",
    )(x3, shift, scale)


def _ln_body(*refs, alpha, with_h):
    if with_h:
        x_ref, y_ref, gate_ref, g_ref, b_ref, shift_ref, scale_ref, xo_ref, ho_ref = refs
    else:
        x_ref, y_ref, gate_ref, g_ref, b_ref, xo_ref = refs
    z = alpha * x_ref[...] + gate_ref[...] * y_ref[...].astype(F32)
    mu = jnp.mean(z, axis=-1, keepdims=True)
    zc = z - mu
    var = jnp.mean(zc * zc, axis=-1, keepdims=True)
    xn = zc * lax.rsqrt(var + LN_EPS) * g_ref[...] + b_ref[...]
    xo_ref[...] = xn
    if with_h:
        ho_ref[...] = (xn * (1.0 + scale_ref[...]) + shift_ref[...]).astype(ho_ref.dtype)


def _ln_residual(x3, y3, gate, g, b, alpha, shift=None, scale=None):
    bsz, s, d = x3.shape
    ts = _tile(s, 256)
    with_h = shift is not None
    tok = pl.BlockSpec((None, ts, d), lambda bi, i: (bi, i, 0))
    vec = pl.BlockSpec((None, 1, d), lambda bi, i: (bi, 0, 0))
    par = pl.BlockSpec((1, d), lambda bi, i: (0, 0))
    in_specs = [tok, tok, vec, par, par]
    args = [x3, y3, gate, g.reshape(1, d), b.reshape(1, d)]
    out_specs = [tok]
    out_shape = [jax.ShapeDtypeStruct((bsz, s, d), F32)]
    if with_h:
        in_specs += [vec, vec]
        args += [shift, scale]
        out_specs.append(tok)
        out_shape.append(jax.ShapeDtypeStruct((bsz, s, d), BF16))
    res = pl.pallas_call(
        functools.partial(_ln_body, alpha=alpha, with_h=with_h),
        grid=(bsz, s // ts),
        in_specs=in_specs,
        out_specs=out_specs,
        out_shape=out_shape,
        compiler_params=_cparams("parallel", "parallel"),
        name="ln_residual",
    )(*args)
    return (res[0], res[1]) if with_h else (res[0], None)


def _row_spec(tm, k, single_buffer):
    mode = dict(pipeline_mode=pl.Buffered(1)) if single_buffer else {}
    return pl.BlockSpec((tm, k), lambda i, j: (i, 0), **mode)


def _mm_tiles(m, ncols, tn_big=256):
    if m >= 2048:
        return _tile(m, 2048), _tile(ncols, tn_big)
    return m, _tile(ncols, 1024)


def _mm_body(a_ref, b_ref, o_ref):
    o_ref[...] = jnp.dot(a_ref[...], b_ref[...].astype(BF16), preferred_element_type=F32).astype(o_ref.dtype)


def _matmul(a, w, layer, *, col0=0, ncols=None, out_dtype=F32):
    m, k = a.shape
    ncols = w.shape[2] - col0 if ncols is None else ncols
    tm, tn = _mm_tiles(m, math.gcd(ncols, col0), tn_big=512)
    jb = col0 // tn
    return pl.pallas_call(
        _mm_body,
        grid=(m // tm, ncols // tn),
        in_specs=[
            _row_spec(tm, k, m > tm),
            pl.BlockSpec((None, k, tn), lambda i, j: (layer, 0, j + jb)),
        ],
        out_specs=pl.BlockSpec((tm, tn), lambda i, j: (i, j)),
        out_shape=jax.ShapeDtypeStruct((m, ncols), out_dtype),
        compiler_params=_cparams("parallel", "arbitrary"),
        name="matmul",
    )(a, w)


def _mm3_body(a1_ref, a2_ref, a3_ref, w_ref, o_ref):
    c1 = a1_ref.shape[1]
    c2 = a2_ref.shape[1]
    acc = jnp.dot(a1_ref[...], w_ref[0:c1, :].astype(BF16), preferred_element_type=F32)
    acc += jnp.dot(a2_ref[...], w_ref[c1:c1 + c2, :].astype(BF16), preferred_element_type=F32)
    acc += jnp.dot(a3_ref[...], w_ref[c1 + c2:, :].astype(BF16), preferred_element_type=F32)
    o_ref[...] = acc.astype(o_ref.dtype)


def _matmul_concat3(a1, a2, a3, w, layer):
    m = a1.shape[0]
    k, n = w.shape[1], w.shape[2]
    assert a1.shape[1] + a2.shape[1] + a3.shape[1] == k
    tm, tn = _mm_tiles(m, n, tn_big=512)
    return pl.pallas_call(
        _mm3_body,
        grid=(m // tm, n // tn),
        in_specs=[
            _row_spec(tm, a1.shape[1], m > tm),
            _row_spec(tm, a2.shape[1], m > tm),
            _row_spec(tm, a3.shape[1], m > tm),
            pl.BlockSpec((None, k, tn), lambda i, j: (layer, 0, j)),
        ],
        out_specs=pl.BlockSpec((tm, tn), lambda i, j: (i, j)),
        out_shape=jax.ShapeDtypeStruct((m, n), BF16),
        compiler_params=_cparams("parallel", "arbitrary"),
        name="matmul_out",
    )(a1, a2, a3, w)


def _gu_body(a_ref, wg_ref, wu_ref, o_ref):
    a = a_ref[...]
    g = jnp.dot(a, wg_ref[...].astype(BF16), preferred_element_type=F32)
    u = jnp.dot(a, wu_ref[...].astype(BF16), preferred_element_type=F32)
    o_ref[...] = (g * _sigmoid(g) * u).astype(o_ref.dtype)


def _matmul_swiglu_in(a, wg, wu, layer):
    m, k = a.shape
    n = wg.shape[2]
    tm, tn = _mm_tiles(m, n)
    tn = min(tn, _tile(n, 512))
    wspec = pl.BlockSpec((None, k, tn), lambda i, j: (layer, 0, j))
    return pl.pallas_call(
        _gu_body,
        grid=(m // tm, n // tn),
        in_specs=[_row_spec(tm, k, m > tm), wspec, wspec],
        out_specs=pl.BlockSpec((tm, tn), lambda i, j: (i, j)),
        out_shape=jax.ShapeDtypeStruct((m, n), BF16),
        compiler_params=_cparams("parallel", "arbitrary"),
        name="matmul_swiglu_in",
    )(a, wg, wu)


def _matmul_down(a, w, layer):
    m, k = a.shape
    n = w.shape[2]
    tm = _tile(m, 1024)
    tn = _tile(n, 256)
    return pl.pallas_call(
        _mm_body,
        grid=(m // tm, n // tn),
        in_specs=[_row_spec(tm, k, m > tm), pl.BlockSpec((None, k, tn), lambda i, j: (layer, 0, j))],
        out_specs=pl.BlockSpec((tm, tn), lambda i, j: (i, j)),
        out_shape=jax.ShapeDtypeStruct((m, n), BF16),
        compiler_params=_cparams("parallel", "arbitrary", vmem=V7X_VMEM_LIMIT_DOWN_BYTES),
        name="matmul_down",
    )(a, w)


def _rope_tables(s):
    t = jnp.arange(s)
    row = (t // GRID_W).astype(F32)[:, None]
    col = (t % GRID_W).astype(F32)[:, None]
    axis_dim = NA_HEAD_DIM // 2
    inv_freq = ROPE_THETA ** (-jnp.arange(0, axis_dim, 2, dtype=F32) / axis_dim)
    ang = jnp.concatenate([row * inv_freq, row * inv_freq, col * inv_freq, col * inv_freq], axis=-1)
    return jnp.cos(ang), jnp.sin(ang)


def _rotate_half_matrix():
    quarter = NA_HEAD_DIM // 4
    p = np.zeros((NA_HEAD_DIM, NA_HEAD_DIM), np.float32)
    for i in range(NA_HEAD_DIM):
        if (i % (2 * quarter)) < quarter:
            p[i + quarter, i] = -1.0
        else:
            p[i - quarter, i] = 1.0
    return jnp.asarray(p, BF16)


NA_SPAN = NA_QROWS + NA_KH - 1
NA_DROWS = 2 * NA_KH
NA_CLASS_BASES = (0, -(NA_KH // 2), -(NA_KH - 1))


def _bias_table_body(rpb_ref, onehot_ref, mask_ref, o_ref):
    o_ref[...] = jnp.dot(rpb_ref[...], onehot_ref[...], preferred_element_type=F32,
                         precision=lax.Precision.HIGHEST) + mask_ref[...]


def _na_bias_pairs(rpb):
    h, nr, nc = rpb.shape
    w = GRID_W
    cq = np.arange(w)[:, None]
    ck = np.arange(w)[None, :]
    d_col = np.clip(ck - cq, -(NA_KW - 1), NA_KW - 1) + (NA_KW - 1)
    col_start = np.clip(cq - NA_KW // 2, 0, w - NA_KW)
    col_ok = (ck >= col_start) & (ck < col_start + NA_KW)
    ncp = -(-nc // 8) * 8
    onehot = np.zeros((2 * ncp, w, 2, w), np.float32)
    for half in range(2):
        onehot[half * ncp + d_col, cq, half, ck] = 1.0
    maskadd = np.where(col_ok, 0.0, MASK_VALUE).astype(np.float32)
    maskadd = np.broadcast_to(maskadd[:, None, :], (w, 2, w)).reshape(1, 2 * w * w)
    rp = jnp.pad(rpb, ((0, 0), (1, NA_DROWS - nr), (0, ncp - nc)))
    rpb_pairs = jnp.concatenate([rp[:, :NA_DROWS], rp[:, 1:NA_DROWS + 1]], axis=-1).reshape(h * NA_DROWS, 2 * ncp)
    out = pl.pallas_call(
        _bias_table_body,
        out_shape=jax.ShapeDtypeStruct((h * NA_DROWS, 2 * w * w), F32),
        compiler_params=_cparams(),
        name="na_bias_table",
    )(rpb_pairs, jnp.asarray(onehot.reshape(2 * ncp, 2 * w * w)), jnp.asarray(maskadd))
    return out.reshape(h, NA_DROWS, w, 2 * w)


def _na_window_valid(cls, a, j):
    if j >= NA_SPAN:
        return False
    if cls == 0:
        return j < NA_KH
    if cls == 2:
        return j >= NA_SPAN - NA_KH
    return 0 <= NA_CLASS_BASES[1] + j - a + NA_KH // 2 < NA_KH


def _na_build_bias(pair_ref, bias_s):
    w = GRID_W
    left = lax.broadcasted_iota(jnp.int32, (w, 2 * w), 1) < w
    for cls, base in enumerate(NA_CLASS_BASES):
        for a in range(NA_QROWS):
            for pj in range(-(-NA_SPAN // 2)):
                j0 = 2 * pj
                v0, v1 = _na_window_valid(cls, a, j0), _na_window_valid(cls, a, j0 + 1)
                width = 2 * w if j0 + 1 < NA_SPAN else w
                if v0 or v1:
                    tile = pair_ref[base + j0 - a + NA_KH]
                    if not v1:
                        tile = jnp.where(left, tile, MASK_VALUE)
                    elif not v0:
                        tile = jnp.where(left, MASK_VALUE, tile)
                else:
                    tile = jnp.full((w, 2 * w), MASK_VALUE, F32)
                bias_s[cls, a * w:(a + 1) * w, j0 * w:j0 * w + width] = tile[:, :width]


def _na_body(q_ref, k_ref, v_ref, kc_ref, vc_ref, pair_ref, cos_ref, sin_ref, rot_ref, o_ref,
             qr_s, kr_s, qb_s, vb_s, bias_s, *, rows, scale):
    @pl.when(pl.program_id(1) == 0)
    def _():
        _na_build_bias(pair_ref, bias_s)

    cos = cos_ref[...]
    sin = sin_ref[...]

    def rope(z, zb):
        return z * cos + jnp.dot(zb, rot_ref[...], preferred_element_type=F32) * sin

    q = q_ref[...] * scale
    qb = q.astype(BF16)
    qb_s[...] = qb
    qr_s[...] = rope(q, qb).astype(BF16)
    k = k_ref[...]
    kr_s[...] = rope(k, k.astype(BF16)).astype(BF16)
    vb_s[...] = v_ref[...].astype(BF16)
    kc = kc_ref[...].astype(BF16)
    vc = vc_ref[...].astype(BF16)
    nt = (((1,), (1,)), ((), ()))
    groups = rows // NA_QROWS
    qn = NA_QROWS * GRID_W
    kn = NA_SPAN * GRID_W

    def body(g, carry):
        ks = jnp.clip(g * NA_QROWS + NA_CLASS_BASES[1], 0, rows - NA_SPAN)
        cls = jnp.where(g == 0, 0, jnp.where(g == groups - 1, 2, 1))
        q0 = pl.multiple_of(g * qn, qn)
        k0 = pl.multiple_of(ks * GRID_W, GRID_W)
        s_win = lax.dot_general(qr_s[pl.ds(q0, qn), :], kr_s[pl.ds(k0, kn), :], nt,
                                preferred_element_type=F32) + bias_s[cls]
        s_ctx = lax.dot_general(qb_s[pl.ds(q0, qn), :], kc, nt, preferred_element_type=F32)
        m = jnp.maximum(jnp.max(s_win, axis=-1, keepdims=True), jnp.max(s_ctx, axis=-1, keepdims=True))
        e_win = jnp.exp(s_win - m)
        e_ctx = jnp.exp(s_ctx - m)
        den = jnp.sum(e_win, axis=-1, keepdims=True) + jnp.sum(e_ctx, axis=-1, keepdims=True)
        o = jnp.dot(e_win.astype(BF16), vb_s[pl.ds(k0, kn), :], preferred_element_type=F32)
        o += jnp.dot(e_ctx.astype(BF16), vc, preferred_element_type=F32)
        o_ref[pl.ds(q0, qn), :] = (o / den).astype(o_ref.dtype)
        return carry

    lax.fori_loop(0, groups, body, 0, unroll=4)


def _neighbourhood_attention(p3, kvc3, bias_pairs, cos, sin, *, off_q, off_k, off_v, off_kc, off_vc, heads):
    bsz, s, _ = p3.shape
    lc = kvc3.shape[1]
    rows = s // GRID_W
    assert s % GRID_W == 0 and rows % NA_QROWS == 0 and rows // NA_QROWS >= 3
    dh = NA_HEAD_DIM

    def col(off):
        return pl.BlockSpec((None, s, dh), lambda h, b: (b, 0, off + h))

    def colc(off):
        return pl.BlockSpec((None, lc, dh), lambda h, b: (b, 0, off + h))

    tab = pl.BlockSpec((s, dh), lambda h, b: (0, 0))
    return pl.pallas_call(
        functools.partial(_na_body, rows=rows, scale=dh ** -0.5),
        grid=(heads, bsz),
        in_specs=[col(off_q), col(off_k), col(off_v), colc(off_kc), colc(off_vc),
                  pl.BlockSpec((None, NA_DROWS, GRID_W, 2 * GRID_W), lambda h, b: (h, 0, 0, 0)), tab, tab,
                  pl.BlockSpec((dh, dh), lambda h, b: (0, 0))],
        out_specs=pl.BlockSpec((None, s, dh), lambda h, b: (b, 0, h)),
        out_shape=jax.ShapeDtypeStruct((bsz, s, heads * dh), BF16),
        scratch_shapes=[pltpu.VMEM((s, dh), BF16)] * 4
        + [pltpu.VMEM((len(NA_CLASS_BASES), NA_QROWS * GRID_W, NA_SPAN * GRID_W), F32)],
        compiler_params=_cparams("arbitrary", "arbitrary"),
        name="neighbourhood_attention",
    )(p3, p3, p3, kvc3, kvc3, bias_pairs, cos, sin, _rotate_half_matrix())


def _ctx_attn_body(q_ref, k_ref, v_ref, o_ref, *, scale):
    nt = (((1,), (1,)), ((), ()))
    s = lax.dot_general(q_ref[...].astype(BF16), k_ref[...].astype(BF16), nt, preferred_element_type=F32) * scale
    m = jnp.max(s, axis=-1, keepdims=True)
    e = jnp.exp(s - m)
    den = jnp.sum(e, axis=-1, keepdims=True)
    o = jnp.dot(e.astype(BF16), v_ref[...].astype(BF16), preferred_element_type=F32)
    o_ref[...] = (o / den).astype(o_ref.dtype)


def _context_attention(pc3, *, off_q, off_k, off_v, heads):
    bsz, lc, _ = pc3.shape
    dh = NA_HEAD_DIM

    def col(off):
        return pl.BlockSpec((None, lc, dh), lambda b, h: (b, 0, off + h))

    return pl.pallas_call(
        functools.partial(_ctx_attn_body, scale=dh ** -0.5),
        grid=(bsz, heads),
        in_specs=[col(off_q), col(off_k), col(off_v)],
        out_specs=pl.BlockSpec((None, lc, dh), lambda b, h: (b, 0, h)),
        out_shape=jax.ShapeDtypeStruct((bsz, lc, heads * dh), BF16),
        compiler_params=_cparams("parallel", "parallel"),
        name="context_attention",
    )(pc3, pc3, pc3)


def _conformer_body(a_ref, g_ref, pa_ref, pg_ref, na_ref, ng_ref, w_ref, wb_ref, lg_ref, lb_ref, o_ref, u_s, sh_s,
                    *, ts, taps):
    i = pl.program_id(1)
    halo = CONV_HALO_ROWS
    a = a_ref[...]
    g = g_ref[...]
    u_s[halo:halo + ts, :] = a * _sigmoid(g)
    prev = pa_ref[...] * _sigmoid(pg_ref[...])
    u_s[0:halo, :] = jnp.where(i > 0, prev, 0.0)
    nxt = na_ref[...] * _sigmoid(ng_ref[...])
    u_s[halo + ts:, :] = jnp.where(i < pl.num_programs(1) - 1, nxt, 0.0)
    span = ts + 2 * halo - SUBLANES
    for r in range(SUBLANES):
        sh_s[r] = u_s[r:r + span, :]
    base = halo - taps // 2
    z = wb_ref[...]
    for t in range(taps):
        a8, r = divmod(base + t, SUBLANES)
        z += w_ref[t:t + 1, :] * sh_s[r, a8 * SUBLANES:a8 * SUBLANES + ts, :]
    mu = jnp.mean(z, axis=-1, keepdims=True)
    zc = z - mu
    var = jnp.mean(zc * zc, axis=-1, keepdims=True)
    zn = zc * lax.rsqrt(var + LN_EPS) * lg_ref[...] + lb_ref[...]
    o_ref[...] = (zn * _sigmoid(zn)).astype(o_ref.dtype)


def _conformer(p3, dw_w, dw_b, ln_g, ln_b, *, off):
    bsz, l, _ = p3.shape
    taps, c = dw_w.shape
    halo = CONV_HALO_ROWS
    assert taps // 2 < halo and l % halo == 0
    ts = _tile(l, 256)
    hb = ts // halo
    nh = l // halo

    def cur(o):
        return pl.BlockSpec((None, ts, c), lambda b, i: (b, i, o))

    def prev(o):
        return pl.BlockSpec((None, halo, c), lambda b, i: (b, jnp.maximum(i * hb - 1, 0), o))

    def nxt(o):
        return pl.BlockSpec((None, halo, c), lambda b, i: (b, jnp.minimum((i + 1) * hb, nh - 1), o))

    par = pl.BlockSpec((1, c), lambda b, i: (0, 0))
    return pl.pallas_call(
        functools.partial(_conformer_body, ts=ts, taps=taps),
        grid=(bsz, l // ts),
        in_specs=[cur(off), cur(off + 1), prev(off), prev(off + 1), nxt(off), nxt(off + 1),
                  pl.BlockSpec((taps, c), lambda b, i: (0, 0)), par, par, par],
        out_specs=pl.BlockSpec((None, ts, c), lambda b, i: (b, i, 0)),
        out_shape=jax.ShapeDtypeStruct((bsz, l, c), BF16),
        scratch_shapes=[pltpu.VMEM((ts + 2 * halo, c), F32),
                        pltpu.VMEM((SUBLANES, ts + 2 * halo - SUBLANES, c), F32)],
        compiler_params=_cparams("parallel", "arbitrary"),
        name="conformer",
    )(p3, p3, p3, p3, p3, p3, dw_w, dw_b.reshape(1, c), ln_g.reshape(1, c), ln_b.reshape(1, c))


def _hy_short_body(cur_ref, prev_ref, next_ref, w_ref, b_ref, v_ref, x1_ref, x2_ref, u_s, *, ts, c):
    i = pl.program_id(1)
    halo = CONV_HALO_ROWS
    u_s[halo:halo + ts, :] = cur_ref[...]
    u_s[0:halo, :] = jnp.where(i > 0, prev_ref[...], 0.0)
    u_s[halo + ts:, :] = jnp.where(i < pl.num_programs(1) - 1, next_ref[...], 0.0)
    z = (b_ref[...] + w_ref[0:1, :] * u_s[halo - 1:halo - 1 + ts, :] + w_ref[1:2, :] * u_s[halo:halo + ts, :]
         + w_ref[2:3, :] * u_s[halo + 1:halo + 1 + ts, :])
    v_ref[...] = z[:, 0:c].astype(v_ref.dtype)
    x1_ref[...] = z[:, c:2 * c]
    x2_ref[...] = z[:, 2 * c:3 * c]


def _hyena_short_conv(p3, short_w, short_b, c):
    bsz, l, _ = p3.shape
    halo = CONV_HALO_ROWS
    ts = _tile(l, 512)
    hb = ts // halo
    nh = l // halo
    w3 = 3 * c
    out = pl.BlockSpec((ts, c), lambda b, i: (i, b))
    return pl.pallas_call(
        functools.partial(_hy_short_body, ts=ts, c=c),
        grid=(bsz, l // ts),
        in_specs=[
            pl.BlockSpec((None, ts, w3), lambda b, i: (b, i, 0)),
            pl.BlockSpec((None, halo, w3), lambda b, i: (b, jnp.maximum(i * hb - 1, 0), 0)),
            pl.BlockSpec((None, halo, w3), lambda b, i: (b, jnp.minimum((i + 1) * hb, nh - 1), 0)),
            pl.BlockSpec((3, w3), lambda b, i: (0, 0)),
            pl.BlockSpec((1, w3), lambda b, i: (0, 0)),
        ],
        out_specs=[out, out, out],
        out_shape=[jax.ShapeDtypeStruct((l, bsz * c), BF16), jax.ShapeDtypeStruct((l, bsz * c), F32),
                   jax.ShapeDtypeStruct((l, bsz * c), F32)],
        scratch_shapes=[pltpu.VMEM((ts + 2 * halo, w3), F32)],
        compiler_params=_cparams("parallel", "arbitrary"),
        name="hyena_short_conv",
    )(p3, p3, p3, short_w, short_b.reshape(1, w3))


def _hyena_embedding(l):
    t = jnp.linspace(0.0, 1.0, l, dtype=F32)[:, None]
    bands = (HYENA_EMB_DIM - 1) // 2
    omega = (2.0 * math.pi / l) * jnp.arange(l, dtype=F32)[:, None]
    f = jnp.linspace(1e-4, bands - 1, bands, dtype=F32)[None, :]
    return jnp.concatenate([t, jnp.cos(f * omega), -jnp.sin(f * omega)], axis=-1)


def _hyena_embedding_pair(l):
    emb = _hyena_embedding(l)
    return jnp.stack([jnp.concatenate([emb[0:1], jnp.flip(emb[1:], axis=0)], axis=0), emb])


def _hy_filter_body(emb_ref, embt_ref, w1_ref, b1_ref, w2_ref, b2_ref, w3_ref, b3_ref, freq_ref, w4_ref, delta_ref,
                    g_ref, hb0_ref, hid_s):
    dirn = pl.program_id(2)
    first = jnp.logical_and(jnp.logical_and(pl.program_id(0) == 0, pl.program_id(1) == 0), dirn == 0)
    tn_dims = (((0,), (0,)), ((), ()))

    @pl.when(first)
    def _():
        freq = freq_ref[...]
        for d in range(2):
            hid = lax.dot_general(w1_ref[...], embt_ref[d], tn_dims, preferred_element_type=F32)
            hid = jnp.sin(freq * (hid + b1_ref[...]))
            hid = jnp.sin(freq * (lax.dot_general(w2_ref[...], hid, tn_dims, preferred_element_type=F32) + b2_ref[...]))
            hid_s[d] = jnp.sin(freq * (lax.dot_general(w3_ref[...], hid, tn_dims, preferred_element_type=F32)
                                       + b3_ref[...]))

    t = emb_ref[dirn][:, 0:1]
    h = lax.dot_general(hid_s[dirn], w4_ref[...], tn_dims, preferred_element_type=F32)
    h = h * jnp.exp(-t * delta_ref[...])
    h = h / (jnp.sum(jnp.abs(h), axis=0, keepdims=True) + 1e-6)
    is_bwd = dirn == 0

    @pl.when(is_bwd)
    def _():
        hb0_ref[...] = h[0:1, :].astype(hb0_ref.dtype)

    row = lax.broadcasted_iota(jnp.int32, (h.shape[0], 1), 0)
    g_ref[...] = jnp.where(jnp.logical_and(row == 0, is_bwd), 0.0, h).astype(g_ref.dtype)


def _hyena_filters(emb2, w1, b1, w2, b2, w3, b3, w4, freq, deltas, order):
    _, l, e = emb2.shape
    hf = w1.shape[1]
    c = w4.shape[1] // (2 * order)
    tn = _tile(c, 512)
    cb = c // tn

    def full(shape):
        return pl.BlockSpec(shape, lambda o, ct, dn: (0,) * len(shape))

    return pl.pallas_call(
        _hy_filter_body,
        grid=(order, cb, 2),
        in_specs=[full((2, l, e)), full((2, e, l)), full((e, hf)), full((hf, 1)), full((hf, hf)), full((hf, 1)),
                  full((hf, hf)), full((hf, 1)), full((hf, 1)),
                  pl.BlockSpec((hf, tn), lambda o, ct, dn: (0, (2 * o + 1 - dn) * cb + ct)),
                  pl.BlockSpec((1, tn), lambda o, ct, dn: (0, ct))],
        out_specs=[pl.BlockSpec((None, l, tn), lambda o, ct, dn: (o, dn, ct)),
                   pl.BlockSpec((None, 1, tn), lambda o, ct, dn: (o, 0, ct))],
        out_shape=[jax.ShapeDtypeStruct((order, 2 * l, c), BF16), jax.ShapeDtypeStruct((order, 1, c), BF16)],
        scratch_shapes=[pltpu.VMEM((2, hf, l), F32)],
        compiler_params=_cparams("arbitrary", "arbitrary", "arbitrary"),
        name="hyena_filters",
    )(emb2, jnp.swapaxes(emb2, 1, 2), w1, b1.reshape(hf, 1), w2, b2.reshape(hf, 1), w3, b3.reshape(hf, 1),
      freq.reshape(hf, 1), w4, deltas)


def _dft_table(l, tm):
    n = 2 * l
    half = tm // 2
    r = np.arange(n)
    f_np = (r // tm) * half + (r % half)
    is_im_np = (r % tm) >= half
    sq = 1 << (int(math.log2(l)) // 2)
    hi = l // sq
    f = jnp.asarray(f_np, jnp.int32)[:, None]
    w0 = 2.0 * math.pi / n
    a_hi = ((f * (jnp.arange(hi, dtype=jnp.int32)[None, :] * sq)) % n).astype(F32) * w0
    a_lo = ((f * jnp.arange(sq, dtype=jnp.int32)[None, :]) % n).astype(F32) * w0
    ch, sh = jnp.cos(a_hi)[:, :, None], jnp.sin(a_hi)[:, :, None]
    cl, sl = jnp.cos(a_lo)[:, None, :], jnp.sin(a_lo)[:, None, :]
    cosv = (ch * cl - sh * sl).reshape(n, l)
    sinv = (sh * cl + ch * sl).reshape(n, l)
    nyq = jnp.asarray(np.where(np.arange(l) % 2 == 0, 1.0, -1.0), F32)[None, :]
    is_im = jnp.asarray(is_im_np)[:, None]
    table = jnp.where(is_im, jnp.where(f == 0, nyq, -sinv), cosv)
    return table.astype(BF16)


def _spec_body(a_ref, blk_ref, skip_ref, hb0_ref, gr_ref, gi_ref, prev_s, c_s, *, n, lag0):
    i = pl.program_id(0)
    b = pl.program_id(3)
    half = a_ref.shape[0] // 2
    blk = blk_ref[...]
    cur = jnp.dot(a_ref[...], blk, preferred_element_type=F32)

    @pl.when(b > 0)
    def _():
        sp_ = prev_s[...]
        c_prev = c_s[...]
        row = lax.broadcasted_iota(jnp.int32, (half, 1), 0)
        is_dc = jnp.logical_and(row == 0, i == 0)
        sigma = (1 - 2 * (row % 2)).astype(F32)
        gr = cur[:half] + sigma * (sp_[:half] - c_prev)
        gi = jnp.where(is_dc, cur[half:] + sp_[half:] - c_prev, cur[half:] + sigma * sp_[half:])
        add = jnp.where(b - 1 == lag0, skip_ref[...] + hb0_ref[...].astype(F32), 0.0)
        gr = gr + add
        gi = jnp.where(is_dc, gi + add, gi)
        w = jnp.where(is_dc, 1.0 / n, 2.0 / n)
        gr_ref[...] = gr * w
        gi_ref[...] = gi * w

    prev_s[...] = cur
    c_s[...] = blk[0:1, :].astype(F32)


def _hyena_filter_spectrum(table, gseq, skip, hb0, tm):
    n, t = table.shape
    order, l2, c = gseq.shape
    nblk = l2 // t
    nlag = nblk - 1
    tn = _tile(c, 1024)
    half = tm // 2
    out = pl.BlockSpec((None, None, half, tn), lambda i, o, j, b: (o, jnp.maximum(b - 1, 0), i, j))
    vec = pl.BlockSpec((None, 1, tn), lambda i, o, j, b: (o, 0, j))
    return pl.pallas_call(
        functools.partial(_spec_body, n=n, lag0=nblk // 2 - 1),
        grid=(n // tm, order, c // tn, nblk),
        in_specs=[
            pl.BlockSpec((tm, t), lambda i, o, j, b: (i, 0)),
            pl.BlockSpec((None, t, tn), lambda i, o, j, b: (o, b, j)),
            vec, vec,
        ],
        out_specs=[out, out],
        out_shape=[jax.ShapeDtypeStruct((order, nlag, t, c), F32)] * 2,
        scratch_shapes=[pltpu.VMEM((tm, tn), F32), pltpu.VMEM((1, tn), F32)],
        compiler_params=_cparams("arbitrary", "arbitrary", "arbitrary", "arbitrary"),
        name="hyena_filter_spectrum",
    )(table, gseq, skip.reshape(order, 1, c), hb0)


def _fwd_body(a_ref, u_ref, o_ref):
    o_ref[...] = jnp.dot(a_ref[...], u_ref[...], preferred_element_type=F32)


def _hyena_fwd(table, u, tm):
    n, t = table.shape
    l, ncol = u.shape
    tn = _tile(ncol, 1024)
    return pl.pallas_call(
        _fwd_body,
        grid=(n // tm, l // t, ncol // tn),
        in_specs=[pl.BlockSpec((tm, t), lambda i, jb, cc: (i, 0)),
                  pl.BlockSpec((t, tn), lambda i, jb, cc: (jb, cc))],
        out_specs=pl.BlockSpec((None, tm, tn), lambda i, jb, cc: (jb, i, cc)),
        out_shape=jax.ShapeDtypeStruct((l // t, n, ncol), F32),
        compiler_params=_cparams("parallel", "arbitrary", "arbitrary"),
        name="hyena_fwd_dft",
    )(table, u)


def _mix_body(u_ref, gr_ref, gi_ref, o_ref, *, nblk, half, chunk):
    i = pl.program_id(0)

    def rows(r0, size):
        re = pl.ds(r0, size)
        im = pl.ds(half + r0, size)
        for io in range(nblk):
            acc_r = None
            acc_i = None
            for jj in range(nblk):
                lag = io - jj + nblk - 1
                ur, ui = u_ref[jj, re, :], u_ref[jj, im, :]
                gr, gi = gr_ref[lag, pl.ds(r0, size), :], gi_ref[lag, pl.ds(r0, size), :]
                pr = ur * gr - ui * gi
                pi = ur * gi + ui * gr
                acc_r = pr if acc_r is None else acc_r + pr
                acc_i = pi if acc_i is None else acc_i + pi
            o_ref[io, re, :] = acc_r.astype(o_ref.dtype)
            o_ref[io, im, :] = acc_i.astype(o_ref.dtype)

    def body(rc, carry):
        rows(pl.multiple_of(rc * chunk, chunk), chunk)
        return carry

    lax.fori_loop(0, half // chunk, body, 0)

    @pl.when(i == 0)
    def _():
        for io in range(nblk):
            dc = None
            ny = None
            for jj in range(nblk):
                lag = io - jj + nblk - 1
                pr = u_ref[jj, 0:1, :] * gr_ref[lag, 0:1, :]
                pi = u_ref[jj, half:half + 1, :] * gi_ref[lag, 0:1, :]
                dc = pr if dc is None else dc + pr
                ny = pi if ny is None else ny + pi
            o_ref[io, 0:1, :] = dc.astype(o_ref.dtype)
            o_ref[io, half:half + 1, :] = ny.astype(o_ref.dtype)


def _hyena_mix(u, gr, gi, order, tm):
    nblk, n, ncol = u.shape
    c = gr.shape[3]
    tc = _tile(c, 256)
    cb = c // tc
    half = tm // 2
    chunk = min(64, half)
    gspec = pl.BlockSpec((None, 2 * nblk - 1, half, tc), lambda i, cc: (order, 0, i, cc % cb))
    return pl.pallas_call(
        functools.partial(_mix_body, nblk=nblk, half=half, chunk=chunk),
        grid=(n // tm, ncol // tc),
        in_specs=[pl.BlockSpec((nblk, tm, tc), lambda i, cc: (0, i, cc)), gspec, gspec],
        out_specs=pl.BlockSpec((nblk, tm, tc), lambda i, cc: (0, i, cc)),
        out_shape=jax.ShapeDtypeStruct((nblk, n, ncol), BF16),
        compiler_params=_cparams("parallel", "arbitrary"),
        name="hyena_block_mix",
    )(u, gr, gi)


def _inv_body(a_ref, y_ref, x_ref, o_ref, acc_ref):
    k_axis = 3
    kk = pl.program_id(k_axis)
    tn_dims = (((0,), (0,)), ((), ()))
    prod = lax.dot_general(a_ref[...], y_ref[...], tn_dims, preferred_element_type=F32)

    @pl.when(kk == 0)
    def _():
        acc_ref[...] = prod

    @pl.when(kk > 0)
    def _():
        acc_ref[...] += prod

    @pl.when(kk == pl.num_programs(k_axis) - 1)
    def _():
        o_ref[...] = (x_ref[...] * acc_ref[...]).astype(o_ref.dtype)


def _hyena_inv_gate(table, y, xg, c, *, token_major):
    n, t = table.shape
    nblk, _, ncol = y.shape
    l = nblk * t
    tmt = _tile(t, 1024)
    tb = t // tmt
    tk = _tile(n, 2048)
    tn = c if token_major else _tile(c, 1024)
    if token_major:
        out_spec = pl.BlockSpec((None, tmt, c), lambda io, ti, j, kk: (j, io * tb + ti, 0))
        out_shape = jax.ShapeDtypeStruct((ncol // c, l, c), BF16)
    else:
        out_spec = pl.BlockSpec((tmt, tn), lambda io, ti, j, kk: (io * tb + ti, j))
        out_shape = jax.ShapeDtypeStruct((l, ncol), BF16)
    return pl.pallas_call(
        _inv_body,
        grid=(nblk, tb, ncol // tn, n // tk),
        in_specs=[pl.BlockSpec((tk, tmt), lambda io, ti, j, kk: (kk, ti)),
                  pl.BlockSpec((None, tk, tn), lambda io, ti, j, kk: (io, kk, j)),
                  pl.BlockSpec((tmt, tn), lambda io, ti, j, kk: (io * tb + ti, j))],
        out_specs=out_spec,
        out_shape=out_shape,
        scratch_shapes=[pltpu.VMEM((tmt, tn), F32)],
        compiler_params=_cparams("parallel", "parallel", "parallel", "arbitrary"),
        name="hyena_inv_dft",
    )(table, y, xg)


def _hyena_group(p3, table, tm, emb2, filt_w, short_w, short_b, skip, deltas):
    order, c = skip.shape
    gseq, hb0 = _hyena_filters(emb2, *filt_w, deltas, order)
    gr, gi = _hyena_filter_spectrum(table, gseq, skip, hb0, tm)
    v, x1, x2 = _hyena_short_conv(p3, short_w, short_b, c)
    y1 = _hyena_mix(_hyena_fwd(table, v, tm), gr, gi, 0, tm)
    z1 = _hyena_inv_gate(table, y1, x1, c, token_major=False)
    y2 = _hyena_mix(_hyena_fwd(table, z1, tm), gr, gi, 1, tm)
    return _hyena_inv_gate(table, y2, x2, c, token_major=True)


def kernel(x, c, ctx, c_ctx, w_mod_down, w_mod_up, b_mod, w_in, w_out, hy_short_w, hy_short_b, hy_filt_w1, hy_filt_b1, hy_filt_w2, hy_filt_b2, hy_filt_w3, hy_filt_b3, hy_filt_w4, hy_filt_freq, hy_skip, na_rpb, cf_dw_w, cf_dw_b, cf_norm_g, cf_norm_b, w_ffn_gate, w_ffn_up, w_ffn_down, ln_mix_g, ln_mix_b, ln_ffn_g, ln_ffn_b):
    bsz, s, d = x.shape
    lc = ctx.shape[1]
    depth = w_in.shape[0]
    in_cols = w_in.shape[2]
    w_hy = hy_skip.shape[2]
    w_cf = cf_dw_w.shape[2]
    w_na = (in_cols - 3 * w_hy - 2 * w_cf) // 3
    heads = w_na // NA_HEAD_DIM
    off_na = 3 * w_hy
    off_kv = off_na + w_na
    off_conf = off_na + 3 * w_na
    alpha = (2 * depth) ** 0.25
    dh = NA_HEAD_DIM
    assert bsz + 1 <= 8 and off_conf % w_cf == 0 and off_na % dh == 0

    cos, sin = _rope_tables(s)
    t_lat = min(HYENA_BLOCK, s)
    t_ctx = min(HYENA_BLOCK, lc)
    assert s % t_lat == 0 and lc % t_ctx == 0
    tm_lat = min(1024, 2 * t_lat)
    tm_ctx = min(1024, 2 * t_ctx)
    table_lat = _dft_table(t_lat, tm_lat)
    table_ctx = _dft_table(t_ctx, tm_ctx)
    emb_lat = _hyena_embedding_pair(s)
    emb_ctx = _hyena_embedding_pair(lc)
    deltas = jnp.abs(jnp.linspace(HYENA_MIN_DECAY, HYENA_MAX_DECAY, w_hy, dtype=F32))
    deltas = deltas[None, :]

    cond8 = jnp.zeros((8, d), F32).at[:bsz].set(c).at[bsz].set(c_ctx)
    mods = _modulation_all(cond8, w_mod_down, w_mod_up, b_mod).reshape(depth, 8, N_MOD, d)

    def mod_lat(l, i):
        return mods[l, :bsz, i][:, None, :]

    def mod_ctx(l, i):
        return jnp.broadcast_to(mods[l, bsz, i][None, None, :], (bsz, 1, d))

    h = _modulate(x, mod_lat(0, 0), mod_lat(0, 1))
    hc = _modulate(ctx, mod_ctx(0, 0), mod_ctx(0, 1))

    for l in range(depth):
        last = l == depth - 1
        filt_w = (hy_filt_w1[l], hy_filt_b1[l], hy_filt_w2[l], hy_filt_b2[l], hy_filt_w3[l], hy_filt_b3[l],
                  hy_filt_w4[l], hy_filt_freq[l])
        bias_cls = _na_bias_pairs(na_rpb[l])

        hc2 = hc.reshape(bsz * lc, d)
        if last:
            kvc = _matmul(hc2, w_in, l, col0=off_kv, ncols=2 * w_na).reshape(bsz, lc, 2 * w_na)
            off_kc, off_vc = 0, heads
        else:
            pc = _matmul(hc2, w_in, l).reshape(bsz, lc, in_cols)
            kvc = pc
            off_kc, off_vc = off_kv // dh, (off_kv + w_na) // dh

        p = _matmul(h.reshape(bsz * s, d), w_in, l).reshape(bsz, s, in_cols)
        y_hy = _hyena_group(p, table_lat, tm_lat, emb_lat, filt_w, hy_short_w[l], hy_short_b[l], hy_skip[l], deltas)
        y_na = _neighbourhood_attention(p, kvc, bias_cls, cos, sin, off_q=off_na // dh, off_k=off_kv // dh,
                                        off_v=(off_kv + w_na) // dh, off_kc=off_kc, off_vc=off_vc, heads=heads)
        y_cf = _conformer(p, cf_dw_w[l], cf_dw_b[l], cf_norm_g[l], cf_norm_b[l], off=off_conf // w_cf)
        y = _matmul_concat3(y_hy.reshape(bsz * s, w_hy), y_na.reshape(bsz * s, w_na), y_cf.reshape(bsz * s, w_cf),
                            w_out, l).reshape(bsz, s, d)
        x, h = _ln_residual(x, y, mod_lat(l, 2), ln_mix_g[l], ln_mix_b[l], alpha, mod_lat(l, 3), mod_lat(l, 4))

        if not last:
            yc_hy = _hyena_group(pc, table_ctx, tm_ctx, emb_ctx, filt_w, hy_short_w[l], hy_short_b[l], hy_skip[l],
                                 deltas)
            yc_na = _context_attention(pc, off_q=off_na // dh, off_k=off_kv // dh, off_v=(off_kv + w_na) // dh,
                                       heads=heads)
            yc_cf = _conformer(pc, cf_dw_w[l], cf_dw_b[l], cf_norm_g[l], cf_norm_b[l], off=off_conf // w_cf)
            yc = _matmul_concat3(yc_hy.reshape(bsz * lc, w_hy), yc_na.reshape(bsz * lc, w_na),
                                 yc_cf.reshape(bsz * lc, w_cf), w_out, l).reshape(bsz, lc, d)
            ctx, hc = _ln_residual(ctx, yc, mod_ctx(l, 2), ln_mix_g[l], ln_mix_b[l], alpha, mod_ctx(l, 3),
                                   mod_ctx(l, 4))
            ac = _matmul_swiglu_in(hc.reshape(bsz * lc, d), w_ffn_gate, w_ffn_up, l)
            fc = _matmul_down(ac, w_ffn_down, l).reshape(bsz, lc, d)
            ctx, hc = _ln_residual(ctx, fc, mod_ctx(l, 5), ln_ffn_g[l], ln_ffn_b[l], alpha, mod_ctx(l + 1, 0),
                                   mod_ctx(l + 1, 1))

        a = _matmul_swiglu_in(h.reshape(bsz * s, d), w_ffn_gate, w_ffn_up, l)
        f = _matmul_down(a, w_ffn_down, l).reshape(bsz, s, d)
        if last:
            x, _ = _ln_residual(x, f, mod_lat(l, 5), ln_ffn_g[l], ln_ffn_b[l], alpha)
        else:
            x, h = _ln_residual(x, f, mod_lat(l, 5), ln_ffn_g[l], ln_ffn_b[l], alpha, mod_lat(l + 1, 0),
                                mod_lat(l + 1, 1))
    return x
```

```python
import functools
import math

import numpy as np
import jax
import jax.numpy as jnp
from jax import lax
from jax.experimental import pallas as pl
from jax.experimental.pallas import tpu as pltpu

F32 = jnp.float32
BF16 = jnp.bfloat16

GRID_W = 64
NA_HEAD_DIM = 128
NA_KH = 8
NA_KW = 16
NA_QROWS = 4
ROPE_THETA = 10000.0
HYENA_EMB_DIM = 33
HYENA_BLOCK = 1024
HYENA_TARGET = 1e-2
HYENA_MIN_DECAY = math.log(HYENA_TARGET) / 1.5
HYENA_MAX_DECAY = math.log(HYENA_TARGET) / 0.3
N_MOD = 6
LN_EPS = 1e-5
MASK_VALUE = -1e30

V7X_VMEM_LIMIT_BYTES = 56 * 1024 * 1024
V7X_VMEM_LIMIT_DOWN_BYTES = 60 * 1024 * 1024
SUBLANES = 8
CONV_HALO_ROWS = 16


def _cparams(*sem, vmem=V7X_VMEM_LIMIT_BYTES):
    return pltpu.CompilerParams(dimension_semantics=sem, vmem_limit_bytes=vmem)


def _tile(dim, pref, align=128):
    t = (min(dim, pref) // align) * align
    while t >= align:
        if dim % t == 0:
            return t
        t -= align
    return dim


def _sigmoid(z):
    return jax.nn.sigmoid(z)


def _mod_body(cond_ref, wd_ref, wu_ref, b_ref, o_ref, t_ref):
    @pl.when(pl.program_id(1) == 0)
    def _():
        cnd = cond_ref[...]
        t_ref[...] = jnp.dot(cnd * _sigmoid(cnd), wd_ref[...], preferred_element_type=F32)

    o_ref[...] = jnp.dot(t_ref[...], wu_ref[...], preferred_element_type=F32) + b_ref[...]


def _modulation_all(cond8, w_down, w_up, b_mod):
    depth, d, rank = w_down.shape
    n_out = w_up.shape[2]
    tn = _tile(n_out, 2048)
    return pl.pallas_call(
        _mod_body,
        grid=(depth, n_out // tn),
        in_specs=[
            pl.BlockSpec((8, d), lambda l, j: (0, 0)),
            pl.BlockSpec((None, d, rank), lambda l, j: (l, 0, 0)),
            pl.BlockSpec((None, rank, tn), lambda l, j: (l, 0, j)),
            pl.BlockSpec((None, 1, tn), lambda l, j: (l, 0, j)),
        ],
        out_specs=pl.BlockSpec((None, 8, tn), lambda l, j: (l, 0, j)),
        out_shape=jax.ShapeDtypeStruct((depth, 8, n_out), F32),
        scratch_shapes=[pltpu.VMEM((8, rank), F32)],
        compiler_params=_cparams("arbitrary", "arbitrary"),
        name="modulation",
    )(cond8, w_down, w_up, b_mod.reshape(depth, 1, n_out))


def _modulate_body(x_ref, shift_ref, scale_ref, o_ref):
    o_ref[...] = (x_ref[...] * (1.0 + scale_ref[...]) + shift_ref[...]).astype(o_ref.dtype)


def _modulate(x3, shift, scale):
    b, s, d = x3.shape
    ts = _tile(s, 256)
    tok = pl.BlockSpec((None, ts, d), lambda bi, i: (bi, i, 0))
    vec = pl.BlockSpec((None, 1, d), lambda bi, i: (bi, 0, 0))
    return pl.pallas_call(
        _modulate_body,
        grid=(b, s // ts),
        in_specs=[tok, vec, vec],
        out_specs=tok,
        out_shape=jax.ShapeDtypeStruct((b, s, d), BF16),
        compiler_params=_cparams("parallel", "parallel"),
        name="modulate",
    )(x3, shift, scale)


def _ln_body(*refs, alpha, with_h):
    if with_h:
        x_ref, y_ref, gate_ref, g_ref, b_ref, shift_ref, scale_ref, xo_ref, ho_ref = refs
    else:
        x_ref, y_ref, gate_ref, g_ref, b_ref, xo_ref = refs
    z = alpha * x_ref[...] + gate_ref[...] * y_ref[...].astype(F32)
    mu = jnp.mean(z, axis=-1, keepdims=True)
    zc = z - mu
    var = jnp.mean(zc * zc, axis=-1, keepdims=True)
    xn = zc * lax.rsqrt(var + LN_EPS) * g_ref[...] + b_ref[...]
    xo_ref[...] = xn
    if with_h:
        ho_ref[...] = (xn * (1.0 + scale_ref[...]) + shift_ref[...]).astype(ho_ref.dtype)


def _ln_residual(x3, y3, gate, g, b, alpha, shift=None, scale=None):
    bsz, s, d = x3.shape
    ts = _tile(s, 256)
    with_h = shift is not None
    tok = pl.BlockSpec((None, ts, d), lambda bi, i: (bi, i, 0))
    vec = pl.BlockSpec((None, 1, d), lambda bi, i: (bi, 0, 0))
    par = pl.BlockSpec((1, d), lambda bi, i: (0, 0))
    in_specs = [tok, tok, vec, par, par]
    args = [x3, y3, gate, g.reshape(1, d), b.reshape(1, d)]
    out_specs = [tok]
    out_shape = [jax.ShapeDtypeStruct((bsz, s, d), F32)]
    if with_h:
        in_specs += [vec, vec]
        args += [shift, scale]
        out_specs.append(tok)
        out_shape.append(jax.ShapeDtypeStruct((bsz, s, d), BF16))
    res = pl.pallas_call(
        functools.partial(_ln_body, alpha=alpha, with_h=with_h),
        grid=(bsz, s // ts),
        in_specs=in_specs,
        out_specs=out_specs,
        out_shape=out_shape,
        compiler_params=_cparams("parallel", "parallel"),
        name="ln_residual",
    )(*args)
    return (res[0], res[1]) if with_h else (res[0], None)


def _row_spec(tm, k, single_buffer):
    mode = dict(pipeline_mode=pl.Buffered(1)) if single_buffer else {}
    return pl.BlockSpec((tm, k), lambda i, j: (i, 0), **mode)


def _mm_tiles(m, ncols, tn_big=256):
    if m >= 2048:
        return _tile(m, 2048), _tile(ncols, tn_big)
    return m, _tile(ncols, 1024)


def _mm_body(a_ref, b_ref, o_ref):
    o_ref[...] = jnp.dot(a_ref[...], b_ref[...].astype(BF16), preferred_element_type=F32).astype(o_ref.dtype)


def _matmul(a, w, layer, *, col0=0, ncols=None, out_dtype=F32):
    m, k = a.shape
    ncols = w.shape[2] - col0 if ncols is None else ncols
    tm, tn = _mm_tiles(m, math.gcd(ncols, col0), tn_big=512)
    jb = col0 // tn
    return pl.pallas_call(
        _mm_body,
        grid=(m // tm, ncols // tn),
        in_specs=[
            _row_spec(tm, k, m > tm),
            pl.BlockSpec((None, k, tn), lambda i, j: (layer, 0, j + jb)),
        ],
        out_specs=pl.BlockSpec((tm, tn), lambda i, j: (i, j)),
        out_shape=jax.ShapeDtypeStruct((m, ncols), out_dtype),
        compiler_params=_cparams("parallel", "arbitrary"),
        name="matmul",
    )(a, w)


def _mm3_body(a1_ref, a2_ref, a3_ref, w_ref, o_ref):
    c1 = a1_ref.shape[1]
    c2 = a2_ref.shape[1]
    acc = jnp.dot(a1_ref[...], w_ref[0:c1, :].astype(BF16), preferred_element_type=F32)
    acc += jnp.dot(a2_ref[...], w_ref[c1:c1 + c2, :].astype(BF16), preferred_element_type=F32)
    acc += jnp.dot(a3_ref[...], w_ref[c1 + c2:, :].astype(BF16), preferred_element_type=F32)
    o_ref[...] = acc.astype(o_ref.dtype)


def _matmul_concat3(a1, a2, a3, w, layer):
    m = a1.shape[0]
    k, n = w.shape[1], w.shape[2]
    assert a1.shape[1] + a2.shape[1] + a3.shape[1] == k
    tm, tn = _mm_tiles(m, n, tn_big=512)
    return pl.pallas_call(
        _mm3_body,
        grid=(m // tm, n // tn),
        in_specs=[
            _row_spec(tm, a1.shape[1], m > tm),
            _row_spec(tm, a2.shape[1], m > tm),
            _row_spec(tm, a3.shape[1], m > tm),
            pl.BlockSpec((None, k, tn), lambda i, j: (layer, 0, j)),
        ],
        out_specs=pl.BlockSpec((tm, tn), lambda i, j: (i, j)),
        out_shape=jax.ShapeDtypeStruct((m, n), BF16),
        compiler_params=_cparams("parallel", "arbitrary"),
        name="matmul_out",
    )(a1, a2, a3, w)


def _gu_body(a_ref, wg_ref, wu_ref, o_ref):
    a = a_ref[...]
    g = jnp.dot(a, wg_ref[...].astype(BF16), preferred_element_type=F32)
    u = jnp.dot(a, wu_ref[...].astype(BF16), preferred_element_type=F32)
    o_ref[...] = (g * _sigmoid(g) * u).astype(o_ref.dtype)


def _matmul_swiglu_in(a, wg, wu, layer):
    m, k = a.shape
    n = wg.shape[2]
    tm, tn = _mm_tiles(m, n)
    tn = min(tn, _tile(n, 512))
    wspec = pl.BlockSpec((None, k, tn), lambda i, j: (layer, 0, j))
    return pl.pallas_call(
        _gu_body,
        grid=(m // tm, n // tn),
        in_specs=[_row_spec(tm, k, m > tm), wspec, wspec],
        out_specs=pl.BlockSpec((tm, tn), lambda i, j: (i, j)),
        out_shape=jax.ShapeDtypeStruct((m, n), BF16),
        compiler_params=_cparams("parallel", "arbitrary"),
        name="matmul_swiglu_in",
    )(a, wg, wu)


def _matmul_down(a, w, layer):
    m, k = a.shape
    n = w.shape[2]
    tm = _tile(m, 1024)
    tn = _tile(n, 256)
    return pl.pallas_call(
        _mm_body,
        grid=(m // tm, n // tn),
        in_specs=[_row_spec(tm, k, m > tm), pl.BlockSpec((None, k, tn), lambda i, j: (layer, 0, j))],
        out_specs=pl.BlockSpec((tm, tn), lambda i, j: (i, j)),
        out_shape=jax.ShapeDtypeStruct((m, n), BF16),
        compiler_params=_cparams("parallel", "arbitrary", vmem=V7X_VMEM_LIMIT_DOWN_BYTES),
        name="matmul_down",
    )(a, w)


def _rope_tables(s):
    t = jnp.arange(s)
    row = (t // GRID_W).astype(F32)[:, None]
    col = (t % GRID_W).astype(F32)[:, None]
    axis_dim = NA_HEAD_DIM // 2
    inv_freq = ROPE_THETA ** (-jnp.arange(0, axis_dim, 2, dtype=F32) / axis_dim)
    ang = jnp.concatenate([row * inv_freq, row * inv_freq, col * inv_freq, col * inv_freq], axis=-1)
    return jnp.cos(ang), jnp.sin(ang)


def _rotate_half_matrix():
    quarter = NA_HEAD_DIM // 4
    p = np.zeros((NA_HEAD_DIM, NA_HEAD_DIM), np.float32)
    for i in range(NA_HEAD_DIM):
        if (i % (2 * quarter)) < quarter:
            p[i + quarter, i] = -1.0
        else:
            p[i - quarter, i] = 1.0
    return jnp.asarray(p, BF16)


NA_SPAN = NA_QROWS + NA_KH - 1
NA_DROWS = 2 * NA_KH
NA_CLASS_BASES = (0, -(NA_KH // 2), -(NA_KH - 1))


def _bias_table_body(rpb_ref, onehot_ref, mask_ref, o_ref):
    o_ref[...] = jnp.dot(rpb_ref[...], onehot_ref[...], preferred_element_type=F32,
                         precision=lax.Precision.HIGHEST) + mask_ref[...]


def _na_bias_pairs(rpb):
    h, nr, nc = rpb.shape
    w = GRID_W
    cq = np.arange(w)[:, None]
    ck = np.arange(w)[None, :]
    d_col = np.clip(ck - cq, -(NA_KW - 1), NA_KW - 1) + (NA_KW - 1)
    col_start = np.clip(cq - NA_KW // 2, 0, w - NA_KW)
    col_ok = (ck >= col_start) & (ck < col_start + NA_KW)
    ncp = -(-nc // 8) * 8
    onehot = np.zeros((2 * ncp, w, 2, w), np.float32)
    for half in range(2):
        onehot[half * ncp + d_col, cq, half, ck] = 1.0
    maskadd = np.where(col_ok, 0.0, MASK_VALUE).astype(np.float32)
    maskadd = np.broadcast_to(maskadd[:, None, :], (w, 2, w)).reshape(1, 2 * w * w)
    rp = jnp.pad(rpb, ((0, 0), (1, NA_DROWS - nr), (0, ncp - nc)))
    rpb_pairs = jnp.concatenate([rp[:, :NA_DROWS], rp[:, 1:NA_DROWS + 1]], axis=-1).reshape(h * NA_DROWS, 2 * ncp)
    out = pl.pallas_call(
        _bias_table_body,
        out_shape=jax.ShapeDtypeStruct((h * NA_DROWS, 2 * w * w), F32),
        compiler_params=_cparams(),
        name="na_bias_table",
    )(rpb_pairs, jnp.asarray(onehot.reshape(2 * ncp, 2 * w * w)), jnp.asarray(maskadd))
    return out.reshape(h, NA_DROWS, w, 2 * w)


def _na_window_valid(cls, a, j):
    if j >= NA_SPAN:
        return False
    if cls == 0:
        return j < NA_KH
    if cls == 2:
        return j >= NA_SPAN - NA_KH
    return 0 <= NA_CLASS_BASES[1] + j - a + NA_KH // 2 < NA_KH


def _na_build_bias(pair_ref, bias_s):
    w = GRID_W
    left = lax.broadcasted_iota(jnp.int32, (w, 2 * w), 1) < w
    for cls, base in enumerate(NA_CLASS_BASES):
        for a in range(NA_QROWS):
            for pj in range(-(-NA_SPAN // 2)):
                j0 = 2 * pj
                v0, v1 = _na_window_valid(cls, a, j0), _na_window_valid(cls, a, j0 + 1)
                width = 2 * w if j0 + 1 < NA_SPAN else w
                if v0 or v1:
                    tile = pair_ref[base + j0 - a + NA_KH]
                    if not v1:
                        tile = jnp.where(left, tile, MASK_VALUE)
                    elif not v0:
                        tile = jnp.where(left, MASK_VALUE, tile)
                else:
                    tile = jnp.full((w, 2 * w), MASK_VALUE, F32)
                bias_s[cls, a * w:(a + 1) * w, j0 * w:j0 * w + width] = tile[:, :width]


def _na_body(q_ref, k_ref, v_ref, kc_ref, vc_ref, pair_ref, cos_ref, sin_ref, rot_ref, o_ref,
             qr_s, kr_s, qb_s, vb_s, bias_s, *, rows, scale):
    @pl.when(pl.program_id(1) == 0)
    def _():
        _na_build_bias(pair_ref, bias_s)

    cos = cos_ref[...]
    sin = sin_ref[...]

    def rope(z, zb):
        return z * cos + jnp.dot(zb, rot_ref[...], preferred_element_type=F32) * sin

    q = q_ref[...] * scale
    qb = q.astype(BF16)
    qb_s[...] = qb
    qr_s[...] = rope(q, qb).astype(BF16)
    k = k_ref[...]
    kr_s[...] = rope(k, k.astype(BF16)).astype(BF16)
    vb_s[...] = v_ref[...].astype(BF16)
    kc = kc_ref[...].astype(BF16)
    vc = vc_ref[...].astype(BF16)
    nt = (((1,), (1,)), ((), ()))
    groups = rows // NA_QROWS
    qn = NA_QROWS * GRID_W
    kn = NA_SPAN * GRID_W

    def body(g, carry):
        ks = jnp.clip(g * NA_QROWS + NA_CLASS_BASES[1], 0, rows - NA_SPAN)
        cls = jnp.where(g == 0, 0, jnp.where(g == groups - 1, 2, 1))
        q0 = pl.multiple_of(g * qn, qn)
        k0 = pl.multiple_of(ks * GRID_W, GRID_W)
        s_win = lax.dot_general(qr_s[pl.ds(q0, qn), :], kr_s[pl.ds(k0, kn), :], nt,
                                preferred_element_type=F32) + bias_s[cls]
        s_ctx = lax.dot_general(qb_s[pl.ds(q0, qn), :], kc, nt, preferred_element_type=F32)
        m = jnp.maximum(jnp.max(s_win, axis=-1, keepdims=True), jnp.max(s_ctx, axis=-1, keepdims=True))
        e_win = jnp.exp(s_win - m)
        e_ctx = jnp.exp(s_ctx - m)
        den = jnp.sum(e_win, axis=-1, keepdims=True) + jnp.sum(e_ctx, axis=-1, keepdims=True)
        o = jnp.dot(e_win.astype(BF16), vb_s[pl.ds(k0, kn), :], preferred_element_type=F32)
        o += jnp.dot(e_ctx.astype(BF16), vc, preferred_element_type=F32)
        o_ref[pl.ds(q0, qn), :] = (o / den).astype(o_ref.dtype)
        return carry

    lax.fori_loop(0, groups, body, 0, unroll=4)


def _neighbourhood_attention(p3, kvc3, bias_pairs, cos, sin, *, off_q, off_k, off_v, off_kc, off_vc, heads):
    bsz, s, _ = p3.shape
    lc = kvc3.shape[1]
    rows = s // GRID_W
    assert s % GRID_W == 0 and rows % NA_QROWS == 0 and rows // NA_QROWS >= 3
    dh = NA_HEAD_DIM

    def col(off):
        return pl.BlockSpec((None, s, dh), lambda h, b: (b, 0, off + h))

    def colc(off):
        return pl.BlockSpec((None, lc, dh), lambda h, b: (b, 0, off + h))

    tab = pl.BlockSpec((s, dh), lambda h, b: (0, 0))
    return pl.pallas_call(
        functools.partial(_na_body, rows=rows, scale=dh ** -0.5),
        grid=(heads, bsz),
        in_specs=[col(off_q), col(off_k), col(off_v), colc(off_kc), colc(off_vc),
                  pl.BlockSpec((None, NA_DROWS, GRID_W, 2 * GRID_W), lambda h, b: (h, 0, 0, 0)), tab, tab,
                  pl.BlockSpec((dh, dh), lambda h, b: (0, 0))],
        out_specs=pl.BlockSpec((None, s, dh), lambda h, b: (b, 0, h)),
        out_shape=jax.ShapeDtypeStruct((bsz, s, heads * dh), BF16),
        scratch_shapes=[pltpu.VMEM((s, dh), BF16)] * 4
        + [pltpu.VMEM((len(NA_CLASS_BASES), NA_QROWS * GRID_W, NA_SPAN * GRID_W), F32)],
        compiler_params=_cparams("arbitrary", "arbitrary"),
        name="neighbourhood_attention",
    )(p3, p3, p3, kvc3, kvc3, bias_pairs, cos, sin, _rotate_half_matrix())


def _ctx_attn_body(q_ref, k_ref, v_ref, o_ref, *, scale):
    nt = (((1,), (1,)), ((), ()))
    s = lax.dot_general(q_ref[...].astype(BF16), k_ref[...].astype(BF16), nt, preferred_element_type=F32) * scale
    m = jnp.max(s, axis=-1, keepdims=True)
    e = jnp.exp(s - m)
    den = jnp.sum(e, axis=-1, keepdims=True)
    o = jnp.dot(e.astype(BF16), v_ref[...].astype(BF16), preferred_element_type=F32)
    o_ref[...] = (o / den).astype(o_ref.dtype)


def _context_attention(pc3, *, off_q, off_k, off_v, heads):
    bsz, lc, _ = pc3.shape
    dh = NA_HEAD_DIM

    def col(off):
        return pl.BlockSpec((None, lc, dh), lambda b, h: (b, 0, off + h))

    return pl.pallas_call(
        functools.partial(_ctx_attn_body, scale=dh ** -0.5),
        grid=(bsz, heads),
        in_specs=[col(off_q), col(off_k), col(off_v)],
        out_specs=pl.BlockSpec((None, lc, dh), lambda b, h: (b, 0, h)),
        out_shape=jax.ShapeDtypeStruct((bsz, lc, heads * dh), BF16),
        compiler_params=_cparams("parallel", "parallel"),
        name="context_attention",
    )(pc3, pc3, pc3)


def _conformer_body(a_ref, g_ref, pa_ref, pg_ref, na_ref, ng_ref, w_ref, wb_ref, lg_ref, lb_ref, o_ref, u_s, sh_s,
                    *, ts, taps):
    i = pl.program_id(1)
    halo = CONV_HALO_ROWS
    a = a_ref[...]
    g = g_ref[...]
    u_s[halo:halo + ts, :] = a * _sigmoid(g)
    prev = pa_ref[...] * _sigmoid(pg_ref[...])
    u_s[0:halo, :] = jnp.where(i > 0, prev, 0.0)
    nxt = na_ref[...] * _sigmoid(ng_ref[...])
    u_s[halo + ts:, :] = jnp.where(i < pl.num_programs(1) - 1, nxt, 0.0)
    span = ts + 2 * halo - SUBLANES
    for r in range(SUBLANES):
        sh_s[r] = u_s[r:r + span, :]
    base = halo - taps // 2
    z = wb_ref[...]
    for t in range(taps):
        a8, r = divmod(base + t, SUBLANES)
        z += w_ref[t:t + 1, :] * sh_s[r, a8 * SUBLANES:a8 * SUBLANES + ts, :]
    mu = jnp.mean(z, axis=-1, keepdims=True)
    zc = z - mu
    var = jnp.mean(zc * zc, axis=-1, keepdims=True)
    zn = zc * lax.rsqrt(var + LN_EPS) * lg_ref[...] + lb_ref[...]
    o_ref[...] = (zn * _sigmoid(zn)).astype(o_ref.dtype)


def _conformer(p3, dw_w, dw_b, ln_g, ln_b, *, off):
    bsz, l, _ = p3.shape
    taps, c = dw_w.shape
    halo = CONV_HALO_ROWS
    assert taps // 2 < halo and l % halo == 0
    ts = _tile(l, 256)
    hb = ts // halo
    nh = l // halo

    def cur(o):
        return pl.BlockSpec((None, ts, c), lambda b, i: (b, i, o))

    def prev(o):
        return pl.BlockSpec((None, halo, c), lambda b, i: (b, jnp.maximum(i * hb - 1, 0), o))

    def nxt(o):
        return pl.BlockSpec((None, halo, c), lambda b, i: (b, jnp.minimum((i + 1) * hb, nh - 1), o))

    par = pl.BlockSpec((1, c), lambda b, i: (0, 0))
    return pl.pallas_call(
        functools.partial(_conformer_body, ts=ts, taps=taps),
        grid=(bsz, l // ts),
        in_specs=[cur(off), cur(off + 1), prev(off), prev(off + 1), nxt(off), nxt(off + 1),
                  pl.BlockSpec((taps, c), lambda b, i: (0, 0)), par, par, par],
        out_specs=pl.BlockSpec((None, ts, c), lambda b, i: (b, i, 0)),
        out_shape=jax.ShapeDtypeStruct((bsz, l, c), BF16),
        scratch_shapes=[pltpu.VMEM((ts + 2 * halo, c), F32),
                        pltpu.VMEM((SUBLANES, ts + 2 * halo - SUBLANES, c), F32)],
        compiler_params=_cparams("parallel", "arbitrary"),
        name="conformer",
    )(p3, p3, p3, p3, p3, p3, dw_w, dw_b.reshape(1, c), ln_g.reshape(1, c), ln_b.reshape(1, c))


def _hy_short_body(cur_ref, prev_ref, next_ref, w_ref, b_ref, v_ref, x1_ref, x2_ref, u_s, *, ts, c):
    i = pl.program_id(1)
    halo = CONV_HALO_ROWS
    u_s[halo:halo + ts, :] = cur_ref[...]
    u_s[0:halo, :] = jnp.where(i > 0, prev_ref[...], 0.0)
    u_s[halo + ts:, :] = jnp.where(i < pl.num_programs(1) - 1, next_ref[...], 0.0)
    z = (b_ref[...] + w_ref[0:1, :] * u_s[halo - 1:halo - 1 + ts, :] + w_ref[1:2, :] * u_s[halo:halo + ts, :]
         + w_ref[2:3, :] * u_s[halo + 1:halo + 1 + ts, :])
    v_ref[...] = z[:, 0:c].astype(v_ref.dtype)
    x1_ref[...] = z[:, c:2 * c]
    x2_ref[...] = z[:, 2 * c:3 * c]


def _hyena_short_conv(p3, short_w, short_b, c):
    bsz, l, _ = p3.shape
    halo = CONV_HALO_ROWS
    ts = _tile(l, 512)
    hb = ts // halo
    nh = l // halo
    w3 = 3 * c
    out = pl.BlockSpec((ts, c), lambda b, i: (i, b))
    return pl.pallas_call(
        functools.partial(_hy_short_body, ts=ts, c=c),
        grid=(bsz, l // ts),
        in_specs=[
            pl.BlockSpec((None, ts, w3), lambda b, i: (b, i, 0)),
            pl.BlockSpec((None, halo, w3), lambda b, i: (b, jnp.maximum(i * hb - 1, 0), 0)),
            pl.BlockSpec((None, halo, w3), lambda b, i: (b, jnp.minimum((i + 1) * hb, nh - 1), 0)),
            pl.BlockSpec((3, w3), lambda b, i: (0, 0)),
            pl.BlockSpec((1, w3), lambda b, i: (0, 0)),
        ],
        out_specs=[out, out, out],
        out_shape=[jax.ShapeDtypeStruct((l, bsz * c), BF16), jax.ShapeDtypeStruct((l, bsz * c), F32),
                   jax.ShapeDtypeStruct((l, bsz * c), F32)],
        scratch_shapes=[pltpu.VMEM((ts + 2 * halo, w3), F32)],
        compiler_params=_cparams("parallel", "arbitrary"),
        name="hyena_short_conv",
    )(p3, p3, p3, short_w, short_b.reshape(1, w3))


def _hyena_embedding(l):
    t = jnp.linspace(0.0, 1.0, l, dtype=F32)[:, None]
    bands = (HYENA_EMB_DIM - 1) // 2
    omega = (2.0 * math.pi / l) * jnp.arange(l, dtype=F32)[:, None]
    f = jnp.linspace(1e-4, bands - 1, bands, dtype=F32)[None, :]
    return jnp.concatenate([t, jnp.cos(f * omega), -jnp.sin(f * omega)], axis=-1)


def _hyena_embedding_pair(l):
    emb = _hyena_embedding(l)
    return jnp.stack([jnp.concatenate([emb[0:1], jnp.flip(emb[1:], axis=0)], axis=0), emb])


def _hy_filter_body(emb_ref, embt_ref, w1_ref, b1_ref, w2_ref, b2_ref, w3_ref, b3_ref, freq_ref, w4_ref, delta_ref,
                    g_ref, hb0_ref, hid_s):
    dirn = pl.program_id(2)
    first = jnp.logical_and(jnp.logical_and(pl.program_id(0) == 0, pl.program_id(1) == 0), dirn == 0)
    tn_dims = (((0,), (0,)), ((), ()))

    @pl.when(first)
    def _():
        freq = freq_ref[...]
        for d in range(2):
            hid = lax.dot_general(w1_ref[...], embt_ref[d], tn_dims, preferred_element_type=F32)
            hid = jnp.sin(freq * (hid + b1_ref[...]))
            hid = jnp.sin(freq * (lax.dot_general(w2_ref[...], hid, tn_dims, preferred_element_type=F32) + b2_ref[...]))
            hid_s[d] = jnp.sin(freq * (lax.dot_general(w3_ref[...], hid, tn_dims, preferred_element_type=F32)
                                       + b3_ref[...]))

    t = emb_ref[dirn][:, 0:1]
    h = lax.dot_general(hid_s[dirn], w4_ref[...], tn_dims, preferred_element_type=F32)
    h = h * jnp.exp(-t * delta_ref[...])
    h = h / (jnp.sum(jnp.abs(h), axis=0, keepdims=True) + 1e-6)
    is_bwd = dirn == 0

    @pl.when(is_bwd)
    def _():
        hb0_ref[...] = h[0:1, :].astype(hb0_ref.dtype)

    row = lax.broadcasted_iota(jnp.int32, (h.shape[0], 1), 0)
    g_ref[...] = jnp.where(jnp.logical_and(row == 0, is_bwd), 0.0, h).astype(g_ref.dtype)


def _hyena_filters(emb2, w1, b1, w2, b2, w3, b3, w4, freq, deltas, order):
    _, l, e = emb2.shape
    hf = w1.shape[1]
    c = w4.shape[1] // (2 * order)
    tn = _tile(c, 512)
    cb = c // tn

    def full(shape):
        return pl.BlockSpec(shape, lambda o, ct, dn: (0,) * len(shape))

    return pl.pallas_call(
        _hy_filter_body,
        grid=(order, cb, 2),
        in_specs=[full((2, l, e)), full((2, e, l)), full((e, hf)), full((hf, 1)), full((hf, hf)), full((hf, 1)),
                  full((hf, hf)), full((hf, 1)), full((hf, 1)),
                  pl.BlockSpec((hf, tn), lambda o, ct, dn: (0, (2 * o + 1 - dn) * cb + ct)),
                  pl.BlockSpec((1, tn), lambda o, ct, dn: (0, ct))],
        out_specs=[pl.BlockSpec((None, l, tn), lambda o, ct, dn: (o, dn, ct)),
                   pl.BlockSpec((None, 1, tn), lambda o, ct, dn: (o, 0, ct))],
        out_shape=[jax.ShapeDtypeStruct((order, 2 * l, c), BF16), jax.ShapeDtypeStruct((order, 1, c), BF16)],
        scratch_shapes=[pltpu.VMEM((2, hf, l), F32)],
        compiler_params=_cparams("arbitrary", "arbitrary", "arbitrary"),
        name="hyena_filters",
    )(emb2, jnp.swapaxes(emb2, 1, 2), w1, b1.reshape(hf, 1), w2, b2.reshape(hf, 1), w3, b3.reshape(hf, 1),
      freq.reshape(hf, 1), w4, deltas)


def _dft_table(l, tm):
    n = 2 * l
    half = tm // 2
    r = np.arange(n)
    f_np = (r // tm) * half + (r % half)
    is_im_np = (r % tm) >= half
    sq = 1 << (int(math.log2(l)) // 2)
    hi = l // sq
    f = jnp.asarray(f_np, jnp.int32)[:, None]
    w0 = 2.0 * math.pi / n
    a_hi = ((f * (jnp.arange(hi, dtype=jnp.int32)[None, :] * sq)) % n).astype(F32) * w0
    a_lo = ((f * jnp.arange(sq, dtype=jnp.int32)[None, :]) % n).astype(F32) * w0
    ch, sh = jnp.cos(a_hi)[:, :, None], jnp.sin(a_hi)[:, :, None]
    cl, sl = jnp.cos(a_lo)[:, None, :], jnp.sin(a_lo)[:, None, :]
    cosv = (ch * cl - sh * sl).reshape(n, l)
    sinv = (sh * cl + ch * sl).reshape(n, l)
    nyq = jnp.asarray(np.where(np.arange(l) % 2 == 0, 1.0, -1.0), F32)[None, :]
    is_im = jnp.asarray(is_im_np)[:, None]
    table = jnp.where(is_im, jnp.where(f == 0, nyq, -sinv), cosv)
    return table.astype(BF16)


def _spec_body(a_ref, blk_ref, skip_ref, hb0_ref, gr_ref, gi_ref, prev_s, c_s, *, n, lag0):
    i = pl.program_id(0)
    b = pl.program_id(3)
    half = a_ref.shape[0] // 2
    blk = blk_ref[...]
    cur = jnp.dot(a_ref[...], blk, preferred_element_type=F32)

    @pl.when(b > 0)
    def _():
        sp_ = prev_s[...]
        c_prev = c_s[...]
        row = lax.broadcasted_iota(jnp.int32, (half, 1), 0)
        is_dc = jnp.logical_and(row == 0, i == 0)
        sigma = (1 - 2 * (row % 2)).astype(F32)
        gr = cur[:half] + sigma * (sp_[:half] - c_prev)
        gi = jnp.where(is_dc, cur[half:] + sp_[half:] - c_prev, cur[half:] + sigma * sp_[half:])
        add = jnp.where(b - 1 == lag0, skip_ref[...] + hb0_ref[...].astype(F32), 0.0)
        gr = gr + add
        gi = jnp.where(is_dc, gi + add, gi)
        w = jnp.where(is_dc, 1.0 / n, 2.0 / n)
        gr_ref[...] = gr * w
        gi_ref[...] = gi * w

    prev_s[...] = cur
    c_s[...] = blk[0:1, :].astype(F32)


def _hyena_filter_spectrum(table, gseq, skip, hb0, tm):
    n, t = table.shape
    order, l2, c = gseq.shape
    nblk = l2 // t
    nlag = nblk - 1
    tn = _tile(c, 1024)
    half = tm // 2
    out = pl.BlockSpec((None, None, half, tn), lambda i, o, j, b: (o, jnp.maximum(b - 1, 0), i, j))
    vec = pl.BlockSpec((None, 1, tn), lambda i, o, j, b: (o, 0, j))
    return pl.pallas_call(
        functools.partial(_spec_body, n=n, lag0=nblk // 2 - 1),
        grid=(n // tm, order, c // tn, nblk),
        in_specs=[
            pl.BlockSpec((tm, t), lambda i, o, j, b: (i, 0)),
            pl.BlockSpec((None, t, tn), lambda i, o, j, b: (o, b, j)),
            vec, vec,
        ],
        out_specs=[out, out],
        out_shape=[jax.ShapeDtypeStruct((order, nlag, t, c), F32)] * 2,
        scratch_shapes=[pltpu.VMEM((tm, tn), F32), pltpu.VMEM((1, tn), F32)],
        compiler_params=_cparams("arbitrary", "arbitrary", "arbitrary", "arbitrary"),
        name="hyena_filter_spectrum",
    )(table, gseq, skip.reshape(order, 1, c), hb0)


def _fwd_body(a_ref, u_ref, o_ref):
    o_ref[...] = jnp.dot(a_ref[...], u_ref[...], preferred_element_type=F32)


def _hyena_fwd(table, u, tm):
    n, t = table.shape
    l, ncol = u.shape
    tn = _tile(ncol, 1024)
    return pl.pallas_call(
        _fwd_body,
        grid=(n // tm, l // t, ncol // tn),
        in_specs=[pl.BlockSpec((tm, t), lambda i, jb, cc: (i, 0)),
                  pl.BlockSpec((t, tn), lambda i, jb, cc: (jb, cc))],
        out_specs=pl.BlockSpec((None, tm, tn), lambda i, jb, cc: (jb, i, cc)),
        out_shape=jax.ShapeDtypeStruct((l // t, n, ncol), F32),
        compiler_params=_cparams("parallel", "arbitrary", "arbitrary"),
        name="hyena_fwd_dft",
    )(table, u)


def _mix_body(u_ref, gr_ref, gi_ref, o_ref, *, nblk, half, chunk):
    i = pl.program_id(0)

    def rows(r0, size):
        re = pl.ds(r0, size)
        im = pl.ds(half + r0, size)
        for io in range(nblk):
            acc_r = None
            acc_i = None
            for jj in range(nblk):
                lag = io - jj + nblk - 1
                ur, ui = u_ref[jj, re, :], u_ref[jj, im, :]
                gr, gi = gr_ref[lag, pl.ds(r0, size), :], gi_ref[lag, pl.ds(r0, size), :]
                pr = ur * gr - ui * gi
                pi = ur * gi + ui * gr
                acc_r = pr if acc_r is None else acc_r + pr
                acc_i = pi if acc_i is None else acc_i + pi
            o_ref[io, re, :] = acc_r.astype(o_ref.dtype)
            o_ref[io, im, :] = acc_i.astype(o_ref.dtype)

    def body(rc, carry):
        rows(pl.multiple_of(rc * chunk, chunk), chunk)
        return carry

    lax.fori_loop(0, half // chunk, body, 0)

    @pl.when(i == 0)
    def _():
        for io in range(nblk):
            dc = None
            ny = None
            for jj in range(nblk):
                lag = io - jj + nblk - 1
                pr = u_ref[jj, 0:1, :] * gr_ref[lag, 0:1, :]
                pi = u_ref[jj, half:half + 1, :] * gi_ref[lag, 0:1, :]
                dc = pr if dc is None else dc + pr
                ny = pi if ny is None else ny + pi
            o_ref[io, 0:1, :] = dc.astype(o_ref.dtype)
            o_ref[io, half:half + 1, :] = ny.astype(o_ref.dtype)


def _hyena_mix(u, gr, gi, order, tm):
    nblk, n, ncol = u.shape
    c = gr.shape[3]
    tc = _tile(c, 256)
    cb = c // tc
    half = tm // 2
    chunk = min(64, half)
    gspec = pl.BlockSpec((None, 2 * nblk - 1, half, tc), lambda i, cc: (order, 0, i, cc % cb))
    return pl.pallas_call(
        functools.partial(_mix_body, nblk=nblk, half=half, chunk=chunk),
        grid=(n // tm, ncol // tc),
        in_specs=[pl.BlockSpec((nblk, tm, tc), lambda i, cc: (0, i, cc)), gspec, gspec],
        out_specs=pl.BlockSpec((nblk, tm, tc), lambda i, cc: (0, i, cc)),
        out_shape=jax.ShapeDtypeStruct((nblk, n, ncol), BF16),
        compiler_params=_cparams("parallel", "arbitrary"),
        name="hyena_block_mix",
    )(u, gr, gi)


def _fwd_mix_body(a_ref, x_ref, gr_ref, gi_ref, o_ref, u_s, *, nblk, half, chunk, t):
    a = a_ref[...]
    for jb in range(nblk):
        u_s[jb] = jnp.dot(a, x_ref[jb * t:(jb + 1) * t, :], preferred_element_type=F32)
    _mix_body(u_s, gr_ref, gi_ref, o_ref, nblk=nblk, half=half, chunk=chunk)


def _hyena_fwd_mix(table, x, gr, gi, order, tm):
    n, t = table.shape
    l, ncol = x.shape
    nblk = l // t
    c = gr.shape[3]
    tc = _tile(c, 256)
    cb = c // tc
    half = tm // 2
    chunk = min(64, half)
    gspec = pl.BlockSpec((None, 2 * nblk - 1, half, tc), lambda i, cc: (order, 0, i, cc % cb))
    return pl.pallas_call(
        functools.partial(_fwd_mix_body, nblk=nblk, half=half, chunk=chunk, t=t),
        grid=(n // tm, ncol // tc),
        in_specs=[pl.BlockSpec((tm, t), lambda i, cc: (i, 0)), pl.BlockSpec((l, tc), lambda i, cc: (0, cc)),
                  gspec, gspec],
        out_specs=pl.BlockSpec((nblk, tm, tc), lambda i, cc: (0, i, cc)),
        out_shape=jax.ShapeDtypeStruct((nblk, n, ncol), BF16),
        scratch_shapes=[pltpu.VMEM((nblk, tm, tc), F32)],
        compiler_params=_cparams("parallel", "arbitrary"),
        name="hyena_fwd_mix",
    )(table, x, gr, gi)


def _inv_body(a_ref, y_ref, x_ref, o_ref, acc_ref):
    k_axis = 3
    kk = pl.program_id(k_axis)
    tn_dims = (((0,), (0,)), ((), ()))
    prod = lax.dot_general(a_ref[...], y_ref[...], tn_dims, preferred_element_type=F32)

    @pl.when(kk == 0)
    def _():
        acc_ref[...] = prod

    @pl.when(kk > 0)
    def _():
        acc_ref[...] += prod

    @pl.when(kk == pl.num_programs(k_axis) - 1)
    def _():
        o_ref[...] = (x_ref[...] * acc_ref[...]).astype(o_ref.dtype)


def _hyena_inv_gate(table, y, xg, c, *, token_major):
    n, t = table.shape
    nblk, _, ncol = y.shape
    l = nblk * t
    tmt = _tile(t, 1024)
    tb = t // tmt
    tk = _tile(n, 2048)
    tn = c if token_major else _tile(c, 1024)
    if token_major:
        out_spec = pl.BlockSpec((None, tmt, c), lambda io, ti, j, kk: (j, io * tb + ti, 0))
        out_shape = jax.ShapeDtypeStruct((ncol // c, l, c), BF16)
    else:
        out_spec = pl.BlockSpec((tmt, tn), lambda io, ti, j, kk: (io * tb + ti, j))
        out_shape = jax.ShapeDtypeStruct((l, ncol), BF16)
    return pl.pallas_call(
        _inv_body,
        grid=(nblk, tb, ncol // tn, n // tk),
        in_specs=[pl.BlockSpec((tk, tmt), lambda io, ti, j, kk: (kk, ti)),
                  pl.BlockSpec((None, tk, tn), lambda io, ti, j, kk: (io, kk, j)),
                  pl.BlockSpec((tmt, tn), lambda io, ti, j, kk: (io * tb + ti, j))],
        out_specs=out_spec,
        out_shape=out_shape,
        scratch_shapes=[pltpu.VMEM((tmt, tn), F32)],
        compiler_params=_cparams("parallel", "parallel", "parallel", "arbitrary"),
        name="hyena_inv_dft",
    )(table, y, xg)


def _hyena_group(p3, table, tm, emb2, filt_w, short_w, short_b, skip, deltas):
    order, c = skip.shape
    gseq, hb0 = _hyena_filters(emb2, *filt_w, deltas, order)
    gr, gi = _hyena_filter_spectrum(table, gseq, skip, hb0, tm)
    v, x1, x2 = _hyena_short_conv(p3, short_w, short_b, c)
    y1 = _hyena_fwd_mix(table, v, gr, gi, 0, tm)
    z1 = _hyena_inv_gate(table, y1, x1, c, token_major=False)
    y2 = _hyena_fwd_mix(table, z1, gr, gi, 1, tm)
    return _hyena_inv_gate(table, y2, x2, c, token_major=True)


def kernel(x, c, ctx, c_ctx, w_mod_down, w_mod_up, b_mod, w_in, w_out, hy_short_w, hy_short_b, hy_filt_w1, hy_filt_b1, hy_filt_w2, hy_filt_b2, hy_filt_w3, hy_filt_b3, hy_filt_w4, hy_filt_freq, hy_skip, na_rpb, cf_dw_w, cf_dw_b, cf_norm_g, cf_norm_b, w_ffn_gate, w_ffn_up, w_ffn_down, ln_mix_g, ln_mix_b, ln_ffn_g, ln_ffn_b):
    bsz, s, d = x.shape
    lc = ctx.shape[1]
    depth = w_in.shape[0]
    in_cols = w_in.shape[2]
    w_hy = hy_skip.shape[2]
    w_cf = cf_dw_w.shape[2]
    w_na = (in_cols - 3 * w_hy - 2 * w_cf) // 3
    heads = w_na // NA_HEAD_DIM
    off_na = 3 * w_hy
    off_kv = off_na + w_na
    off_conf = off_na + 3 * w_na
    alpha = (2 * depth) ** 0.25
    dh = NA_HEAD_DIM
    assert bsz + 1 <= 8 and off_conf % w_cf == 0 and off_na % dh == 0

    cos, sin = _rope_tables(s)
    t_lat = min(HYENA_BLOCK, s)
    t_ctx = min(HYENA_BLOCK, lc)
    assert s % t_lat == 0 and lc % t_ctx == 0
    tm_lat = min(1024, 2 * t_lat)
    tm_ctx = min(1024, 2 * t_ctx)
    table_lat = _dft_table(t_lat, tm_lat)
    table_ctx = _dft_table(t_ctx, tm_ctx)
    emb_lat = _hyena_embedding_pair(s)
    emb_ctx = _hyena_embedding_pair(lc)
    deltas = jnp.abs(jnp.linspace(HYENA_MIN_DECAY, HYENA_MAX_DECAY, w_hy, dtype=F32))
    deltas = deltas[None, :]

    cond8 = jnp.zeros((8, d), F32).at[:bsz].set(c).at[bsz].set(c_ctx)
    mods = _modulation_all(cond8, w_mod_down, w_mod_up, b_mod).reshape(depth, 8, N_MOD, d)

    def mod_lat(l, i):
        return mods[l, :bsz, i][:, None, :]

    def mod_ctx(l, i):
        return jnp.broadcast_to(mods[l, bsz, i][None, None, :], (bsz, 1, d))

    h = _modulate(x, mod_lat(0, 0), mod_lat(0, 1))
    hc = _modulate(ctx, mod_ctx(0, 0), mod_ctx(0, 1))

    for l in range(depth):
        last = l == depth - 1
        filt_w = (hy_filt_w1[l], hy_filt_b1[l], hy_filt_w2[l], hy_filt_b2[l], hy_filt_w3[l], hy_filt_b3[l],
                  hy_filt_w4[l], hy_filt_freq[l])
        bias_cls = _na_bias_pairs(na_rpb[l])

        hc2 = hc.reshape(bsz * lc, d)
        if last:
            kvc = _matmul(hc2, w_in, l, col0=off_kv, ncols=2 * w_na).reshape(bsz, lc, 2 * w_na)
            off_kc, off_vc = 0, heads
        else:
            pc = _matmul(hc2, w_in, l).reshape(bsz, lc, in_cols)
            kvc = pc
            off_kc, off_vc = off_kv // dh, (off_kv + w_na) // dh

        p = _matmul(h.reshape(bsz * s, d), w_in, l).reshape(bsz, s, in_cols)
        y_hy = _hyena_group(p, table_lat, tm_lat, emb_lat, filt_w, hy_short_w[l], hy_short_b[l], hy_skip[l], deltas)
        y_na = _neighbourhood_attention(p, kvc, bias_cls, cos, sin, off_q=off_na // dh, off_k=off_kv // dh,
                                        off_v=(off_kv + w_na) // dh, off_kc=off_kc, off_vc=off_vc, heads=heads)
        y_cf = _conformer(p, cf_dw_w[l], cf_dw_b[l], cf_norm_g[l], cf_norm_b[l], off=off_conf // w_cf)
        y = _matmul_concat3(y_hy.reshape(bsz * s, w_hy), y_na.reshape(bsz * s, w_na), y_cf.reshape(bsz * s, w_cf),
                            w_out, l).reshape(bsz, s, d)
        x, h = _ln_residual(x, y, mod_lat(l, 2), ln_mix_g[l], ln_mix_b[l], alpha, mod_lat(l, 3), mod_lat(l, 4))

        if not last:
            yc_hy = _hyena_group(pc, table_ctx, tm_ctx, emb_ctx, filt_w, hy_short_w[l], hy_short_b[l], hy_skip[l],
                                 deltas)
            yc_na = _context_attention(pc, off_q=off_na // dh, off_k=off_kv // dh, off_v=(off_kv + w_na) // dh,
                                       heads=heads)
            yc_cf = _conformer(pc, cf_dw_w[l], cf_dw_b[l], cf_norm_g[l], cf_norm_b[l], off=off_conf // w_cf)
            yc = _matmul_concat3(yc_hy.reshape(bsz * lc, w_hy), yc_na.reshape(bsz * lc, w_na),
                                 yc_cf.reshape(bsz * lc, w_cf), w_out, l).reshape(bsz, lc, d)
            ctx, hc = _ln_residual(ctx, yc, mod_ctx(l, 2), ln_mix_g[l], ln_mix_b[l], alpha, mod_ctx(l, 3),
                                   mod_ctx(l, 4))
            ac = _matmul_swiglu_in(hc.reshape(bsz * lc, d), w_ffn_gate, w_ffn_up, l)
            fc = _matmul_down(ac, w_ffn_down, l).reshape(bsz, lc, d)
            ctx, hc = _ln_residual(ctx, fc, mod_ctx(l, 5), ln_ffn_g[l], ln_ffn_b[l], alpha, mod_ctx(l + 1, 0),
                                   mod_ctx(l + 1, 1))

        a = _matmul_swiglu_in(h.reshape(bsz * s, d), w_ffn_gate, w_ffn_up, l)
        f = _matmul_down(a, w_ffn_down, l).reshape(bsz, s, d)
        if last:
            x, _ = _ln_residual(x, f, mod_lat(l, 5), ln_ffn_g[l], ln_ffn_b[l], alpha)
        else:
            x, h = _ln_residual(x, f, mod_lat(l, 5), ln_ffn_g[l], ln_ffn_b[l], alpha, mod_lat(l + 1, 0),
                                mod_lat(l + 1, 1))
    return x
```
